```python
import jax, jax.numpy as jnp
from jax import lax
import numpy as np

D_MODEL = 4096
BATCH = 4
SEQ = 4096
DEPTH = 2

CTX_LEN = 256
GRID_W = 64
CHUNK = 64
CONV_W = 4
D_RNN = D_MODEL // 4
RNN_BLOCKS = 8
LRU_C = 8.0
GLA_HEADS = 4
GLA_DK = D_MODEL // 8 // GLA_HEADS
GLA_DV = D_MODEL // 4 // GLA_HEADS
GLA_RANK = 16
GLA_TAU = 16.0
MLSTM_HEADS = 4
MLSTM_D = D_MODEL // 4 // MLSTM_HEADS
N_BRANCH = 3
N_EXPERTS = 16
EC_CAPACITY = 2
MOE_FF = D_MODEL // 4
DEEPNORM_ALPHA = (2 * DEPTH) ** 0.25
DEEPNORM_BETA = (8 * DEPTH) ** -0.25
LN_EPS = 1e-5
GLA_K = GLA_HEADS * GLA_DK
GLA_V = GLA_HEADS * GLA_DV
MLSTM_W = MLSTM_HEADS * MLSTM_D
IN_SIZES = (D_RNN, D_RNN, GLA_K, GLA_K, GLA_V, GLA_V, 2 * GLA_RANK, MLSTM_W, MLSTM_W, MLSTM_W, MLSTM_W, 4 * MLSTM_HEADS)
N_FEAT = sum(IN_SIZES)
N_IN = N_FEAT + N_BRANCH * D_MODEL
IN_SPLIT = tuple(sum(IN_SIZES[:i + 1]) for i in range(len(IN_SIZES) - 1))

kernel_name = 'hybrid_rglru_gla_mlstm_ec_moe_dit_trunk'


def layer_norm(t, g, b):
    tf = t.astype(jnp.float32)
    mu = tf.mean(-1, keepdims=True)
    var = jnp.square(tf - mu).mean(-1, keepdims=True)
    return ((tf - mu) * lax.rsqrt(var + LN_EPS) * g + b).astype(t.dtype)


def head_norm(t, g):
    mu = t.mean(-1, keepdims=True)
    var = jnp.square(t - mu).mean(-1, keepdims=True)
    y = (t - mu) * lax.rsqrt(var + LN_EPS)
    return y.reshape(t.shape[0], t.shape[1], -1) * g


def centred_dwconv(t, w, b):
    k = w.shape[0]
    left = (k - 1) // 2
    y = lax.conv_general_dilated(t, w[:, None, :].astype(t.dtype), (1,), [(left, k - 1 - left)],
                                 dimension_numbers=('NWC', 'WIO', 'NWC'), feature_group_count=t.shape[-1])
    return y + b


def to_colmajor(t):
    bn, n, w = t.shape
    rows = n // GRID_W
    return t.reshape(bn, rows, GRID_W, w).transpose(0, 2, 1, 3).reshape(bn, n, w)


def from_colmajor(t):
    bn, n, w = t.shape
    rows = n // GRID_W
    return t.reshape(bn, GRID_W, rows, w).transpose(0, 2, 1, 3).reshape(bn, n, w)


def flip_parts(t, nc):
    return jnp.concatenate([jnp.flip(t[:, :nc], 1), jnp.flip(t[:, nc:], 1)], axis=1)


def block_diag(u, w):
    nb, bs, _ = w.shape
    return jnp.einsum('btni,nij->btnj', u.reshape(u.shape[0], u.shape[1], nb, bs), w).reshape(u.shape)


def linear_scan(a, b):
    def comb(l, r):
        return (l[0] * r[0], r[0] * l[1] + r[1])
    return lax.associative_scan(comb, (a, b), axis=1)[1]


def to_chunks(t):
    bn, n, h, d = t.shape
    return t.reshape(bn, n // CHUNK, CHUNK, h, d).transpose(1, 0, 3, 2, 4)


def from_chunks(t):
    nch, bn, h, l, d = t.shape
    return t.transpose(1, 0, 3, 2, 4).reshape(bn, nch * l, h, d)


def gla_chunked(q, k, v, g):
    bn, _, h, dk = q.shape
    dv = v.shape[-1]
    mask = jnp.tril(jnp.ones((CHUNK, CHUNK), bool))

    def step(s, inp):
        qi, ki, vi, gi = inp
        b = jnp.cumsum(gi, axis=2)
        b_mid = b[:, :, CHUNK // 2 - 1:CHUNK // 2]
        b_last = b[:, :, -1:]
        att = jnp.einsum('bhid,bhjd->bhij', qi * jnp.exp(b - b_mid), ki * jnp.exp(b_mid - b))
        att = jnp.where(mask, att, 0.0)
        o = jnp.einsum('bhij,bhjv->bhiv', att, vi) + jnp.einsum('bhid,bhdv->bhiv', qi * jnp.exp(b), s)
        s = jnp.exp(b_last[:, :, 0])[..., None] * s + jnp.einsum('bhjd,bhjv->bhdv', ki * jnp.exp(b_last - b), vi)
        return s, o

    s0 = jnp.zeros((bn, h, dk, dv), jnp.float32)
    _, o = lax.scan(step, s0, (to_chunks(q), to_chunks(k), to_chunks(v), to_chunks(g)))
    return from_chunks(o)


def mlstm_chunked(q, k, v, ig, lf):
    bn, _, h, d = q.shape
    mask = jnp.tril(jnp.ones((CHUNK, CHUNK), bool))

    def step(carry, inp):
        cm, nv, m = carry
        qi, ki, vi, ii, fi = inp
        b = jnp.cumsum(fi, axis=-1)
        dmat = jnp.where(mask, b[..., :, None] - b[..., None, :] + ii[..., None, :], -jnp.inf)
        inter = b + m[..., None]
        m_row = jnp.maximum(dmat.max(-1), inter)
        p = jnp.einsum('bhid,bhjd->bhij', qi, ki) * jnp.exp(dmat - m_row[..., None])
        s_inter = jnp.exp(inter - m_row)
        num = jnp.einsum('bhij,bhjv->bhiv', p, vi) + s_inter[..., None] * jnp.einsum('bhvd,bhid->bhiv', cm, qi)
        den = p.sum(-1) + s_inter * jnp.einsum('bhd,bhid->bhi', nv, qi)
        hout = num / jnp.maximum(jnp.abs(den), jnp.exp(-m_row))[..., None]
        b_last = b[..., -1]
        wl = b_last[..., None] - b + ii
        m_new = jnp.maximum(b_last + m, wl.max(-1))
        sw = jnp.exp(wl - m_new[..., None])
        decay = jnp.exp(b_last + m - m_new)
        cm = decay[..., None, None] * cm + jnp.einsum('bhj,bhjv,bhjd->bhvd', sw, vi, ki)
        nv = decay[..., None] * nv + jnp.einsum('bhj,bhjd->bhd', sw, ki)
        return (cm, nv, m_new), hout

    carry0 = (jnp.zeros((bn, h, d, d), jnp.float32), jnp.zeros((bn, h, d), jnp.float32), jnp.zeros((bn, h), jnp.float32))
    gates = lambda t: to_chunks(t[..., None])[..., 0]
    _, o = lax.scan(step, carry0, (to_chunks(q), to_chunks(k), to_chunks(v), gates(ig), gates(lf)))
    return from_chunks(o)


def rglru_branch(a_x, a_g, nc, r0, conv_w, conv_b, wa, ba, wx, bx, lam):
    u = jnp.concatenate([centred_dwconv(a_x[:, :nc], conv_w, conv_b),
                         centred_dwconv(to_colmajor(a_x[:, nc:]), conv_w, conv_b)], axis=1).astype(jnp.float32)
    h = 0.0
    for d in range(2):
        r = jax.nn.sigmoid(block_diag(u, wa[d]) + ba[d])
        i = jax.nn.sigmoid(block_diag(u, wx[d]) + bx[d])
        log_a = -LRU_C * jax.nn.softplus(-lam[d]) * r
        a = jnp.exp(log_a)
        bt = jnp.sqrt(-jnp.expm1(2.0 * log_a)) * (i * u)
        if d == 1:
            a, bt = flip_parts(a, nc), flip_parts(bt, nc)
        hd = linear_scan(a, bt)
        h = h + (flip_parts(hd, nc) if d == 1 else hd)
    h = jnp.concatenate([h[:, :nc], from_colmajor(h[:, nc:])], axis=1)[:, r0:]
    return h * jax.nn.gelu(a_g[:, r0:], approximate=True)


def gla_branch(q, k, v, r, lr, nc, r0, wa2, ba, norm_g):
    bn, t = q.shape[:2]
    heads = lambda z, dh: z.reshape(bn, t, -1, dh).astype(jnp.float32)
    qh, kh, vh = heads(q, GLA_DK) * GLA_DK ** -0.5, heads(k, GLA_DK), heads(v, GLA_DV)
    lr = lr.reshape(bn, t, 2, GLA_RANK)
    o = 0.0
    for d in range(2):
        g = heads(jax.nn.log_sigmoid((lr[:, :, d] @ wa2[d] + ba[d]).astype(jnp.float32)) / GLA_TAU, GLA_DK)
        ins = (qh, kh, vh, g)
        if d == 1:
            ins = tuple(flip_parts(z, nc) for z in ins)
        od = gla_chunked(*ins)
        o = o + (flip_parts(od, nc) if d == 1 else od)
    return head_norm(o[:, r0:], norm_g) * jax.nn.silu(r[:, r0:])


def mlstm_branch(q, k, v, o, g, nc, r0, conv_w, conv_b, norm_g):
    bn, t = q.shape[:2]
    qk = jnp.concatenate([q, k], axis=-1)
    qk = jax.nn.silu(jnp.concatenate([centred_dwconv(qk[:, :nc], conv_w, conv_b),
                                      centred_dwconv(qk[:, nc:], conv_w, conv_b)], axis=1))
    heads = lambda z: z.reshape(bn, t, MLSTM_HEADS, MLSTM_D).astype(jnp.float32)
    qh, kh, vh = heads(qk[..., :MLSTM_W]), heads(qk[..., MLSTM_W:]) * MLSTM_D ** -0.5, heads(v)
    g = g.reshape(bn, t, 2, 2, MLSTM_HEADS).astype(jnp.float32)
    h = 0.0
    for d in range(2):
        ins = (qh, kh, vh, g[:, :, d, 0], jax.nn.log_sigmoid(g[:, :, d, 1]))
        if d == 1:
            ins = tuple(flip_parts(z, nc) for z in ins)
        hd = mlstm_chunked(*ins)
        h = h + (flip_parts(hd, nc) if d == 1 else hd)
    return head_norm(h[:, r0:], norm_g) * jax.nn.sigmoid(o[:, r0:])


def mixer(h_lat, h_ctx, w_in, b_in, conv_a_w, conv_a_b, lru_wa, lru_ba, lru_wx, lru_bx, lru_lam,
          gla_wa2, gla_ba, gla_norm_g, conv_c_w, conv_c_b, mlstm_norm_g, w_branch, w_out, with_ctx):
    nc = h_ctx.shape[1]
    h_all = jnp.concatenate([h_ctx, h_lat], axis=1)
    r0 = 0 if with_ctx else nc
    feat = h_all @ w_in[:, :N_FEAT] + b_in[:N_FEAT]
    a_x, a_g, b_q, b_k, b_v, b_r, b_lr, c_q, c_k, c_v, c_o, c_g = jnp.split(feat, IN_SPLIT, axis=-1)
    ys = (rglru_branch(a_x, a_g, nc, r0, conv_a_w, conv_a_b, lru_wa, lru_ba, lru_wx, lru_bx, lru_lam),
          gla_branch(b_q, b_k, b_v, b_r, b_lr, nc, r0, gla_wa2, gla_ba, gla_norm_g),
          mlstm_branch(c_q, c_k, c_v, c_o, c_g, nc, r0, conv_c_w, conv_c_b, mlstm_norm_g))
    h_sel = h_all[:, r0:]
    merged = 0.0
    for n, y in enumerate(ys):
        lo = N_FEAT + n * D_MODEL
        gate = jax.nn.sigmoid(h_sel @ w_in[:, lo:lo + D_MODEL] + b_in[lo:lo + D_MODEL])
        merged = merged + gate * (y.astype(h_all.dtype) @ w_branch[n])
    return merged @ w_out


def expert_choice_ffn(h, w_router, w_gate, w_up, w_down):
    bn, n, _ = h.shape
    cap = EC_CAPACITY * n // N_EXPERTS
    aff = jax.nn.softmax((h @ w_router).astype(jnp.float32), axis=-1)
    gv, idx = lax.top_k(jnp.swapaxes(aff, 1, 2), cap)
    bidx = jnp.arange(bn)[:, None, None]
    xg = h[bidx, idx]
    hid = jax.nn.silu(jnp.einsum('becd,edf->becf', xg, w_gate)) * jnp.einsum('becd,edf->becf', xg, w_up)
    ye = jnp.einsum('becf,efd->becd', hid, w_down) * gv[..., None].astype(h.dtype)
    return jnp.zeros_like(h).at[bidx, idx].add(ye)


def setup_inputs(seed: int = 0) -> dict:
    key = jax.random.key(seed)
    ks = jax.random.split(key, 32)
    nrm = lambda i, shape, scale: jax.random.normal(ks[i], shape, jnp.float32) * scale
    D, L, E = D_MODEL, DEPTH, N_EXPERTS
    beta = DEEPNORM_BETA
    bs = D_RNN // RNN_BLOCKS
    a8 = jax.random.uniform(ks[10], (L, 2, D_RNN), jnp.float32, 0.9, 0.999)
    a = a8 ** (1.0 / LRU_C)
    gate_b = jnp.tile(jnp.concatenate([jnp.zeros((MLSTM_HEADS,), jnp.float32), jnp.linspace(3.0, 6.0, MLSTM_HEADS)]), 2)
    b_in = nrm(6, (L, N_IN), 0.02).at[:, N_FEAT - 4 * MLSTM_HEADS:N_FEAT].add(gate_b)
    return {
        'x': nrm(0, (BATCH, SEQ, D), 1.0),
        'c': nrm(1, (BATCH, D), 1.0),
        'ctx': nrm(2, (BATCH, CTX_LEN, D), 1.0),
        'c_ctx': nrm(3, (D,), 1.0),
        'w_mod': nrm(4, (L, D, 6 * D), 0.5 * D ** -0.5),
        'b_mod': nrm(5, (L, 6 * D), 0.02),
        'w_in': nrm(7, (L, D, N_IN), D ** -0.5),
        'b_in': b_in,
        'conv_a_w': nrm(8, (L, CONV_W, D_RNN), CONV_W ** -0.5),
        'conv_a_b': nrm(9, (L, D_RNN), 0.02),
        'lru_wa': nrm(11, (L, 2, RNN_BLOCKS, bs, bs), bs ** -0.5),
        'lru_ba': nrm(12, (L, 2, D_RNN), 0.02),
        'lru_wx': nrm(13, (L, 2, RNN_BLOCKS, bs, bs), bs ** -0.5),
        'lru_bx': nrm(14, (L, 2, D_RNN), 0.02),
        'lru_lam': jnp.log(a) - jnp.log1p(-a),
        'gla_wa2': nrm(15, (L, 2, GLA_RANK, GLA_K), GLA_RANK ** -0.5),
        'gla_ba': nrm(16, (L, 2, GLA_K), 0.02),
        'gla_norm_g': 1.0 + nrm(17, (L, GLA_V), 0.02),
        'conv_c_w': nrm(18, (L, CONV_W, 2 * MLSTM_W), CONV_W ** -0.5),
        'conv_c_b': nrm(19, (L, 2 * MLSTM_W), 0.02),
        'mlstm_norm_g': 1.0 + nrm(20, (L, MLSTM_W), 0.02),
        'w_branch': nrm(21, (L, N_BRANCH, D_RNN, D), beta * D_RNN ** -0.5),
        'w_out': nrm(22, (L, D, D), beta * D ** -0.5),
        'ln1_g': 1.0 + nrm(23, (L, D), 0.02),
        'ln1_b': nrm(24, (L, D), 0.02),
        'w_router': nrm(25, (L, D, E), D ** -0.5),
        'w_e_gate': nrm(26, (L, E, D, MOE_FF), beta * D ** -0.5),
        'w_e_up': nrm(27, (L, E, D, MOE_FF), beta * D ** -0.5),
        'w_e_down': nrm(28, (L, E, MOE_FF, D), beta * MOE_FF ** -0.5),
        'ln2_g': 1.0 + nrm(29, (L, D), 0.02),
        'ln2_b': nrm(30, (L, D), 0.02),
    }


def reference(x, c, ctx, c_ctx, w_mod, b_mod, w_in, b_in, conv_a_w, conv_a_b, lru_wa, lru_ba, lru_wx, lru_bx,
              lru_lam, gla_wa2, gla_ba, gla_norm_g, conv_c_w, conv_c_b, mlstm_norm_g, w_branch, w_out, ln1_g, ln1_b,
              w_router, w_e_gate, w_e_up, w_e_down, ln2_g, ln2_b):
    nc = ctx.shape[1]
    for l in range(DEPTH):
        last = l == DEPTH - 1
        mod = jax.nn.silu(c) @ w_mod[l] + b_mod[l]
        mod_c = jax.nn.silu(c_ctx) @ w_mod[l] + b_mod[l]
        sh1, sc1, g1, sh2, sc2, g2 = jnp.split(mod[:, None, :], 6, axis=-1)
        sh1c, sc1c, g1c, sh2c, sc2c, g2c = jnp.split(mod_c, 6)
        y = mixer(x * (1 + sc1) + sh1, ctx * (1 + sc1c) + sh1c, w_in[l], b_in[l], conv_a_w[l], conv_a_b[l],
                  lru_wa[l], lru_ba[l], lru_wx[l], lru_bx[l], lru_lam[l], gla_wa2[l], gla_ba[l], gla_norm_g[l],
                  conv_c_w[l], conv_c_b[l], mlstm_norm_g[l], w_branch[l], w_out[l], not last)
        if last:
            y_lat = y
        else:
            y_lat = y[:, nc:]
            ctx = layer_norm(DEEPNORM_ALPHA * ctx + g1c * y[:, :nc], ln1_g[l], ln1_b[l])
        x = layer_norm(DEEPNORM_ALPHA * x + g1 * y_lat, ln1_g[l], ln1_b[l])
        f = expert_choice_ffn(x * (1 + sc2) + sh2, w_router[l], w_e_gate[l], w_e_up[l], w_e_down[l])
        x = layer_norm(DEEPNORM_ALPHA * x + g2 * f, ln2_g[l], ln2_b[l])
        if not last:
            fc = expert_choice_ffn(ctx * (1 + sc2c) + sh2c, w_router[l], w_e_gate[l], w_e_up[l], w_e_down[l])
            ctx = layer_norm(DEEPNORM_ALPHA * ctx + g2c * fc, ln2_g[l], ln2_b[l])
    return x
```

```python
import functools

import jax
import jax.numpy as jnp
from jax import lax
from jax.experimental import pallas as pl
from jax.experimental.pallas import tpu as pltpu

F32 = jnp.float32
BF16 = jnp.bfloat16
HI = lax.Precision.HIGHEST

GRID_W = 64
CHUNK = 64
LRU_C = 8.0
GLA_HEADS = 4
GLA_TAU = 16.0
MLSTM_HEADS = 4
EC_CAPACITY = 2
LN_EPS = 1e-5

LANE = 128
SUBLANE = 8
ROW_TILE = 256
VMEM_LIMIT = 48 << 20

NT = (((1,), (1,)), ((), ()))
TN = (((0,), (0,)), ((), ()))


def _cparams(*sem):
    return pltpu.CompilerParams(dimension_semantics=sem, vmem_limit_bytes=VMEM_LIMIT)


def _sds(shape, dtype):
    return jax.ShapeDtypeStruct(shape, dtype)


def _dot(a, b, dims=None, precision=None):
    if dims is None:
        return jnp.dot(a, b, preferred_element_type=F32, precision=precision)
    return lax.dot_general(a, b, dims, preferred_element_type=F32, precision=precision)


def _mod_kernel(c_ref, w_ref, b_ref, o_ref):
    a = jax.nn.silu(c_ref[...]).astype(BF16)
    o_ref[0] = _dot(a, w_ref[0].astype(BF16)) + b_ref[0]


def _mod_all(cvec, w_mod, b_mod):
    depth, d, n6 = w_mod.shape
    tn = 512
    return pl.pallas_call(
        _mod_kernel,
        grid=(depth, n6 // tn),
        in_specs=[pl.BlockSpec((SUBLANE, d), lambda l, j: (0, 0)),
                  pl.BlockSpec((1, d, tn), lambda l, j: (l, 0, j)),
                  pl.BlockSpec((1, 1, tn), lambda l, j: (l, 0, j))],
        out_specs=pl.BlockSpec((1, SUBLANE, tn), lambda l, j: (l, 0, j)),
        out_shape=_sds((depth, SUBLANE, n6), F32),
        compiler_params=_cparams("parallel", "parallel"),
        name="mod",
    )(cvec, w_mod, b_mod.reshape(depth, 1, n6))


def _modulate_kernel(x_ref, m_ref, o_ref, *, off):
    sh = m_ref[0, off:off + 1, :]
    sc = m_ref[0, off + 1:off + 2, :]
    o_ref[0] = (x_ref[0] * (1.0 + sc) + sh).astype(o_ref.dtype)


def _mod_row_map(n_lat_tiles, batch):
    return lambda b, i: (jnp.where(i < n_lat_tiles, b, batch), 0, 0)


def _modulate(xs, mod6, off, n_lat):
    batch, t, d = xs.shape
    tr = ROW_TILE
    return pl.pallas_call(
        functools.partial(_modulate_kernel, off=off),
        grid=(batch, t // tr),
        in_specs=[pl.BlockSpec((1, tr, d), lambda b, i: (b, i, 0)),
                  pl.BlockSpec((1, 6, d), _mod_row_map(n_lat // tr, batch))],
        out_specs=pl.BlockSpec((1, tr, d), lambda b, i: (b, i, 0)),
        out_shape=_sds((batch, t, d), BF16),
        compiler_params=_cparams("parallel", "parallel"),
        name="modulate",
    )(xs, mod6)


def _mm_kernel(a_ref, w_ref, b_ref, o_ref):
    o_ref[...] = (_dot(a_ref[...], w_ref[...]) + b_ref[...]).astype(o_ref.dtype)


def _pick_tile(n, pref):
    while pref > LANE and n % pref:
        pref //= 2
    assert n % pref == 0
    return pref


def _matmul(a, w, bias, out_dtype, tm, tn, name):
    m, k = a.shape
    n = w.shape[1]
    tm, tn = _pick_tile(m, tm), _pick_tile(n, tn)
    return pl.pallas_call(
        _mm_kernel,
        grid=(n // tn, m // tm),
        in_specs=[pl.BlockSpec((tm, k), lambda j, i: (i, 0)),
                  pl.BlockSpec((k, tn), lambda j, i: (0, j)),
                  pl.BlockSpec((1, tn), lambda j, i: (0, j))],
        out_specs=pl.BlockSpec((tm, tn), lambda j, i: (i, j)),
        out_shape=_sds((m, n), out_dtype),
        compiler_params=_cparams("parallel", "parallel"),
        name=name,
    )(a, w, bias)


def _res_ln_kernel(x_ref, y_ref, m_ref, g_ref, b_ref, o_ref, *, off, alpha):
    gate = m_ref[0, off:off + 1, :]
    z = alpha * x_ref[0] + gate * y_ref[0]
    mu = jnp.mean(z, axis=-1, keepdims=True)
    zc = z - mu
    var = jnp.mean(zc * zc, axis=-1, keepdims=True)
    o_ref[0] = zc * lax.rsqrt(var + LN_EPS) * g_ref[...] + b_ref[...]


def _res_ln(xs, y, mod6, off, ln_g, ln_b, alpha, n_lat, t_out):
    batch, t, d = xs.shape
    tr = ROW_TILE
    return pl.pallas_call(
        functools.partial(_res_ln_kernel, off=off, alpha=alpha),
        grid=(batch, t_out // tr),
        in_specs=[pl.BlockSpec((1, tr, d), lambda b, i: (b, i, 0)),
                  pl.BlockSpec((1, tr, d), lambda b, i: (b, i, 0)),
                  pl.BlockSpec((1, 6, d), _mod_row_map(n_lat // tr, batch)),
                  pl.BlockSpec((1, d), lambda b, i: (0, 0)),
                  pl.BlockSpec((1, d), lambda b, i: (0, 0))],
        out_specs=pl.BlockSpec((1, tr, d), lambda b, i: (b, i, 0)),
        out_shape=_sds((batch, t_out, d), F32),
        compiler_params=_cparams("parallel", "parallel"),
        name="res_ln",
    )(xs, y, mod6, ln_g.reshape(1, d), ln_b.reshape(1, d))


def _gelu_tanh(x):
    return jax.nn.gelu(x, approximate=True)


def _lru_gates(u2, wa_ref, ba_ref, wx_ref, bx_ref, lam_ref):
    ub = u2.astype(BF16)
    r = jax.nn.sigmoid(_dot(ub, wa_ref[0]) + ba_ref[...])
    i = jax.nn.sigmoid(_dot(ub, wx_ref[0]) + bx_ref[...])
    log_a = (-LRU_C * jax.nn.softplus(-lam_ref[...])) * r
    a = jnp.exp(log_a)
    bt = jnp.sqrt(1.0 - jnp.exp(2.0 * log_a)) * (i * u2)
    return a, bt


def _lru_lat_kernel(*refs, rev, n_cg):
    if rev:
        (x_ref, pv_ref, nx_ref, cw_ref, cb_ref, wa_ref, ba_ref, wx_ref, bx_ref, lam_ref, e0_ref,
         ag_ref, hf_ref, _alias, o_ref, a_s, b_s, carry) = refs
    else:
        (x_ref, pv_ref, nx_ref, cw_ref, cb_ref, wa_ref, ba_ref, wx_ref, bx_ref, lam_ref, e0_ref,
         _alias, o_ref, a_s, b_s, carry) = refs
    s = pl.program_id(1)
    cg = (n_cg - 1 - s) if rev else s
    bsz, rows, ncol, cb = x_ref.shape

    @pl.when(s == 0)
    def _():
        carry[...] = e0_ref[...]

    x = x_ref[...]
    col = lax.broadcasted_iota(jnp.int32, (1, ncol, 1), 1)
    not_first = (cg > 0).astype(F32)
    not_last = (cg < n_cg - 1).astype(F32)
    top = jnp.where(col == 0, pltpu.roll(pv_ref[:, SUBLANE - 1], 1, 1) * not_first, pltpu.roll(x[:, rows - 1], 1, 1))
    bot1 = jnp.where(col == ncol - 1, pltpu.roll(nx_ref[:, 0], ncol - 1, 1) * not_last, pltpu.roll(x[:, 0], ncol - 1, 1))
    bot2 = jnp.where(col == ncol - 1, pltpu.roll(nx_ref[:, 1], ncol - 1, 1) * not_last, pltpu.roll(x[:, 1], ncol - 1, 1))
    xe = jnp.concatenate([top[:, None], x, bot1[:, None], bot2[:, None]], axis=1)
    u = cb_ref[...].reshape(1, 1, 1, cb)
    for k in range(4):
        u = u + cw_ref[k:k + 1, :].reshape(1, 1, 1, cb) * xe[:, k:k + rows]
    a, bt = _lru_gates(u.reshape(bsz * rows * ncol, cb), wa_ref, ba_ref, wx_ref, bx_ref, lam_ref)
    a_s[...] = a.reshape(bsz, rows, ncol, cb)
    b_s[...] = bt.reshape(bsz, rows, ncol, cb)

    def body(t, hp):
        h, p = hp
        r = (rows - 1 - t) if rev else t
        a_t = a_s[:, r]
        h = a_t * h + b_s[:, r]
        p = p * a_t
        b_s[:, r] = h
        a_s[:, r] = p
        return h, p

    h_end, p_end = lax.fori_loop(0, rows, body, (jnp.zeros((bsz, ncol, cb), F32), jnp.ones((bsz, ncol, cb), F32)), unroll=8)
    av, bv = p_end, h_end
    sh = 1
    while sh < ncol:
        if rev:
            valid = col < ncol - sh
            amt = ncol - sh
        else:
            valid = col >= sh
            amt = sh
        b_sh = jnp.where(valid, pltpu.roll(bv, amt, 1), 0.0)
        a_sh = jnp.where(valid, pltpu.roll(av, amt, 1), 1.0)
        bv = bv + av * b_sh
        av = av * a_sh
        sh *= 2
    e_prev = carry[...]
    e = bv + av * e_prev
    if rev:
        c_in = jnp.where(col == ncol - 1, e_prev, pltpu.roll(e, ncol - 1, 1))
        carry[...] = jnp.broadcast_to(e[:, 0:1], e.shape)
    else:
        c_in = jnp.where(col == 0, e_prev, pltpu.roll(e, 1, 1))
        carry[...] = jnp.broadcast_to(e[:, ncol - 1:ncol], e.shape)
    h = b_s[...] + a_s[...] * c_in[:, None]
    if rev:
        o_ref[...] = (hf_ref[...] + h) * _gelu_tanh(ag_ref[...])
    else:
        o_ref[...] = h


def _lru_ctx_kernel(*refs, rev):
    if rev:
        (x_ref, cw_ref, cb_ref, wa_ref, ba_ref, wx_ref, bx_ref, lam_ref, ag_ref, hf_ref, o_ref, e_ref) = refs
    else:
        (x_ref, cw_ref, cb_ref, wa_ref, ba_ref, wx_ref, bx_ref, lam_ref, o_ref, e_ref) = refs
    bsz, nc, cb = x_ref.shape
    x = x_ref[...]
    t = lax.broadcasted_iota(jnp.int32, (1, nc, 1), 1)
    xm1 = jnp.where(t >= 1, pltpu.roll(x, 1, 1), 0.0)
    xp1 = jnp.where(t < nc - 1, pltpu.roll(x, nc - 1, 1), 0.0)
    xp2 = jnp.where(t < nc - 2, pltpu.roll(x, nc - 2, 1), 0.0)
    w = [cw_ref[k:k + 1, :].reshape(1, 1, cb) for k in range(4)]
    u = w[0] * xm1 + w[1] * x + w[2] * xp1 + w[3] * xp2 + cb_ref[...].reshape(1, 1, cb)
    a, bt = _lru_gates(u.reshape(bsz * nc, cb), wa_ref, ba_ref, wx_ref, bx_ref, lam_ref)
    av = a.reshape(bsz, nc, cb)
    bv = bt.reshape(bsz, nc, cb)
    sh = 1
    while sh < nc:
        if rev:
            valid = t < nc - sh
            amt = nc - sh
        else:
            valid = t >= sh
            amt = sh
        b_sh = jnp.where(valid, pltpu.roll(bv, amt, 1), 0.0)
        a_sh = jnp.where(valid, pltpu.roll(av, amt, 1), 1.0)
        bv = bv + av * b_sh
        av = av * a_sh
        sh *= 2
    if rev:
        o_ref[...] = (hf_ref[...] + bv) * _gelu_tanh(ag_ref[...])
        e_ref[...] = jnp.broadcast_to(bv[:, 0:1], (bsz, SUBLANE, cb))
    else:
        o_ref[...] = bv
        e_ref[...] = jnp.broadcast_to(bv[:, nc - 1:nc], (bsz, SUBLANE, cb))


def _rglru(feat, p, d_rnn, n_lat):
    bsz, t, nf = feat.shape
    nc = t - n_lat
    nb, bs, _ = p["wa"][0].shape
    cb = bs
    assert d_rnn == nb * bs and cb % LANE == 0 and n_lat % nc == 0 and nc % GRID_W == 0
    rows = n_lat // GRID_W
    assert rows % SUBLANE == 0 and GRID_W % SUBLANE == 0
    n_cg = GRID_W // SUBLANE
    ag_off = d_rnn // cb
    feat4 = feat.reshape(bsz, t // GRID_W, GRID_W, nf)
    ctx_blk = n_lat // nc

    def wspecs(im):
        return [pl.BlockSpec((4, cb), im(lambda j: (0, j))),
                pl.BlockSpec((1, cb), im(lambda j: (0, j))),
                pl.BlockSpec((1, bs, bs), im(lambda j: (j, 0, 0))),
                pl.BlockSpec((1, cb), im(lambda j: (0, j))),
                pl.BlockSpec((1, bs, bs), im(lambda j: (j, 0, 0))),
                pl.BlockSpec((1, cb), im(lambda j: (0, j))),
                pl.BlockSpec((1, cb), im(lambda j: (0, j)))]

    def wargs(d):
        return [p["conv_w"], p["conv_b"], p["wa"][d], p["ba"][d], p["wx"][d], p["bx"][d], p["lam"][d]]

    im1 = lambda f: (lambda j: f(j))
    im2 = lambda f: (lambda j, s: f(j))
    hf = None
    out = None
    for rev in (False, True):
        in_specs = [pl.BlockSpec((bsz, nc, cb), lambda j: (0, ctx_blk, j))] + wspecs(im1)
        args = [feat] + wargs(int(rev))
        if rev:
            in_specs += [pl.BlockSpec((bsz, nc, cb), lambda j: (0, ctx_blk, ag_off + j)),
                         pl.BlockSpec((bsz, nc, cb), lambda j: (0, ctx_blk, j))]
            args += [feat, hf]
        part, e0 = pl.pallas_call(
            functools.partial(_lru_ctx_kernel, rev=rev),
            grid=(nb,),
            in_specs=in_specs,
            out_specs=[pl.BlockSpec((bsz, nc, cb), lambda j: (0, ctx_blk, j)),
                       pl.BlockSpec((bsz, SUBLANE, cb), lambda j: (0, 0, j))],
            out_shape=[_sds((bsz, t, d_rnn), F32), _sds((bsz, SUBLANE, d_rnn), F32)],
            compiler_params=_cparams("parallel"),
            name="lru_ctx_bwd" if rev else "lru_ctx_fwd",
        )(*args)
        cgm = (lambda s: n_cg - 1 - s) if rev else (lambda s: s)
        in_specs = [pl.BlockSpec((bsz, rows, SUBLANE, cb), lambda j, s: (0, 0, cgm(s), j)),
                    pl.BlockSpec((bsz, SUBLANE, SUBLANE, cb), lambda j, s: (0, rows // SUBLANE - 1, jnp.maximum(cgm(s) - 1, 0), j)),
                    pl.BlockSpec((bsz, SUBLANE, SUBLANE, cb), lambda j, s: (0, 0, jnp.minimum(cgm(s) + 1, n_cg - 1), j))]
        in_specs += wspecs(im2) + [pl.BlockSpec((bsz, SUBLANE, cb), lambda j, s: (0, 0, j))]
        args = [feat4, feat4, feat4] + wargs(int(rev)) + [e0]
        if rev:
            in_specs += [pl.BlockSpec((bsz, rows, SUBLANE, cb), lambda j, s: (0, 0, cgm(s), ag_off + j)),
                         pl.BlockSpec((bsz, rows, SUBLANE, cb), lambda j, s: (0, 0, cgm(s), j))]
            args += [feat4, hf.reshape(bsz, t // GRID_W, GRID_W, d_rnn)]
        in_specs += [pl.BlockSpec(memory_space=pl.ANY)]
        args += [part.reshape(bsz, t // GRID_W, GRID_W, d_rnn)]
        res = pl.pallas_call(
            functools.partial(_lru_lat_kernel, rev=rev, n_cg=n_cg),
            grid=(nb, n_cg),
            in_specs=in_specs,
            out_specs=pl.BlockSpec((bsz, rows, SUBLANE, cb), lambda j, s: (0, 0, cgm(s), j)),
            out_shape=_sds((bsz, t // GRID_W, GRID_W, d_rnn), F32),
            scratch_shapes=[pltpu.VMEM((bsz, rows, SUBLANE, cb), F32),
                            pltpu.VMEM((bsz, rows, SUBLANE, cb), F32),
                            pltpu.VMEM((bsz, SUBLANE, cb), F32)],
            input_output_aliases={len(args) - 1: 0},
            compiler_params=_cparams("parallel", "arbitrary"),
            name="lru_lat_bwd" if rev else "lru_lat_fwd",
        )(*args)
        res = res.reshape(bsz, t, d_rnn)
        if rev:
            out = res
        else:
            hf = res
    return out


def _chunk_of(s, n_ch, n_lat_ch, rev):
    if rev:
        return n_ch - 1 - s
    return jnp.where(s < n_ch - n_lat_ch, s + n_lat_ch, s - (n_ch - n_lat_ch))


def _head_norm(o, g):
    mu = jnp.mean(o, axis=-1, keepdims=True)
    oc = o - mu
    var = jnp.mean(oc * oc, axis=-1, keepdims=True)
    return oc * lax.rsqrt(var + LN_EPS) * g


def _gla_kernel(*refs, rev, heads):
    if rev:
        (q_ref, k_ref, v_ref, lr_ref, wlr_ref, ba_ref, tri_ref, rsel_ref, r_ref, of_ref, ng_ref, o_ref, st) = refs
    else:
        (q_ref, k_ref, v_ref, lr_ref, wlr_ref, ba_ref, tri_ref, rsel_ref, o_ref, st) = refs
    s = pl.program_id(1)

    @pl.when(s == 0)
    def _():
        st[...] = jnp.zeros(st.shape, F32)

    dk = q_ref.shape[-1] // heads
    dv = v_ref.shape[-1] // heads
    tri = tri_ref[...]
    g = jax.nn.log_sigmoid(_dot(lr_ref[0], wlr_ref[...], precision=HI) + ba_ref[...]) / GLA_TAU
    bcum = _dot(tri, g, precision=HI)
    bm = _dot(rsel_ref[...], g, precision=HI)
    q = q_ref[0] * dk ** -0.5
    k = k_ref[0]
    v = v_ref[0]
    for h in range(heads):
        sk = slice(h * dk, (h + 1) * dk)
        sv = slice(h * dv, (h + 1) * dv)
        qh, kh, bh = q[:, sk], k[:, sk], bcum[:, sk]
        vh = v[:, sv].astype(BF16)
        bmid, blast = bm[0:1, sk], bm[1:2, sk]
        att = _dot((qh * jnp.exp(bh - bmid)).astype(BF16), (kh * jnp.exp(bmid - bh)).astype(BF16), NT) * tri
        sth = st[h]
        o = _dot(att.astype(BF16), vh) + _dot((qh * jnp.exp(bh)).astype(BF16), sth.astype(BF16), NT)
        st[h] = jnp.exp(blast) * sth + _dot(vh, (kh * jnp.exp(blast - bh)).astype(BF16), TN)
        if rev:
            o = _head_norm(of_ref[0, :, sv] + o, ng_ref[:, sv]) * jax.nn.silu(r_ref[0, :, sv])
        o_ref[0, :, sv] = o


def _scan_consts(rev):
    i = jnp.arange(CHUNK)
    tri = (i[None, :] >= i[:, None]) if rev else (i[None, :] <= i[:, None])
    tri = tri.astype(F32)
    mid = CHUNK // 2 if rev else CHUNK // 2 - 1
    last = 0 if rev else CHUNK - 1
    rsel = jnp.zeros((SUBLANE, CHUNK), F32).at[0].set(tri[mid]).at[1].set(tri[last])
    return tri, rsel


def _gla(feat, feat_s, p, offs, n_lat):
    bsz, t, _ = feat.shape
    kg, vg = p["ba"][0].shape[-1], p["norm_g"].shape[-1]
    oq, ok, ov, orr = offs
    assert oq % kg == 0 and ok % kg == 0 and ov % vg == 0 and orr % vg == 0
    n_ch, n_lat_ch = t // CHUNK, n_lat // CHUNK
    of = None
    for rev in (False, True):
        d = int(rev)
        tri, rsel = _scan_consts(rev)
        cm = lambda b, s: (b, _chunk_of(s, n_ch, n_lat_ch, rev))
        const = lambda b, s: (0, 0)
        in_specs = [pl.BlockSpec((1, CHUNK, kg), lambda b, s: cm(b, s) + (oq // kg,)),
                    pl.BlockSpec((1, CHUNK, kg), lambda b, s: cm(b, s) + (ok // kg,)),
                    pl.BlockSpec((1, CHUNK, vg), lambda b, s: cm(b, s) + (ov // vg,)),
                    pl.BlockSpec((1, CHUNK, LANE), lambda b, s: cm(b, s) + (0,)),
                    pl.BlockSpec((LANE, kg), const),
                    pl.BlockSpec((1, kg), const),
                    pl.BlockSpec((CHUNK, CHUNK), const),
                    pl.BlockSpec((SUBLANE, CHUNK), const)]
        args = [feat, feat, feat, feat_s, p["wlr"][d], p["ba"][d], tri, rsel]
        if rev:
            in_specs += [pl.BlockSpec((1, CHUNK, vg), lambda b, s: cm(b, s) + (orr // vg,)),
                         pl.BlockSpec((1, CHUNK, vg), lambda b, s: cm(b, s) + (0,)),
                         pl.BlockSpec((1, vg), const)]
            args += [feat, of, p["norm_g"]]
        res = pl.pallas_call(
            functools.partial(_gla_kernel, rev=rev, heads=GLA_HEADS),
            grid=(bsz, n_ch),
            in_specs=in_specs,
            out_specs=pl.BlockSpec((1, CHUNK, vg), lambda b, s: cm(b, s) + (0,)),
            out_shape=_sds((bsz, t, vg), F32),
            scratch_shapes=[pltpu.VMEM((GLA_HEADS, vg // GLA_HEADS, kg // GLA_HEADS), F32)],
            compiler_params=_cparams("parallel", "arbitrary"),
            name="gla_bwd" if rev else "gla_fwd",
        )(*args)
        of = res
    return of


def _conv_rows(x, prev_row, next_rows, w_ref, b_ref, lo, hi):
    n = x.shape[0]
    t = lax.broadcasted_iota(jnp.int32, (n, 1), 0)
    xm1 = jnp.where(t == 0, prev_row, pltpu.roll(x, 1, 0))
    xp1 = jnp.where(t == n - 1, next_rows[0:1], pltpu.roll(x, n - 1, 0))
    xp2 = jnp.where(t == n - 2, next_rows[0:1], jnp.where(t == n - 1, next_rows[1:2], pltpu.roll(x, n - 2, 0)))
    w = w_ref[:, lo:hi]
    return w[0:1] * xm1 + w[1:2] * x + w[2:3] * xp1 + w[3:4] * xp2 + b_ref[:, lo:hi]


def _mlstm_kernel(*refs, rev, heads, n_ch, n_lat_ch, g_off):
    if rev:
        (q_ref, qp_ref, qn_ref, k_ref, kp_ref, kn_ref, v_ref, cg_ref, cw_ref, cbias_ref, tri_ref, trit_ref,
         og_ref, hf_ref, ng_ref, o_ref, c_s, n_s, m_s) = refs
    else:
        (q_ref, qp_ref, qn_ref, k_ref, kp_ref, kn_ref, v_ref, cg_ref, cw_ref, cbias_ref, tri_ref, trit_ref,
         o_ref, c_s, n_s, m_s) = refs
    s = pl.program_id(1)

    @pl.when(s == 0)
    def _():
        c_s[...] = jnp.zeros(c_s.shape, F32)
        n_s[...] = jnp.zeros(n_s.shape, F32)
        m_s[...] = jnp.zeros(m_s.shape, F32)

    ch = _chunk_of(s, n_ch, n_lat_ch, rev)
    not_first = jnp.logical_and(ch != 0, ch != n_lat_ch).astype(F32)
    not_last = jnp.logical_and(ch != n_lat_ch - 1, ch != n_ch - 1).astype(F32)
    wm = q_ref.shape[-1]
    dh = wm // heads
    qc = jax.nn.silu(_conv_rows(q_ref[0], qp_ref[0, SUBLANE - 1:SUBLANE] * not_first, qn_ref[0, 0:2] * not_last,
                                cw_ref, cbias_ref, 0, wm))
    kc = jax.nn.silu(_conv_rows(k_ref[0], kp_ref[0, SUBLANE - 1:SUBLANE] * not_first, kn_ref[0, 0:2] * not_last,
                                cw_ref, cbias_ref, wm, 2 * wm)) * dh ** -0.5
    v = v_ref[0]
    tri = tri_ref[...]
    gts = cg_ref[0]
    gls = jax.nn.log_sigmoid(gts)
    bcol_all = _dot(tri, gls, precision=HI)
    ci0 = g_off + int(rev) * 2 * heads
    cf0 = ci0 + heads
    lane = lax.broadcasted_iota(jnp.int32, (SUBLANE, LANE), 1)
    row = lax.broadcasted_iota(jnp.int32, (SUBLANE, LANE), 0)
    sel_i = (lane == row + ci0).astype(F32)
    sel_f = (lane == row + cf0).astype(F32)
    ig_rows = _dot(sel_i, gts, NT, precision=HI)
    b_rows = _dot(_dot(sel_f, gls, NT, precision=HI), trit_ref[...], precision=HI)
    last = 0 if rev else CHUNK - 1
    for h in range(heads):
        sl = slice(h * dh, (h + 1) * dh)
        qh = qc[:, sl]
        qb, kb, vh = qh.astype(BF16), kc[:, sl].astype(BF16), v[:, sl]
        bc = bcol_all[:, cf0 + h:cf0 + h + 1]
        igc = gts[:, ci0 + h:ci0 + h + 1]
        m = m_s[h:h + 1, 0:1]
        dmat = jnp.where(tri > 0.0, bc - b_rows[h:h + 1, :] + ig_rows[h:h + 1, :], -jnp.inf)
        inter = bc + m
        m_row = jnp.maximum(jnp.max(dmat, axis=-1, keepdims=True), inter)
        pmat = _dot(qb, kb, NT) * jnp.exp(dmat - m_row)
        s_inter = jnp.exp(inter - m_row)
        cm = c_s[h]
        nv = n_s[h:h + 1, :]
        num = _dot(pmat.astype(BF16), vh.astype(BF16)) + s_inter * _dot(qb, cm.astype(BF16), NT)
        den = jnp.sum(pmat, axis=-1, keepdims=True) + s_inter * jnp.sum(qh * nv, axis=-1, keepdims=True)
        hout = num / jnp.maximum(jnp.abs(den), jnp.exp(-m_row))
        b_last = bc[last:last + 1]
        wl = b_last - bc + igc
        m_new = jnp.maximum(b_last + m, jnp.max(wl, axis=0, keepdims=True))
        sw = jnp.exp(wl - m_new)
        decay = jnp.exp(b_last + m - m_new)
        c_s[h] = decay * cm + _dot((sw * vh).astype(BF16), kb, TN)
        n_s[h:h + 1, :] = decay * nv + jnp.sum(sw * kc[:, sl], axis=0, keepdims=True)
        m_s[h:h + 1, :] = jnp.broadcast_to(m_new, (1, LANE))
        if rev:
            hout = _head_norm(hf_ref[0, :, sl] + hout, ng_ref[:, sl]) * jax.nn.sigmoid(og_ref[0, :, sl])
        o_ref[0, :, sl] = hout


def _mlstm(feat, feat_s, p, offs, g_off, n_lat):
    bsz, t, _ = feat.shape
    wm = p["norm_g"].shape[-1]
    oq, ok, ov, oo = offs
    assert all(o % wm == 0 for o in offs)
    n_ch, n_lat_ch = t // CHUNK, n_lat // CHUNK
    n_r8 = t // SUBLANE
    per = CHUNK // SUBLANE
    hf = None
    for rev in (False, True):
        tri, _ = _scan_consts(rev)
        chm = lambda s: _chunk_of(s, n_ch, n_lat_ch, rev)
        const = lambda b, s: (0, 0)

        def cur(off):
            return pl.BlockSpec((1, CHUNK, wm), lambda b, s: (b, chm(s), off // wm))

        def prv(off):
            return pl.BlockSpec((1, SUBLANE, wm), lambda b, s: (b, jnp.maximum(chm(s) * per - 1, 0), off // wm))

        def nxt(off):
            return pl.BlockSpec((1, SUBLANE, wm), lambda b, s: (b, jnp.minimum((chm(s) + 1) * per, n_r8 - 1), off // wm))

        in_specs = [cur(oq), prv(oq), nxt(oq), cur(ok), prv(ok), nxt(ok), cur(ov),
                    pl.BlockSpec((1, CHUNK, LANE), lambda b, s: (b, chm(s), 0)),
                    pl.BlockSpec((4, 2 * wm), const),
                    pl.BlockSpec((1, 2 * wm), const),
                    pl.BlockSpec((CHUNK, CHUNK), const),
                    pl.BlockSpec((CHUNK, CHUNK), const)]
        args = [feat] * 7 + [feat_s, p["conv_w"], p["conv_b"], tri, tri.T]
        if rev:
            in_specs += [cur(oo), pl.BlockSpec((1, CHUNK, wm), lambda b, s: (b, chm(s), 0)), pl.BlockSpec((1, wm), const)]
            args += [feat, hf, p["norm_g"]]
        dh = wm // MLSTM_HEADS
        hf = pl.pallas_call(
            functools.partial(_mlstm_kernel, rev=rev, heads=MLSTM_HEADS, n_ch=n_ch, n_lat_ch=n_lat_ch, g_off=g_off),
            grid=(bsz, n_ch),
            in_specs=in_specs,
            out_specs=pl.BlockSpec((1, CHUNK, wm), lambda b, s: (b, chm(s), 0)),
            out_shape=_sds((bsz, t, wm), F32),
            scratch_shapes=[pltpu.VMEM((MLSTM_HEADS, dh, dh), F32),
                            pltpu.VMEM((SUBLANE, dh), F32),
                            pltpu.VMEM((SUBLANE, LANE), F32)],
            compiler_params=_cparams("parallel", "arbitrary"),
            name="mlstm_bwd" if rev else "mlstm_fwd",
        )(*args)
    return hf


def _merge_kernel(h_ref, wg_ref, bg_ref, y0_ref, y1_ref, y2_ref, wb_ref, o_ref, *, tn):
    g = jax.nn.sigmoid(_dot(h_ref[...], wg_ref[...]) + bg_ref[...])
    acc = None
    for n, y_ref in enumerate((y0_ref, y1_ref, y2_ref)):
        term = g[:, n * tn:(n + 1) * tn] * _dot(y_ref[...].astype(BF16), wb_ref[n])
        acc = term if acc is None else acc + term
    o_ref[...] = acc.astype(o_ref.dtype)


MERGE_TN = 512


def _merge(h, wg3, bg3, ys, wb):
    m, d = h.shape
    r = ys[0].shape[1]
    tm, tn = 256, MERGE_TN
    return pl.pallas_call(
        functools.partial(_merge_kernel, tn=tn),
        grid=(d // tn, m // tm),
        in_specs=[pl.BlockSpec((tm, d), lambda j, i: (i, 0)),
                  pl.BlockSpec((d, 3 * tn), lambda j, i: (0, j)),
                  pl.BlockSpec((1, 3 * tn), lambda j, i: (0, j)),
                  pl.BlockSpec((tm, r), lambda j, i: (i, 0)),
                  pl.BlockSpec((tm, r), lambda j, i: (i, 0)),
                  pl.BlockSpec((tm, r), lambda j, i: (i, 0)),
                  pl.BlockSpec((3, r, tn), lambda j, i: (0, 0, j))],
        out_specs=pl.BlockSpec((tm, tn), lambda j, i: (i, j)),
        out_shape=_sds((m, d), BF16),
        compiler_params=_cparams("parallel", "parallel"),
        name="merge",
    )(h, wg3, bg3, *ys, wb)


def _router_kernel(x_ref, w_ref, o_ref):
    logits = _dot(w_ref[...], x_ref[0], NT)
    ex = jnp.exp(logits - jnp.max(logits, axis=0, keepdims=True))
    o_ref[0] = ex / jnp.sum(ex, axis=0, keepdims=True)


def _router(xm, w_rt):
    bsz, t, d = xm.shape
    e = w_rt.shape[0]
    tr = ROW_TILE
    return pl.pallas_call(
        _router_kernel,
        grid=(bsz, t // tr),
        in_specs=[pl.BlockSpec((1, tr, d), lambda b, i: (b, i, 0)),
                  pl.BlockSpec((e, d), lambda b, i: (0, 0))],
        out_specs=pl.BlockSpec((1, e, tr), lambda b, i: (b, 0, i)),
        out_shape=_sds((bsz, e, t), F32),
        compiler_params=_cparams("parallel", "parallel"),
        name="router",
    )(xm, w_rt)


def _prefix_excl(src_ref, dst_ref, upper):
    e, n = src_ref.shape
    off = jnp.zeros((e, 1), F32)
    for kb in range(n // LANE):
        blk = src_ref[:, kb * LANE:(kb + 1) * LANE]
        inc = _dot(blk.astype(BF16), upper)
        dst_ref[:, kb * LANE:(kb + 1) * LANE] = inc - blk + off
        off = off + inc[:, LANE - 1:LANE]


def _topk_kernel(a_ref, slot_ref, slot_t_ref, gv_t_ref, m_s, r_s, *, cap):
    aff = a_ref[0]
    e, n = aff.shape
    bits = pltpu.bitcast(aff, jnp.int32)
    thr = jnp.zeros((e, 1), jnp.int32)
    for bit in range(30, -1, -1):
        cand = thr | (1 << bit)
        cnt = jnp.sum((bits >= cand).astype(jnp.int32), axis=1, keepdims=True)
        thr = jnp.where(cnt >= cap, cand, thr)
    gt = (bits > thr).astype(F32)
    eq = (bits == thr).astype(F32)
    need = cap - jnp.sum(gt, axis=1, keepdims=True)
    ii = lax.broadcasted_iota(jnp.int32, (LANE, LANE), 0)
    jj = lax.broadcasted_iota(jnp.int32, (LANE, LANE), 1)
    upper = (ii <= jj).astype(BF16)
    m_s[...] = eq
    _prefix_excl(m_s, r_s, upper)
    sel = gt + eq * (r_s[...] < need).astype(F32)
    m_s[...] = sel
    _prefix_excl(m_s, r_s, upper)
    slot = jnp.where(sel > 0.0, r_s[...], -1.0)
    slot_ref[0] = slot
    eye = (lax.broadcasted_iota(jnp.int32, (e, e), 0) == lax.broadcasted_iota(jnp.int32, (e, e), 1)).astype(F32)
    slot_t_ref[0] = _dot(slot, eye, TN, precision=HI)
    gv_t_ref[0] = _dot(aff, eye, TN, precision=HI)


def _topk(aff_t, blk, ntok, cap):
    bsz, e, _ = aff_t.shape
    return pl.pallas_call(
        functools.partial(_topk_kernel, cap=cap),
        grid=(bsz,),
        in_specs=[pl.BlockSpec((1, e, ntok), lambda b: (b, 0, blk))],
        out_specs=[pl.BlockSpec((1, e, ntok), lambda b: (b, 0, 0)),
                   pl.BlockSpec((1, ntok, e), lambda b: (b, 0, 0)),
                   pl.BlockSpec((1, ntok, e), lambda b: (b, 0, 0))],
        out_shape=[_sds((bsz, e, ntok), F32), _sds((bsz, ntok, e), F32), _sds((bsz, ntok, e), F32)],
        scratch_shapes=[pltpu.VMEM((e, ntok), F32), pltpu.VMEM((e, ntok), F32)],
        compiler_params=_cparams("parallel"),
        name="topk",
    )(aff_t)


def _expert_up_kernel(slot_ref, x_ref, wg_ref, wu_ref, o_ref, p_s, g_s, u_s):
    k = pl.program_id(2)
    capp, n = p_s.shape

    @pl.when(k == 0)
    def _():
        srow = slot_ref[0, pl.ds(pl.program_id(1), 1), :]
        sidx = lax.broadcasted_iota(jnp.int32, (capp, n), 0).astype(F32)
        p_s[...] = jnp.where(srow == sidx, 1.0, 0.0).astype(BF16)
        g_s[...] = jnp.zeros(g_s.shape, F32)
        u_s[...] = jnp.zeros(u_s.shape, F32)

    xg = _dot(p_s[...], x_ref[0]).astype(BF16)
    g_s[...] += _dot(xg, wg_ref[0])
    u_s[...] += _dot(xg, wu_ref[0])

    @pl.when(k == pl.num_programs(2) - 1)
    def _():
        o_ref[0, 0] = (jax.nn.silu(g_s[...]) * u_s[...]).astype(o_ref.dtype)


def _expert_up(slot, xm, blk, ntok, capp, w_gate, w_up):
    bsz, e = slot.shape[:2]
    d, ff = w_gate.shape[1:]
    tk = 512
    return pl.pallas_call(
        _expert_up_kernel,
        grid=(bsz, e, d // tk),
        in_specs=[pl.BlockSpec((1, e, ntok), lambda b, ei, k: (b, 0, 0)),
                  pl.BlockSpec((1, ntok, tk), lambda b, ei, k: (b, blk, k)),
                  pl.BlockSpec((1, tk, ff), lambda b, ei, k: (ei, k, 0)),
                  pl.BlockSpec((1, tk, ff), lambda b, ei, k: (ei, k, 0))],
        out_specs=pl.BlockSpec((1, 1, capp, ff), lambda b, ei, k: (b, ei, 0, 0)),
        out_shape=_sds((bsz, e, capp, ff), BF16),
        scratch_shapes=[pltpu.VMEM((capp, ntok), BF16), pltpu.VMEM((capp, ff), F32), pltpu.VMEM((capp, ff), F32)],
        compiler_params=_cparams("parallel", "parallel", "arbitrary"),
        name="expert_up",
    )(slot, xm, w_gate, w_up)


def _expert_down_kernel(hid_ref, wd_ref, slot_t_ref, gv_t_ref, o_ref):
    ei = pl.program_id(2)
    n, e = slot_t_ref.shape[1:]
    capp = hid_ref.shape[2]
    tn = o_ref.shape[-1]

    @pl.when(ei == 0)
    def _():
        o_ref[...] = jnp.zeros(o_ref.shape, F32)

    ye = _dot(hid_ref[0, 0], wd_ref[0]).astype(BF16)
    pick = (lax.broadcasted_iota(jnp.int32, (e, LANE), 0) == ei).astype(F32)
    slot_b = _dot(slot_t_ref[0], pick, precision=HI)
    gv_b = _dot(gv_t_ref[0], pick, precision=HI)
    lane = lax.broadcasted_iota(jnp.int32, (1, LANE), 1).astype(F32)
    pt = jnp.concatenate([jnp.where(slot_b == lane + float(c * LANE), 1.0, 0.0).astype(BF16) for c in range(capp // LANE)],
                         axis=1)
    contrib = _dot(pt, ye)
    for c in range(tn // LANE):
        o_ref[0, :, c * LANE:(c + 1) * LANE] += contrib[:, c * LANE:(c + 1) * LANE] * gv_b


def _expert_down(hid, w_down, slot_t, gv_t, blk, ntok, t):
    bsz, e, capp, ff = hid.shape
    d = w_down.shape[-1]
    tn = 256
    return pl.pallas_call(
        _expert_down_kernel,
        grid=(bsz, d // tn, e),
        in_specs=[pl.BlockSpec((1, 1, capp, ff), lambda b, j, ei: (b, ei, 0, 0)),
                  pl.BlockSpec((1, ff, tn), lambda b, j, ei: (ei, 0, j)),
                  pl.BlockSpec((1, ntok, e), lambda b, j, ei: (b, 0, 0)),
                  pl.BlockSpec((1, ntok, e), lambda b, j, ei: (b, 0, 0))],
        out_specs=pl.BlockSpec((1, ntok, tn), lambda b, j, ei: (b, 0, j)),
        out_shape=_sds((bsz, ntok, d), F32),
        compiler_params=_cparams("parallel", "parallel", "arbitrary"),
        name="expert_down",
    )(hid, w_down, slot_t, gv_t)


def _round_up(x, m):
    return (x + m - 1) // m * m


def _moe_part(aff_t, xm, blk, ntok, we):
    e = aff_t.shape[1]
    cap = EC_CAPACITY * ntok // e
    capp = _round_up(cap, LANE)
    slot, slot_t, gv_t = _topk(aff_t, blk, ntok, cap)
    hid = _expert_up(slot, xm, blk, ntok, capp, we["gate"], we["up"])
    return _expert_down(hid, we["down"], slot_t, gv_t, blk, ntok, xm.shape[1])


def kernel(x, c, ctx, c_ctx, w_mod, b_mod, w_in, b_in, conv_a_w, conv_a_b, lru_wa, lru_ba, lru_wx, lru_bx, lru_lam, gla_wa2, gla_ba, gla_norm_g, conv_c_w, conv_c_b, mlstm_norm_g, w_branch, w_out, ln1_g, ln1_b, w_router, w_e_gate, w_e_up, w_e_down, ln2_g, ln2_b):
    bsz, n_lat, d = x.shape
    nc = ctx.shape[1]
    t = n_lat + nc
    depth = w_mod.shape[0]
    d_rnn = conv_a_w.shape[-1]
    kg, vg = gla_ba.shape[-1], gla_norm_g.shape[-1]
    rank = gla_wa2.shape[2]
    wm = mlstm_norm_g.shape[-1]
    n_gate = 4 * MLSTM_HEADS
    n_exp = w_router.shape[-1]
    alpha = (2 * depth) ** 0.25
    assert 2 * rank + n_gate <= LANE and t % ROW_TILE == 0 and n_lat % ROW_TILE == 0 and bsz < SUBLANE

    sizes = (d_rnn, d_rnn, kg, kg, vg, vg, 2 * rank, wm, wm, wm, wm, n_gate)
    offs = [0]
    for sz in sizes:
        offs.append(offs[-1] + sz)
    n_feat = offs[-1]
    take_main = lambda a: jnp.concatenate([a[..., offs[0]:offs[6]], a[..., offs[7]:offs[11]]], axis=-1)
    take_small = lambda a: jnp.concatenate([a[..., offs[6]:offs[7]], a[..., offs[11]:offs[12]]], axis=-1)
    n_main = (offs[6] - offs[0]) + (offs[11] - offs[7])
    pad_s = LANE - (2 * rank + n_gate)
    mo = {"a_x": 0, "a_g": d_rnn, "b_q": 2 * d_rnn, "b_k": 2 * d_rnn + kg, "b_v": 2 * d_rnn + 2 * kg,
          "b_r": 2 * d_rnn + 2 * kg + vg}
    mo["c_q"] = mo["b_r"] + vg
    mo["c_k"], mo["c_v"], mo["c_o"] = mo["c_q"] + wm, mo["c_q"] + 2 * wm, mo["c_q"] + 3 * wm

    cvec = jnp.concatenate([c, c_ctx[None], jnp.zeros((SUBLANE - bsz - 1, d), F32)], axis=0)
    mod = _mod_all(cvec, w_mod, b_mod)
    xs = jnp.concatenate([x, ctx], axis=1)

    for l in range(depth):
        last = l == depth - 1
        mod6 = mod[l].reshape(SUBLANE, 6, d)
        w_l = w_in[l]
        w_main = take_main(w_l).astype(BF16)
        b_main = take_main(b_in[l]).reshape(1, n_main)
        w_small = jnp.pad(take_small(w_l), ((0, 0), (0, pad_s))).astype(BF16)
        b_small = jnp.pad(take_small(b_in[l]), (0, pad_s)).reshape(1, LANE)
        tn = MERGE_TN
        wg3 = w_l[:, n_feat:].reshape(d, 3, d // tn, tn).transpose(0, 2, 1, 3).reshape(d, 3 * d).astype(BF16)
        bg3 = b_in[l][n_feat:].reshape(3, d // tn, tn).transpose(1, 0, 2).reshape(1, 3 * d)
        lru_p = {"conv_w": conv_a_w[l], "conv_b": conv_a_b[l].reshape(1, d_rnn),
                 "wa": lru_wa[l].astype(BF16), "ba": lru_ba[l].reshape(2, 1, d_rnn),
                 "wx": lru_wx[l].astype(BF16), "bx": lru_bx[l].reshape(2, 1, d_rnn),
                 "lam": lru_lam[l].reshape(2, 1, d_rnn)}
        wlr = jnp.zeros((2, LANE, kg), F32)
        for dd in range(2):
            wlr = wlr.at[dd, dd * rank:(dd + 1) * rank].set(gla_wa2[l, dd])
        gla_p = {"wlr": wlr, "ba": gla_ba[l].reshape(2, 1, kg), "norm_g": gla_norm_g[l].reshape(1, vg)}
        ml_p = {"conv_w": conv_c_w[l], "conv_b": conv_c_b[l].reshape(1, 2 * wm), "norm_g": mlstm_norm_g[l].reshape(1, wm)}

        h = _modulate(xs, mod6, 0, n_lat).reshape(bsz * t, d)
        feat = _matmul(h, w_main, b_main, F32, 512, 1024, "feat").reshape(bsz, t, n_main)
        feat_s = _matmul(h, w_small, b_small, F32, 1024, LANE, "feat_small").reshape(bsz, t, LANE)
        y0 = _rglru(feat, lru_p, d_rnn, n_lat)
        y1 = _gla(feat, feat_s, gla_p, (mo["b_q"], mo["b_k"], mo["b_v"], mo["b_r"]), n_lat)
        y2 = _mlstm(feat, feat_s, ml_p, (mo["c_q"], mo["c_k"], mo["c_v"], mo["c_o"]), 2 * rank, n_lat)
        ys = [y.reshape(bsz * t, -1) for y in (y0, y1, y2)]
        merged = _merge(h, wg3, bg3, ys, w_branch[l].astype(BF16))
        y = _matmul(merged, w_out[l].astype(BF16), jnp.zeros((1, d), F32), F32, 512, 1024, "out_proj").reshape(bsz, t, d)
        xs = _res_ln(xs, y, mod6, 2, ln1_g[l], ln1_b[l], alpha, n_lat, t)

        xm = _modulate(xs, mod6, 3, n_lat)
        aff_t = _router(xm, w_router[l].T.astype(BF16))
        we = {"gate": w_e_gate[l].astype(BF16), "up": w_e_up[l].astype(BF16), "down": w_e_down[l].astype(BF16)}
        f = _moe_part(aff_t, xm, 0, n_lat, we)
        if not last:
            f_ctx = _moe_part(aff_t, xm, n_lat // nc, nc, we)
            f = jnp.concatenate([f, f_ctx], axis=1)
            xs = _res_ln(xs, f, mod6, 5, ln2_g[l], ln2_b[l], alpha, n_lat, t)
        else:
            xs = _res_ln(xs, f, mod6, 5, ln2_g[l], ln2_b[l], alpha, n_lat, n_lat)
    return xs
```

```python
import functools

import jax
import jax.numpy as jnp
from jax import lax
from jax.experimental import pallas as pl
from jax.experimental.pallas import tpu as pltpu

F32 = jnp.float32
BF16 = jnp.bfloat16
HI = lax.Precision.HIGHEST

GRID_W = 64
CHUNK = 64
LRU_C = 8.0
GLA_HEADS = 4
GLA_TAU = 16.0
MLSTM_HEADS = 4
EC_CAPACITY = 2
LN_EPS = 1e-5

LANE = 128
SUBLANE = 8
ROW_TILE = 256
VMEM_LIMIT = 48 << 20

NT = (((1,), (1,)), ((), ()))
TN = (((0,), (0,)), ((), ()))


def _cparams(*sem):
    return pltpu.CompilerParams(dimension_semantics=sem, vmem_limit_bytes=VMEM_LIMIT)


def _sds(shape, dtype):
    return jax.ShapeDtypeStruct(shape, dtype)


def _dot(a, b, dims=None, precision=None):
    if dims is None:
        return jnp.dot(a, b, preferred_element_type=F32, precision=precision)
    return lax.dot_general(a, b, dims, preferred_element_type=F32, precision=precision)


def _mod_kernel(c_ref, w_ref, b_ref, o_ref):
    a = jax.nn.silu(c_ref[...]).astype(BF16)
    o_ref[0] = _dot(a, w_ref[0].astype(BF16)) + b_ref[0]


def _mod_all(cvec, w_mod, b_mod):
    depth, d, n6 = w_mod.shape
    tn = 512
    return pl.pallas_call(
        _mod_kernel,
        grid=(depth, n6 // tn),
        in_specs=[pl.BlockSpec((SUBLANE, d), lambda l, j: (0, 0)),
                  pl.BlockSpec((1, d, tn), lambda l, j: (l, 0, j)),
                  pl.BlockSpec((1, 1, tn), lambda l, j: (l, 0, j))],
        out_specs=pl.BlockSpec((1, SUBLANE, tn), lambda l, j: (l, 0, j)),
        out_shape=_sds((depth, SUBLANE, n6), F32),
        compiler_params=_cparams("parallel", "parallel"),
        name="mod",
    )(cvec, w_mod, b_mod.reshape(depth, 1, n6))


def _modulate_kernel(x_ref, m_ref, o_ref, *, off):
    sh = m_ref[0, off:off + 1, :]
    sc = m_ref[0, off + 1:off + 2, :]
    o_ref[0] = (x_ref[0] * (1.0 + sc) + sh).astype(o_ref.dtype)


def _mod_row_map(n_lat_tiles, batch):
    return lambda b, i: (jnp.where(i < n_lat_tiles, b, batch), 0, 0)


def _modulate(xs, mod6, off, n_lat):
    batch, t, d = xs.shape
    tr = ROW_TILE
    return pl.pallas_call(
        functools.partial(_modulate_kernel, off=off),
        grid=(batch, t // tr),
        in_specs=[pl.BlockSpec((1, tr, d), lambda b, i: (b, i, 0)),
                  pl.BlockSpec((1, 6, d), _mod_row_map(n_lat // tr, batch))],
        out_specs=pl.BlockSpec((1, tr, d), lambda b, i: (b, i, 0)),
        out_shape=_sds((batch, t, d), BF16),
        compiler_params=_cparams("parallel", "parallel"),
        name="modulate",
    )(xs, mod6)


def _mm_kernel(a_ref, w_ref, b_ref, o_ref):
    o_ref[...] = (_dot(a_ref[...], w_ref[...]) + b_ref[...]).astype(o_ref.dtype)


def _pick_tile(n, pref):
    while pref > LANE and n % pref:
        pref //= 2
    assert n % pref == 0
    return pref


def _mm_castw_kernel(a_ref, w_ref, b_ref, o_ref, w_s):
    @pl.when(pl.program_id(1) == 0)
    def _():
        w_s[...] = w_ref[...].astype(BF16)

    o_ref[...] = (_dot(a_ref[...], w_s[...]) + b_ref[...]).astype(o_ref.dtype)


def _matmul(a, w, bias, out_dtype, tm, tn, name):
    m, k = a.shape
    n = w.shape[1]
    tm, tn = _pick_tile(m, tm), _pick_tile(n, tn)
    if w.dtype == F32:
        return pl.pallas_call(
            _mm_castw_kernel,
            grid=(n // tn, m // tm),
            in_specs=[pl.BlockSpec((tm, k), lambda j, i: (i, 0)),
                      pl.BlockSpec((k, tn), lambda j, i: (0, j)),
                      pl.BlockSpec((1, tn), lambda j, i: (0, j))],
            out_specs=pl.BlockSpec((tm, tn), lambda j, i: (i, j)),
            out_shape=_sds((m, n), out_dtype),
            scratch_shapes=[pltpu.VMEM((k, tn), BF16)],
            compiler_params=_cparams("parallel", "arbitrary"),
            name=name,
        )(a, w, bias)
    return pl.pallas_call(
        _mm_kernel,
        grid=(n // tn, m // tm),
        in_specs=[pl.BlockSpec((tm, k), lambda j, i: (i, 0)),
                  pl.BlockSpec((k, tn), lambda j, i: (0, j)),
                  pl.BlockSpec((1, tn), lambda j, i: (0, j))],
        out_specs=pl.BlockSpec((tm, tn), lambda j, i: (i, j)),
        out_shape=_sds((m, n), out_dtype),
        compiler_params=_cparams("parallel", "parallel"),
        name=name,
    )(a, w, bias)


def _res_ln_kernel(x_ref, y_ref, m_ref, g_ref, b_ref, o_ref, *, off, alpha):
    gate = m_ref[0, off:off + 1, :]
    z = alpha * x_ref[0] + gate * y_ref[0]
    mu = jnp.mean(z, axis=-1, keepdims=True)
    zc = z - mu
    var = jnp.mean(zc * zc, axis=-1, keepdims=True)
    o_ref[0] = zc * lax.rsqrt(var + LN_EPS) * g_ref[...] + b_ref[...]


def _res_ln(xs, y, mod6, off, ln_g, ln_b, alpha, n_lat, t_out):
    batch, t, d = xs.shape
    tr = ROW_TILE
    return pl.pallas_call(
        functools.partial(_res_ln_kernel, off=off, alpha=alpha),
        grid=(batch, t_out // tr),
        in_specs=[pl.BlockSpec((1, tr, d), lambda b, i: (b, i, 0)),
                  pl.BlockSpec((1, tr, d), lambda b, i: (b, i, 0)),
                  pl.BlockSpec((1, 6, d), _mod_row_map(n_lat // tr, batch)),
                  pl.BlockSpec((1, d), lambda b, i: (0, 0)),
                  pl.BlockSpec((1, d), lambda b, i: (0, 0))],
        out_specs=pl.BlockSpec((1, tr, d), lambda b, i: (b, i, 0)),
        out_shape=_sds((batch, t_out, d), F32),
        compiler_params=_cparams("parallel", "parallel"),
        name="res_ln",
    )(xs, y, mod6, ln_g.reshape(1, d), ln_b.reshape(1, d))


def _gelu_tanh(x):
    return jax.nn.gelu(x, approximate=True)


def _lru_gates(u2, wa_ref, ba_ref, wx_ref, bx_ref, lam_ref):
    ub = u2.astype(BF16)
    r = jax.nn.sigmoid(_dot(ub, wa_ref[0]) + ba_ref[...])
    i = jax.nn.sigmoid(_dot(ub, wx_ref[0]) + bx_ref[...])
    log_a = (-LRU_C * jax.nn.softplus(-lam_ref[...])) * r
    a = jnp.exp(log_a)
    bt = jnp.sqrt(1.0 - jnp.exp(2.0 * log_a)) * (i * u2)
    return a, bt


def _lru_lat_kernel(*refs, rev, n_cg):
    if rev:
        (x_ref, pv_ref, nx_ref, cw_ref, cb_ref, wa_ref, ba_ref, wx_ref, bx_ref, lam_ref, e0_ref,
         ag_ref, hf_ref, _alias, o_ref, a_s, b_s, carry) = refs
    else:
        (x_ref, pv_ref, nx_ref, cw_ref, cb_ref, wa_ref, ba_ref, wx_ref, bx_ref, lam_ref, e0_ref,
         _alias, o_ref, a_s, b_s, carry) = refs
    s = pl.program_id(1)
    cg = (n_cg - 1 - s) if rev else s
    bsz, rows, ncol, cb = x_ref.shape

    @pl.when(s == 0)
    def _():
        carry[...] = e0_ref[...]

    x = x_ref[...]
    col = lax.broadcasted_iota(jnp.int32, (1, ncol, 1), 1)
    not_first = (cg > 0).astype(F32)
    not_last = (cg < n_cg - 1).astype(F32)
    top = jnp.where(col == 0, pltpu.roll(pv_ref[:, SUBLANE - 1], 1, 1) * not_first, pltpu.roll(x[:, rows - 1], 1, 1))
    bot1 = jnp.where(col == ncol - 1, pltpu.roll(nx_ref[:, 0], ncol - 1, 1) * not_last, pltpu.roll(x[:, 0], ncol - 1, 1))
    bot2 = jnp.where(col == ncol - 1, pltpu.roll(nx_ref[:, 1], ncol - 1, 1) * not_last, pltpu.roll(x[:, 1], ncol - 1, 1))
    xe = jnp.concatenate([top[:, None], x, bot1[:, None], bot2[:, None]], axis=1)
    u = cb_ref[...].reshape(1, 1, 1, cb)
    for k in range(4):
        u = u + cw_ref[k:k + 1, :].reshape(1, 1, 1, cb) * xe[:, k:k + rows]
    a, bt = _lru_gates(u.reshape(bsz * rows * ncol, cb), wa_ref, ba_ref, wx_ref, bx_ref, lam_ref)
    a_s[...] = a.reshape(bsz, rows, ncol, cb)
    b_s[...] = bt.reshape(bsz, rows, ncol, cb)

    def body(t, hp):
        h, p = hp
        r = (rows - 1 - t) if rev else t
        a_t = a_s[:, r]
        h = a_t * h + b_s[:, r]
        p = p * a_t
        b_s[:, r] = h
        a_s[:, r] = p
        return h, p

    h_end, p_end = lax.fori_loop(0, rows, body, (jnp.zeros((bsz, ncol, cb), F32), jnp.ones((bsz, ncol, cb), F32)), unroll=8)
    av, bv = p_end, h_end
    sh = 1
    while sh < ncol:
        if rev:
            valid = col < ncol - sh
            amt = ncol - sh
        else:
            valid = col >= sh
            amt = sh
        b_sh = jnp.where(valid, pltpu.roll(bv, amt, 1), 0.0)
        a_sh = jnp.where(valid, pltpu.roll(av, amt, 1), 1.0)
        bv = bv + av * b_sh
        av = av * a_sh
        sh *= 2
    e_prev = carry[...]
    e = bv + av * e_prev
    if rev:
        c_in = jnp.where(col == ncol - 1, e_prev, pltpu.roll(e, ncol - 1, 1))
        carry[...] = jnp.broadcast_to(e[:, 0:1], e.shape)
    else:
        c_in = jnp.where(col == 0, e_prev, pltpu.roll(e, 1, 1))
        carry[...] = jnp.broadcast_to(e[:, ncol - 1:ncol], e.shape)
    h = b_s[...] + a_s[...] * c_in[:, None]
    if rev:
        o_ref[...] = (hf_ref[...] + h) * _gelu_tanh(ag_ref[...])
    else:
        o_ref[...] = h


def _lru_ctx_kernel(*refs, rev):
    if rev:
        (x_ref, cw_ref, cb_ref, wa_ref, ba_ref, wx_ref, bx_ref, lam_ref, ag_ref, hf_ref, o_ref, e_ref) = refs
    else:
        (x_ref, cw_ref, cb_ref, wa_ref, ba_ref, wx_ref, bx_ref, lam_ref, o_ref, e_ref) = refs
    bsz, nc, cb = x_ref.shape
    x = x_ref[...]
    t = lax.broadcasted_iota(jnp.int32, (1, nc, 1), 1)
    xm1 = jnp.where(t >= 1, pltpu.roll(x, 1, 1), 0.0)
    xp1 = jnp.where(t < nc - 1, pltpu.roll(x, nc - 1, 1), 0.0)
    xp2 = jnp.where(t < nc - 2, pltpu.roll(x, nc - 2, 1), 0.0)
    w = [cw_ref[k:k + 1, :].reshape(1, 1, cb) for k in range(4)]
    u = w[0] * xm1 + w[1] * x + w[2] * xp1 + w[3] * xp2 + cb_ref[...].reshape(1, 1, cb)
    a, bt = _lru_gates(u.reshape(bsz * nc, cb), wa_ref, ba_ref, wx_ref, bx_ref, lam_ref)
    av = a.reshape(bsz, nc, cb)
    bv = bt.reshape(bsz, nc, cb)
    sh = 1
    while sh < nc:
        if rev:
            valid = t < nc - sh
            amt = nc - sh
        else:
            valid = t >= sh
            amt = sh
        b_sh = jnp.where(valid, pltpu.roll(bv, amt, 1), 0.0)
        a_sh = jnp.where(valid, pltpu.roll(av, amt, 1), 1.0)
        bv = bv + av * b_sh
        av = av * a_sh
        sh *= 2
    if rev:
        o_ref[...] = (hf_ref[...] + bv) * _gelu_tanh(ag_ref[...])
        e_ref[...] = jnp.broadcast_to(bv[:, 0:1], (bsz, SUBLANE, cb))
    else:
        o_ref[...] = bv
        e_ref[...] = jnp.broadcast_to(bv[:, nc - 1:nc], (bsz, SUBLANE, cb))


def _rglru(feat, p, d_rnn, n_lat):
    bsz, t, nf = feat.shape
    nc = t - n_lat
    nb, bs, _ = p["wa"][0].shape
    cb = bs
    assert d_rnn == nb * bs and cb % LANE == 0 and n_lat % nc == 0 and nc % GRID_W == 0
    rows = n_lat // GRID_W
    assert rows % SUBLANE == 0 and GRID_W % SUBLANE == 0
    n_cg = GRID_W // SUBLANE
    ag_off = d_rnn // cb
    feat4 = feat.reshape(bsz, t // GRID_W, GRID_W, nf)
    ctx_blk = n_lat // nc

    def wspecs(im):
        return [pl.BlockSpec((4, cb), im(lambda j: (0, j))),
                pl.BlockSpec((1, cb), im(lambda j: (0, j))),
                pl.BlockSpec((1, bs, bs), im(lambda j: (j, 0, 0))),
                pl.BlockSpec((1, cb), im(lambda j: (0, j))),
                pl.BlockSpec((1, bs, bs), im(lambda j: (j, 0, 0))),
                pl.BlockSpec((1, cb), im(lambda j: (0, j))),
                pl.BlockSpec((1, cb), im(lambda j: (0, j)))]

    def wargs(d):
        return [p["conv_w"], p["conv_b"], p["wa"][d], p["ba"][d], p["wx"][d], p["bx"][d], p["lam"][d]]

    im1 = lambda f: (lambda j: f(j))
    im2 = lambda f: (lambda j, s: f(j))
    hf = None
    out = None
    for rev in (False, True):
        in_specs = [pl.BlockSpec((bsz, nc, cb), lambda j: (0, ctx_blk, j))] + wspecs(im1)
        args = [feat] + wargs(int(rev))
        if rev:
            in_specs += [pl.BlockSpec((bsz, nc, cb), lambda j: (0, ctx_blk, ag_off + j)),
                         pl.BlockSpec((bsz, nc, cb), lambda j: (0, ctx_blk, j))]
            args += [feat, hf]
        part, e0 = pl.pallas_call(
            functools.partial(_lru_ctx_kernel, rev=rev),
            grid=(nb,),
            in_specs=in_specs,
            out_specs=[pl.BlockSpec((bsz, nc, cb), lambda j: (0, ctx_blk, j)),
                       pl.BlockSpec((bsz, SUBLANE, cb), lambda j: (0, 0, j))],
            out_shape=[_sds((bsz, t, d_rnn), F32), _sds((bsz, SUBLANE, d_rnn), F32)],
            compiler_params=_cparams("parallel"),
            name="lru_ctx_bwd" if rev else "lru_ctx_fwd",
        )(*args)
        cgm = (lambda s: n_cg - 1 - s) if rev else (lambda s: s)
        in_specs = [pl.BlockSpec((bsz, rows, SUBLANE, cb), lambda j, s: (0, 0, cgm(s), j)),
                    pl.BlockSpec((bsz, SUBLANE, SUBLANE, cb), lambda j, s: (0, rows // SUBLANE - 1, jnp.maximum(cgm(s) - 1, 0), j)),
                    pl.BlockSpec((bsz, SUBLANE, SUBLANE, cb), lambda j, s: (0, 0, jnp.minimum(cgm(s) + 1, n_cg - 1), j))]
        in_specs += wspecs(im2) + [pl.BlockSpec((bsz, SUBLANE, cb), lambda j, s: (0, 0, j))]
        args = [feat4, feat4, feat4] + wargs(int(rev)) + [e0]
        if rev:
            in_specs += [pl.BlockSpec((bsz, rows, SUBLANE, cb), lambda j, s: (0, 0, cgm(s), ag_off + j)),
                         pl.BlockSpec((bsz, rows, SUBLANE, cb), lambda j, s: (0, 0, cgm(s), j))]
            args += [feat4, hf.reshape(bsz, t // GRID_W, GRID_W, d_rnn)]
        in_specs += [pl.BlockSpec(memory_space=pl.ANY)]
        args += [part.reshape(bsz, t // GRID_W, GRID_W, d_rnn)]
        res = pl.pallas_call(
            functools.partial(_lru_lat_kernel, rev=rev, n_cg=n_cg),
            grid=(nb, n_cg),
            in_specs=in_specs,
            out_specs=pl.BlockSpec((bsz, rows, SUBLANE, cb), lambda j, s: (0, 0, cgm(s), j)),
            out_shape=_sds((bsz, t // GRID_W, GRID_W, d_rnn), F32),
            scratch_shapes=[pltpu.VMEM((bsz, rows, SUBLANE, cb), F32),
                            pltpu.VMEM((bsz, rows, SUBLANE, cb), F32),
                            pltpu.VMEM((bsz, SUBLANE, cb), F32)],
            input_output_aliases={len(args) - 1: 0},
            compiler_params=_cparams("parallel", "arbitrary"),
            name="lru_lat_bwd" if rev else "lru_lat_fwd",
        )(*args)
        res = res.reshape(bsz, t, d_rnn)
        if rev:
            out = res
        else:
            hf = res
    return out


def _chunk_of(s, n_ch, n_lat_ch, rev):
    if rev:
        return n_ch - 1 - s
    return jnp.where(s < n_ch - n_lat_ch, s + n_lat_ch, s - (n_ch - n_lat_ch))


def _head_norm(o, g):
    mu = jnp.mean(o, axis=-1, keepdims=True)
    oc = o - mu
    var = jnp.mean(oc * oc, axis=-1, keepdims=True)
    return oc * lax.rsqrt(var + LN_EPS) * g


def _gla_kernel(*refs, rev, heads):
    if rev:
        (q_ref, k_ref, v_ref, lr_ref, wlr_ref, ba_ref, tri_ref, rsel_ref, r_ref, of_ref, ng_ref, o_ref, st) = refs
    else:
        (q_ref, k_ref, v_ref, lr_ref, wlr_ref, ba_ref, tri_ref, rsel_ref, o_ref, st) = refs
    s = pl.program_id(1)

    @pl.when(s == 0)
    def _():
        st[...] = jnp.zeros(st.shape, F32)

    dk = q_ref.shape[-1] // heads
    dv = v_ref.shape[-1] // heads
    tri = tri_ref[...]
    g = jax.nn.log_sigmoid(_dot(lr_ref[0], wlr_ref[...], precision=HI) + ba_ref[...]) / GLA_TAU
    bcum = _dot(tri, g, precision=HI)
    bm = _dot(rsel_ref[...], g, precision=HI)
    q = q_ref[0] * dk ** -0.5
    k = k_ref[0]
    v = v_ref[0]
    for h in range(heads):
        sk = slice(h * dk, (h + 1) * dk)
        sv = slice(h * dv, (h + 1) * dv)
        qh, kh, bh = q[:, sk], k[:, sk], bcum[:, sk]
        vh = v[:, sv].astype(BF16)
        bmid, blast = bm[0:1, sk], bm[1:2, sk]
        att = _dot((qh * jnp.exp(bh - bmid)).astype(BF16), (kh * jnp.exp(bmid - bh)).astype(BF16), NT) * tri
        sth = st[h]
        o = _dot(att.astype(BF16), vh) + _dot((qh * jnp.exp(bh)).astype(BF16), sth.astype(BF16), NT)
        st[h] = jnp.exp(blast) * sth + _dot(vh, (kh * jnp.exp(blast - bh)).astype(BF16), TN)
        if rev:
            o = _head_norm(of_ref[0, :, sv] + o, ng_ref[:, sv]) * jax.nn.silu(r_ref[0, :, sv])
        o_ref[0, :, sv] = o


def _scan_consts(rev):
    i = jnp.arange(CHUNK)
    tri = (i[None, :] >= i[:, None]) if rev else (i[None, :] <= i[:, None])
    tri = tri.astype(F32)
    mid = CHUNK // 2 if rev else CHUNK // 2 - 1
    last = 0 if rev else CHUNK - 1
    rsel = jnp.zeros((SUBLANE, CHUNK), F32).at[0].set(tri[mid]).at[1].set(tri[last])
    return tri, rsel


def _gla(feat, feat_s, p, offs, n_lat):
    bsz, t, _ = feat.shape
    kg, vg = p["ba"][0].shape[-1], p["norm_g"].shape[-1]
    oq, ok, ov, orr = offs
    assert oq % kg == 0 and ok % kg == 0 and ov % vg == 0 and orr % vg == 0
    n_ch, n_lat_ch = t // CHUNK, n_lat // CHUNK
    of = None
    for rev in (False, True):
        d = int(rev)
        tri, rsel = _scan_consts(rev)
        cm = lambda b, s: (b, _chunk_of(s, n_ch, n_lat_ch, rev))
        const = lambda b, s: (0, 0)
        in_specs = [pl.BlockSpec((1, CHUNK, kg), lambda b, s: cm(b, s) + (oq // kg,)),
                    pl.BlockSpec((1, CHUNK, kg), lambda b, s: cm(b, s) + (ok // kg,)),
                    pl.BlockSpec((1, CHUNK, vg), lambda b, s: cm(b, s) + (ov // vg,)),
                    pl.BlockSpec((1, CHUNK, LANE), lambda b, s: cm(b, s) + (0,)),
                    pl.BlockSpec((LANE, kg), const),
                    pl.BlockSpec((1, kg), const),
                    pl.BlockSpec((CHUNK, CHUNK), const),
                    pl.BlockSpec((SUBLANE, CHUNK), const)]
        args = [feat, feat, feat, feat_s, p["wlr"][d], p["ba"][d], tri, rsel]
        if rev:
            in_specs += [pl.BlockSpec((1, CHUNK, vg), lambda b, s: cm(b, s) + (orr // vg,)),
                         pl.BlockSpec((1, CHUNK, vg), lambda b, s: cm(b, s) + (0,)),
                         pl.BlockSpec((1, vg), const)]
            args += [feat, of, p["norm_g"]]
        res = pl.pallas_call(
            functools.partial(_gla_kernel, rev=rev, heads=GLA_HEADS),
            grid=(bsz, n_ch),
            in_specs=in_specs,
            out_specs=pl.BlockSpec((1, CHUNK, vg), lambda b, s: cm(b, s) + (0,)),
            out_shape=_sds((bsz, t, vg), F32),
            scratch_shapes=[pltpu.VMEM((GLA_HEADS, vg // GLA_HEADS, kg // GLA_HEADS), F32)],
            compiler_params=_cparams("parallel", "arbitrary"),
            name="gla_bwd" if rev else "gla_fwd",
        )(*args)
        of = res
    return of


def _conv_rows(x, prev_row, next_rows, w_ref, b_ref, lo, hi):
    n = x.shape[0]
    t = lax.broadcasted_iota(jnp.int32, (n, 1), 0)
    xm1 = jnp.where(t == 0, prev_row, pltpu.roll(x, 1, 0))
    xp1 = jnp.where(t == n - 1, next_rows[0:1], pltpu.roll(x, n - 1, 0))
    xp2 = jnp.where(t == n - 2, next_rows[0:1], jnp.where(t == n - 1, next_rows[1:2], pltpu.roll(x, n - 2, 0)))
    w = w_ref[:, lo:hi]
    return w[0:1] * xm1 + w[1:2] * x + w[2:3] * xp1 + w[3:4] * xp2 + b_ref[:, lo:hi]


def _mlstm_kernel(*refs, rev, heads, n_ch, n_lat_ch, g_off):
    if rev:
        (q_ref, qp_ref, qn_ref, k_ref, kp_ref, kn_ref, v_ref, cg_ref, cw_ref, cbias_ref, tri_ref, trit_ref,
         og_ref, hf_ref, ng_ref, o_ref, c_s, n_s, m_s) = refs
    else:
        (q_ref, qp_ref, qn_ref, k_ref, kp_ref, kn_ref, v_ref, cg_ref, cw_ref, cbias_ref, tri_ref, trit_ref,
         o_ref, c_s, n_s, m_s) = refs
    s = pl.program_id(1)

    @pl.when(s == 0)
    def _():
        c_s[...] = jnp.zeros(c_s.shape, F32)
        n_s[...] = jnp.zeros(n_s.shape, F32)
        m_s[...] = jnp.zeros(m_s.shape, F32)

    ch = _chunk_of(s, n_ch, n_lat_ch, rev)
    not_first = jnp.logical_and(ch != 0, ch != n_lat_ch).astype(F32)
    not_last = jnp.logical_and(ch != n_lat_ch - 1, ch != n_ch - 1).astype(F32)
    wm = q_ref.shape[-1]
    dh = wm // heads
    qc = jax.nn.silu(_conv_rows(q_ref[0], qp_ref[0, SUBLANE - 1:SUBLANE] * not_first, qn_ref[0, 0:2] * not_last,
                                cw_ref, cbias_ref, 0, wm))
    kc = jax.nn.silu(_conv_rows(k_ref[0], kp_ref[0, SUBLANE - 1:SUBLANE] * not_first, kn_ref[0, 0:2] * not_last,
                                cw_ref, cbias_ref, wm, 2 * wm)) * dh ** -0.5
    v = v_ref[0]
    tri = tri_ref[...]
    gts = cg_ref[0]
    gls = jax.nn.log_sigmoid(gts)
    bcol_all = _dot(tri, gls, precision=HI)
    ci0 = g_off + int(rev) * 2 * heads
    cf0 = ci0 + heads
    lane = lax.broadcasted_iota(jnp.int32, (SUBLANE, LANE), 1)
    row = lax.broadcasted_iota(jnp.int32, (SUBLANE, LANE), 0)
    sel_i = (lane == row + ci0).astype(F32)
    sel_f = (lane == row + cf0).astype(F32)
    ig_rows = _dot(sel_i, gts, NT, precision=HI)
    b_rows = _dot(_dot(sel_f, gls, NT, precision=HI), trit_ref[...], precision=HI)
    last = 0 if rev else CHUNK - 1
    for h in range(heads):
        sl = slice(h * dh, (h + 1) * dh)
        qh = qc[:, sl]
        qb, kb, vh = qh.astype(BF16), kc[:, sl].astype(BF16), v[:, sl]
        bc = bcol_all[:, cf0 + h:cf0 + h + 1]
        igc = gts[:, ci0 + h:ci0 + h + 1]
        m = m_s[h:h + 1, 0:1]
        dmat = jnp.where(tri > 0.0, bc - b_rows[h:h + 1, :] + ig_rows[h:h + 1, :], -jnp.inf)
        inter = bc + m
        m_row = jnp.maximum(jnp.max(dmat, axis=-1, keepdims=True), inter)
        pmat = _dot(qb, kb, NT) * jnp.exp(dmat - m_row)
        s_inter = jnp.exp(inter - m_row)
        cm = c_s[h]
        nv = n_s[h:h + 1, :]
        num = _dot(pmat.astype(BF16), vh.astype(BF16)) + s_inter * _dot(qb, cm.astype(BF16), NT)
        den = jnp.sum(pmat, axis=-1, keepdims=True) + s_inter * jnp.sum(qh * nv, axis=-1, keepdims=True)
        hout = num / jnp.maximum(jnp.abs(den), jnp.exp(-m_row))
        b_last = bc[last:last + 1]
        wl = b_last - bc + igc
        m_new = jnp.maximum(b_last + m, jnp.max(wl, axis=0, keepdims=True))
        sw = jnp.exp(wl - m_new)
        decay = jnp.exp(b_last + m - m_new)
        c_s[h] = decay * cm + _dot((sw * vh).astype(BF16), kb, TN)
        n_s[h:h + 1, :] = decay * nv + jnp.sum(sw * kc[:, sl], axis=0, keepdims=True)
        m_s[h:h + 1, :] = jnp.broadcast_to(m_new, (1, LANE))
        if rev:
            hout = _head_norm(hf_ref[0, :, sl] + hout, ng_ref[:, sl]) * jax.nn.sigmoid(og_ref[0, :, sl])
        o_ref[0, :, sl] = hout


def _mlstm(feat, feat_s, p, offs, g_off, n_lat):
    bsz, t, _ = feat.shape
    wm = p["norm_g"].shape[-1]
    oq, ok, ov, oo = offs
    assert all(o % wm == 0 for o in offs)
    n_ch, n_lat_ch = t // CHUNK, n_lat // CHUNK
    n_r8 = t // SUBLANE
    per = CHUNK // SUBLANE
    hf = None
    for rev in (False, True):
        tri, _ = _scan_consts(rev)
        chm = lambda s: _chunk_of(s, n_ch, n_lat_ch, rev)
        const = lambda b, s: (0, 0)

        def cur(off):
            return pl.BlockSpec((1, CHUNK, wm), lambda b, s: (b, chm(s), off // wm))

        def prv(off):
            return pl.BlockSpec((1, SUBLANE, wm), lambda b, s: (b, jnp.maximum(chm(s) * per - 1, 0), off // wm))

        def nxt(off):
            return pl.BlockSpec((1, SUBLANE, wm), lambda b, s: (b, jnp.minimum((chm(s) + 1) * per, n_r8 - 1), off // wm))

        in_specs = [cur(oq), prv(oq), nxt(oq), cur(ok), prv(ok), nxt(ok), cur(ov),
                    pl.BlockSpec((1, CHUNK, LANE), lambda b, s: (b, chm(s), 0)),
                    pl.BlockSpec((4, 2 * wm), const),
                    pl.BlockSpec((1, 2 * wm), const),
                    pl.BlockSpec((CHUNK, CHUNK), const),
                    pl.BlockSpec((CHUNK, CHUNK), const)]
        args = [feat] * 7 + [feat_s, p["conv_w"], p["conv_b"], tri, tri.T]
        if rev:
            in_specs += [cur(oo), pl.BlockSpec((1, CHUNK, wm), lambda b, s: (b, chm(s), 0)), pl.BlockSpec((1, wm), const)]
            args += [feat, hf, p["norm_g"]]
        dh = wm // MLSTM_HEADS
        hf = pl.pallas_call(
            functools.partial(_mlstm_kernel, rev=rev, heads=MLSTM_HEADS, n_ch=n_ch, n_lat_ch=n_lat_ch, g_off=g_off),
            grid=(bsz, n_ch),
            in_specs=in_specs,
            out_specs=pl.BlockSpec((1, CHUNK, wm), lambda b, s: (b, chm(s), 0)),
            out_shape=_sds((bsz, t, wm), F32),
            scratch_shapes=[pltpu.VMEM((MLSTM_HEADS, dh, dh), F32),
                            pltpu.VMEM((SUBLANE, dh), F32),
                            pltpu.VMEM((SUBLANE, LANE), F32)],
            compiler_params=_cparams("parallel", "arbitrary"),
            name="mlstm_bwd" if rev else "mlstm_fwd",
        )(*args)
    return hf


def _merge_kernel(h_ref, wg_ref, bg_ref, y0_ref, y1_ref, y2_ref, wb_ref, o_ref, *, tn):
    g = jax.nn.sigmoid(_dot(h_ref[...], wg_ref[...]) + bg_ref[...])
    acc = None
    for n, y_ref in enumerate((y0_ref, y1_ref, y2_ref)):
        term = g[:, n * tn:(n + 1) * tn] * _dot(y_ref[...].astype(BF16), wb_ref[n])
        acc = term if acc is None else acc + term
    o_ref[...] = acc.astype(o_ref.dtype)


MERGE_TN = 512


def _merge(h, wg3, bg3, ys, wb):
    m, d = h.shape
    r = ys[0].shape[1]
    tm, tn = 256, MERGE_TN
    return pl.pallas_call(
        functools.partial(_merge_kernel, tn=tn),
        grid=(d // tn, m // tm),
        in_specs=[pl.BlockSpec((tm, d), lambda j, i: (i, 0)),
                  pl.BlockSpec((d, 3 * tn), lambda j, i: (0, j)),
                  pl.BlockSpec((1, 3 * tn), lambda j, i: (0, j)),
                  pl.BlockSpec((tm, r), lambda j, i: (i, 0)),
                  pl.BlockSpec((tm, r), lambda j, i: (i, 0)),
                  pl.BlockSpec((tm, r), lambda j, i: (i, 0)),
                  pl.BlockSpec((3, r, tn), lambda j, i: (0, 0, j))],
        out_specs=pl.BlockSpec((tm, tn), lambda j, i: (i, j)),
        out_shape=_sds((m, d), BF16),
        compiler_params=_cparams("parallel", "parallel"),
        name="merge",
    )(h, wg3, bg3, *ys, wb)


def _router_kernel(x_ref, w_ref, o_ref):
    logits = _dot(w_ref[...], x_ref[0], NT)
    ex = jnp.exp(logits - jnp.max(logits, axis=0, keepdims=True))
    o_ref[0] = ex / jnp.sum(ex, axis=0, keepdims=True)


def _router(xm, w_rt):
    bsz, t, d = xm.shape
    e = w_rt.shape[0]
    tr = ROW_TILE
    return pl.pallas_call(
        _router_kernel,
        grid=(bsz, t // tr),
        in_specs=[pl.BlockSpec((1, tr, d), lambda b, i: (b, i, 0)),
                  pl.BlockSpec((e, d), lambda b, i: (0, 0))],
        out_specs=pl.BlockSpec((1, e, tr), lambda b, i: (b, 0, i)),
        out_shape=_sds((bsz, e, t), F32),
        compiler_params=_cparams("parallel", "parallel"),
        name="router",
    )(xm, w_rt)


def _prefix_excl(src_ref, dst_ref, upper):
    e, n = src_ref.shape
    off = jnp.zeros((e, 1), F32)
    for kb in range(n // LANE):
        blk = src_ref[:, kb * LANE:(kb + 1) * LANE]
        inc = _dot(blk.astype(BF16), upper)
        dst_ref[:, kb * LANE:(kb + 1) * LANE] = inc - blk + off
        off = off + inc[:, LANE - 1:LANE]


def _topk_kernel(a_ref, slot_ref, slot_t_ref, gv_t_ref, m_s, r_s, *, cap):
    aff = a_ref[0]
    e, n = aff.shape
    bits = pltpu.bitcast(aff, jnp.int32)
    thr = jnp.zeros((e, 1), jnp.int32)
    for bit in range(30, -1, -1):
        cand = thr | (1 << bit)
        cnt = jnp.sum((bits >= cand).astype(jnp.int32), axis=1, keepdims=True)
        thr = jnp.where(cnt >= cap, cand, thr)
    gt = (bits > thr).astype(F32)
    eq = (bits == thr).astype(F32)
    need = cap - jnp.sum(gt, axis=1, keepdims=True)
    ii = lax.broadcasted_iota(jnp.int32, (LANE, LANE), 0)
    jj = lax.broadcasted_iota(jnp.int32, (LANE, LANE), 1)
    upper = (ii <= jj).astype(BF16)
    m_s[...] = eq
    _prefix_excl(m_s, r_s, upper)
    sel = gt + eq * (r_s[...] < need).astype(F32)
    m_s[...] = sel
    _prefix_excl(m_s, r_s, upper)
    slot = jnp.where(sel > 0.0, r_s[...], -1.0)
    slot_ref[0] = slot
    eye = (lax.broadcasted_iota(jnp.int32, (e, e), 0) == lax.broadcasted_iota(jnp.int32, (e, e), 1)).astype(F32)
    slot_t_ref[0] = _dot(slot, eye, TN, precision=HI)
    gv_t_ref[0] = _dot(aff, eye, TN, precision=HI)


def _topk(aff_t, blk, ntok, cap):
    bsz, e, _ = aff_t.shape
    return pl.pallas_call(
        functools.partial(_topk_kernel, cap=cap),
        grid=(bsz,),
        in_specs=[pl.BlockSpec((1, e, ntok), lambda b: (b, 0, blk))],
        out_specs=[pl.BlockSpec((1, e, ntok), lambda b: (b, 0, 0)),
                   pl.BlockSpec((1, ntok, e), lambda b: (b, 0, 0)),
                   pl.BlockSpec((1, ntok, e), lambda b: (b, 0, 0))],
        out_shape=[_sds((bsz, e, ntok), F32), _sds((bsz, ntok, e), F32), _sds((bsz, ntok, e), F32)],
        scratch_shapes=[pltpu.VMEM((e, ntok), F32), pltpu.VMEM((e, ntok), F32)],
        compiler_params=_cparams("parallel"),
        name="topk",
    )(aff_t)


def _expert_up_kernel(slot_ref, x_ref, wg_ref, wu_ref, o_ref, p_s, g_s, u_s):
    k = pl.program_id(2)
    sps, capp, n = p_s.shape

    @pl.when(k == 0)
    def _():
        sidx = lax.broadcasted_iota(jnp.int32, (capp, n), 0).astype(F32)
        for i in range(sps):
            srow = slot_ref[i, pl.ds(pl.program_id(1), 1), :]
            p_s[i] = jnp.where(srow == sidx, 1.0, 0.0).astype(BF16)
        g_s[...] = jnp.zeros(g_s.shape, F32)
        u_s[...] = jnp.zeros(u_s.shape, F32)

    xg = jnp.concatenate([_dot(p_s[i], x_ref[i]).astype(BF16) for i in range(sps)], axis=0)
    g_s[...] += _dot(xg, wg_ref[0].astype(BF16))
    u_s[...] += _dot(xg, wu_ref[0].astype(BF16))

    @pl.when(k == pl.num_programs(2) - 1)
    def _():
        hid = (jax.nn.silu(g_s[...]) * u_s[...]).astype(o_ref.dtype)
        for i in range(sps):
            o_ref[i, 0] = hid[i * capp:(i + 1) * capp]


def _expert_up(slot, xm, blk, ntok, capp, sps, w_gate, w_up):
    bsz, e = slot.shape[:2]
    d, ff = w_gate.shape[1:]
    tk = 512
    return pl.pallas_call(
        _expert_up_kernel,
        grid=(bsz // sps, e, d // tk),
        in_specs=[pl.BlockSpec((sps, e, ntok), lambda b, ei, k: (b, 0, 0)),
                  pl.BlockSpec((sps, ntok, tk), lambda b, ei, k: (b, blk, k)),
                  pl.BlockSpec((1, tk, ff), lambda b, ei, k: (ei, k, 0)),
                  pl.BlockSpec((1, tk, ff), lambda b, ei, k: (ei, k, 0))],
        out_specs=pl.BlockSpec((sps, 1, capp, ff), lambda b, ei, k: (b, ei, 0, 0)),
        out_shape=_sds((bsz, e, capp, ff), BF16),
        scratch_shapes=[pltpu.VMEM((sps, capp, ntok), BF16), pltpu.VMEM((sps * capp, ff), F32),
                        pltpu.VMEM((sps * capp, ff), F32)],
        compiler_params=_cparams("parallel", "parallel", "arbitrary"),
        name="expert_up",
    )(slot, xm, w_gate, w_up)


def _expert_down_kernel(hid_ref, wd_ref, o_ref, w_s):
    @pl.when(pl.program_id(2) == 0)
    def _():
        w_s[...] = wd_ref[0].astype(BF16)

    o_ref[0, 0] = _dot(hid_ref[0, 0], w_s[...]).astype(o_ref.dtype)


def _expert_down(hid, w_down):
    bsz, e, capp, ff = hid.shape
    d = w_down.shape[-1]
    tn = _pick_tile(d, 2048)
    return pl.pallas_call(
        _expert_down_kernel,
        grid=(e, d // tn, bsz),
        in_specs=[pl.BlockSpec((1, 1, capp, ff), lambda ei, j, b: (b, ei, 0, 0)),
                  pl.BlockSpec((1, ff, tn), lambda ei, j, b: (ei, 0, j))],
        out_specs=pl.BlockSpec((1, 1, capp, tn), lambda ei, j, b: (b, ei, 0, j)),
        out_shape=_sds((bsz, e, capp, d), BF16),
        scratch_shapes=[pltpu.VMEM((ff, tn), BF16)],
        compiler_params=_cparams("parallel", "parallel", "arbitrary"),
        name="expert_down",
    )(hid, w_down)


COMBINE_TN = 512
COMBINE_VMEM_LIMIT = 56 << 20


def _combine_kernel(*refs):
    ye_ref, slot_t_ref, gv_t_ref = refs[:3]
    o_ref = refs[-1]
    ei = pl.program_id(2)
    n, e = slot_t_ref.shape[1:]
    capp, d = ye_ref.shape[2:]

    @pl.when(ei == 0)
    def _():
        o_ref[...] = jnp.zeros(o_ref.shape, F32)

    pick = (lax.broadcasted_iota(jnp.int32, (e, LANE), 0) == ei).astype(F32)
    slot_b = _dot(slot_t_ref[0], pick, precision=HI)
    gv_b = _dot(gv_t_ref[0], pick, precision=HI)
    lane = lax.broadcasted_iota(jnp.int32, (1, LANE), 1).astype(F32)
    pt = jnp.concatenate([jnp.where(slot_b == lane + float(c * LANE), 1.0, 0.0).astype(BF16) for c in range(capp // LANE)],
                         axis=1)
    tn = COMBINE_TN
    gv = jnp.concatenate([gv_b] * (tn // LANE), axis=1)
    for j in range(d // tn):
        o_ref[0, :, j * tn:(j + 1) * tn] += _dot(pt, ye_ref[0, 0, :, j * tn:(j + 1) * tn]) * gv


def _combine(ye, slot_t, gv_t, tq, blk0, t, prev=None):
    bsz, e, capp, d = ye.shape
    ntok = slot_t.shape[1]
    in_specs = [pl.BlockSpec((1, 1, capp, d), lambda b, q, ei: (b, ei, 0, 0)),
                pl.BlockSpec((1, tq, e), lambda b, q, ei: (b, q, 0)),
                pl.BlockSpec((1, tq, e), lambda b, q, ei: (b, q, 0))]
    args = [ye, slot_t, gv_t]
    aliases = {}
    if prev is not None:
        in_specs.append(pl.BlockSpec(memory_space=pl.ANY))
        args.append(prev)
        aliases = {3: 0}
    return pl.pallas_call(
        _combine_kernel,
        grid=(bsz, ntok // tq, e),
        in_specs=in_specs,
        out_specs=pl.BlockSpec((1, tq, d), lambda b, q, ei: (b, blk0 + q, 0)),
        out_shape=_sds((bsz, t, d), F32),
        input_output_aliases=aliases,
        compiler_params=pltpu.CompilerParams(dimension_semantics=("parallel", "parallel", "arbitrary"),
                                             vmem_limit_bytes=COMBINE_VMEM_LIMIT),
        name="combine",
    )(*args)


def _round_up(x, m):
    return (x + m - 1) // m * m


def _moe_part(aff_t, xm, blk, ntok, sps, tq, t_out, we, prev=None):
    e = aff_t.shape[1]
    cap = EC_CAPACITY * ntok // e
    capp = _round_up(cap, LANE)
    slot, slot_t, gv_t = _topk(aff_t, blk, ntok, cap)
    hid = _expert_up(slot, xm, blk, ntok, capp, sps, we["gate"], we["up"])
    ye = _expert_down(hid, we["down"])
    return _combine(ye, slot_t, gv_t, tq, blk * ntok // tq, t_out, prev)


def kernel(x, c, ctx, c_ctx, w_mod, b_mod, w_in, b_in, conv_a_w, conv_a_b, lru_wa, lru_ba, lru_wx, lru_bx, lru_lam, gla_wa2, gla_ba, gla_norm_g, conv_c_w, conv_c_b, mlstm_norm_g, w_branch, w_out, ln1_g, ln1_b, w_router, w_e_gate, w_e_up, w_e_down, ln2_g, ln2_b):
    bsz, n_lat, d = x.shape
    nc = ctx.shape[1]
    t = n_lat + nc
    depth = w_mod.shape[0]
    d_rnn = conv_a_w.shape[-1]
    kg, vg = gla_ba.shape[-1], gla_norm_g.shape[-1]
    rank = gla_wa2.shape[2]
    wm = mlstm_norm_g.shape[-1]
    n_gate = 4 * MLSTM_HEADS
    n_exp = w_router.shape[-1]
    alpha = (2 * depth) ** 0.25
    assert 2 * rank + n_gate <= LANE and t % ROW_TILE == 0 and n_lat % ROW_TILE == 0 and bsz < SUBLANE

    sizes = (d_rnn, d_rnn, kg, kg, vg, vg, 2 * rank, wm, wm, wm, wm, n_gate)
    offs = [0]
    for sz in sizes:
        offs.append(offs[-1] + sz)
    n_feat = offs[-1]
    take_main = lambda a: jnp.concatenate([a[..., offs[0]:offs[6]], a[..., offs[7]:offs[11]]], axis=-1)
    take_small = lambda a: jnp.concatenate([a[..., offs[6]:offs[7]], a[..., offs[11]:offs[12]]], axis=-1)
    n_main = (offs[6] - offs[0]) + (offs[11] - offs[7])
    pad_s = LANE - (2 * rank + n_gate)
    mo = {"a_x": 0, "a_g": d_rnn, "b_q": 2 * d_rnn, "b_k": 2 * d_rnn + kg, "b_v": 2 * d_rnn + 2 * kg,
          "b_r": 2 * d_rnn + 2 * kg + vg}
    mo["c_q"] = mo["b_r"] + vg
    mo["c_k"], mo["c_v"], mo["c_o"] = mo["c_q"] + wm, mo["c_q"] + 2 * wm, mo["c_q"] + 3 * wm

    cvec = jnp.concatenate([c, c_ctx[None], jnp.zeros((SUBLANE - bsz - 1, d), F32)], axis=0)
    mod = _mod_all(cvec, w_mod, b_mod)
    xs = jnp.concatenate([x, ctx], axis=1)

    for l in range(depth):
        last = l == depth - 1
        mod6 = mod[l].reshape(SUBLANE, 6, d)
        w_l = w_in[l]
        w_main = take_main(w_l).astype(BF16)
        b_main = take_main(b_in[l]).reshape(1, n_main)
        w_small = jnp.pad(take_small(w_l), ((0, 0), (0, pad_s))).astype(BF16)
        b_small = jnp.pad(take_small(b_in[l]), (0, pad_s)).reshape(1, LANE)
        tn = MERGE_TN
        wg3 = w_l[:, n_feat:].reshape(d, 3, d // tn, tn).transpose(0, 2, 1, 3).reshape(d, 3 * d).astype(BF16)
        bg3 = b_in[l][n_feat:].reshape(3, d // tn, tn).transpose(1, 0, 2).reshape(1, 3 * d)
        lru_p = {"conv_w": conv_a_w[l], "conv_b": conv_a_b[l].reshape(1, d_rnn),
                 "wa": lru_wa[l].astype(BF16), "ba": lru_ba[l].reshape(2, 1, d_rnn),
                 "wx": lru_wx[l].astype(BF16), "bx": lru_bx[l].reshape(2, 1, d_rnn),
                 "lam": lru_lam[l].reshape(2, 1, d_rnn)}
        wlr = jnp.zeros((2, LANE, kg), F32)
        for dd in range(2):
            wlr = wlr.at[dd, dd * rank:(dd + 1) * rank].set(gla_wa2[l, dd])
        gla_p = {"wlr": wlr, "ba": gla_ba[l].reshape(2, 1, kg), "norm_g": gla_norm_g[l].reshape(1, vg)}
        ml_p = {"conv_w": conv_c_w[l], "conv_b": conv_c_b[l].reshape(1, 2 * wm), "norm_g": mlstm_norm_g[l].reshape(1, wm)}

        h = _modulate(xs, mod6, 0, n_lat).reshape(bsz * t, d)
        feat = _matmul(h, w_main, b_main, F32, 512, 1024, "feat").reshape(bsz, t, n_main)
        feat_s = _matmul(h, w_small, b_small, F32, 1024, LANE, "feat_small").reshape(bsz, t, LANE)
        y0 = _rglru(feat, lru_p, d_rnn, n_lat)
        y1 = _gla(feat, feat_s, gla_p, (mo["b_q"], mo["b_k"], mo["b_v"], mo["b_r"]), n_lat)
        y2 = _mlstm(feat, feat_s, ml_p, (mo["c_q"], mo["c_k"], mo["c_v"], mo["c_o"]), 2 * rank, n_lat)
        ys = [y.reshape(bsz * t, -1) for y in (y0, y1, y2)]
        merged = _merge(h, wg3, bg3, ys, w_branch[l].astype(BF16))
        y = _matmul(merged, w_out[l], jnp.zeros((1, d), F32), F32, 512, 512, "out_proj").reshape(bsz, t, d)
        xs = _res_ln(xs, y, mod6, 2, ln1_g[l], ln1_b[l], alpha, n_lat, t)

        xm = _modulate(xs, mod6, 3, n_lat)
        aff_t = _router(xm, w_router[l].T.astype(BF16))
        we = {"gate": w_e_gate[l], "up": w_e_up[l], "down": w_e_down[l]}
        t_out = n_lat if last else t
        f = _moe_part(aff_t, xm, 0, n_lat, 1, _pick_tile(n_lat, 1024), t_out, we)
        if not last:
            f = _moe_part(aff_t, xm, n_lat // nc, nc, bsz, nc, t_out, we, prev=f)
        xs = _res_ln(xs, f, mod6, 5, ln2_g[l], ln2_b[l], alpha, n_lat, t_out)
    return xs
```

```python
import functools

import jax
import jax.numpy as jnp
from jax import lax
from jax.experimental import pallas as pl
from jax.experimental.pallas import tpu as pltpu

F32 = jnp.float32
BF16 = jnp.bfloat16
HI = lax.Precision.HIGHEST

GRID_W = 64
CHUNK = 64
LRU_C = 8.0
GLA_HEADS = 4
GLA_TAU = 16.0
MLSTM_HEADS = 4
EC_CAPACITY = 2
LN_EPS = 1e-5

LANE = 128
SUBLANE = 8
ROW_TILE = 256
VMEM_LIMIT = 48 << 20

NT = (((1,), (1,)), ((), ()))
TN = (((0,), (0,)), ((), ()))


def _cparams(*sem):
    return pltpu.CompilerParams(dimension_semantics=sem, vmem_limit_bytes=VMEM_LIMIT)


def _sds(shape, dtype):
    return jax.ShapeDtypeStruct(shape, dtype)


def _dot(a, b, dims=None, precision=None):
    if dims is None:
        return jnp.dot(a, b, preferred_element_type=F32, precision=precision)
    return lax.dot_general(a, b, dims, preferred_element_type=F32, precision=precision)


def _mod_kernel(c_ref, w_ref, b_ref, o_ref):
    a = jax.nn.silu(c_ref[...]).astype(BF16)
    o_ref[0] = _dot(a, w_ref[0].astype(BF16)) + b_ref[0]


def _mod_all(cvec, w_mod, b_mod):
    depth, d, n6 = w_mod.shape
    tn = 512
    return pl.pallas_call(
        _mod_kernel,
        grid=(depth, n6 // tn),
        in_specs=[pl.BlockSpec((SUBLANE, d), lambda l, j: (0, 0)),
                  pl.BlockSpec((1, d, tn), lambda l, j: (l, 0, j)),
                  pl.BlockSpec((1, 1, tn), lambda l, j: (l, 0, j))],
        out_specs=pl.BlockSpec((1, SUBLANE, tn), lambda l, j: (l, 0, j)),
        out_shape=_sds((depth, SUBLANE, n6), F32),
        compiler_params=_cparams("parallel", "parallel"),
        name="mod",
    )(cvec, w_mod, b_mod.reshape(depth, 1, n6))


def _modulate_kernel(x_ref, m_ref, o_ref, *, off):
    sh = m_ref[0, off:off + 1, :]
    sc = m_ref[0, off + 1:off + 2, :]
    o_ref[0] = (x_ref[0] * (1.0 + sc) + sh).astype(o_ref.dtype)


def _mod_row_map(n_lat_tiles, batch):
    return lambda b, i: (jnp.where(i < n_lat_tiles, b, batch), 0, 0)


def _modulate(xs, mod6, off, n_lat):
    batch, t, d = xs.shape
    tr = ROW_TILE
    return pl.pallas_call(
        functools.partial(_modulate_kernel, off=off),
        grid=(batch, t // tr),
        in_specs=[pl.BlockSpec((1, tr, d), lambda b, i: (b, i, 0)),
                  pl.BlockSpec((1, 6, d), _mod_row_map(n_lat // tr, batch))],
        out_specs=pl.BlockSpec((1, tr, d), lambda b, i: (b, i, 0)),
        out_shape=_sds((batch, t, d), BF16),
        compiler_params=_cparams("parallel", "parallel"),
        name="modulate",
    )(xs, mod6)


def _mm_kernel(a_ref, w_ref, b_ref, o_ref):
    o_ref[...] = (_dot(a_ref[...], w_ref[...]) + b_ref[...]).astype(o_ref.dtype)


def _pick_tile(n, pref):
    while pref > LANE and n % pref:
        pref //= 2
    assert n % pref == 0
    return pref


def _mm_castw_kernel(a_ref, w_ref, b_ref, o_ref, w_s):
    @pl.when(pl.program_id(1) == 0)
    def _():
        w_s[...] = w_ref[...].astype(BF16)

    o_ref[...] = (_dot(a_ref[...], w_s[...]) + b_ref[...]).astype(o_ref.dtype)


def _matmul(a, w, layer, bias, out_dtype, tm, tn, name):
    m, k = a.shape
    n = w.shape[2]
    tm, tn = _pick_tile(m, tm), _pick_tile(n, tn)
    cast = w.dtype == F32
    return pl.pallas_call(
        _mm_castw_kernel if cast else _mm_kernel,
        grid=(n // tn, m // tm),
        in_specs=[pl.BlockSpec((tm, k), lambda j, i: (i, 0)),
                  pl.BlockSpec((None, k, tn), lambda j, i: (layer, 0, j)),
                  pl.BlockSpec((1, tn), lambda j, i: (0, j))],
        out_specs=pl.BlockSpec((tm, tn), lambda j, i: (i, j)),
        out_shape=_sds((m, n), out_dtype),
        scratch_shapes=[pltpu.VMEM((k, tn), BF16)] if cast else [],
        compiler_params=_cparams("parallel", "arbitrary" if cast else "parallel"),
        name=name,
    )(a, w, bias)


def _wprep_kernel(a_ref, b_ref, o_ref, *, regions):
    j = pl.program_id(2)
    tc = o_ref.shape[-1]
    for lo, hi, shift in regions:
        @pl.when(jnp.logical_and(j >= lo, j < hi))
        def _():
            if shift == 0:
                rows = a_ref[...]
            else:
                rows = jnp.concatenate([a_ref[...], b_ref[:LANE]], axis=0)[shift:shift + tc]
            o_ref[...] = rows.T.astype(BF16)


def _wprep(w_t, tc, base_blk, n_blk, regions, name):
    depth, _, k = w_t.shape
    assert all(0 <= s < LANE and s % SUBLANE == 0 for _, _, s in regions)
    tr = _pick_tile(k, 1024)
    return pl.pallas_call(
        functools.partial(_wprep_kernel, regions=regions),
        grid=(depth, k // tr, n_blk),
        in_specs=[pl.BlockSpec((None, tc, tr), lambda l, i, j: (l, base_blk + j, i)),
                  pl.BlockSpec((None, tc, tr), lambda l, i, j: (l, base_blk + j + 1, i))],
        out_specs=pl.BlockSpec((None, tr, tc), lambda l, i, j: (l, i, j)),
        out_shape=_sds((depth, k, n_blk * tc), BF16),
        compiler_params=_cparams("parallel", "parallel", "parallel"),
        name=name,
    )(w_t, w_t)


def _res_ln_kernel(*refs, off, alpha, next_off):
    if next_off is None:
        x_ref, y_ref, m_ref, g_ref, b_ref, o_ref = refs
    else:
        x_ref, y_ref, m_ref, g_ref, b_ref, mn_ref, o_ref, on_ref = refs
    gate = m_ref[0, off:off + 1, :]
    z = alpha * x_ref[0] + gate * y_ref[0]
    mu = jnp.mean(z, axis=-1, keepdims=True)
    zc = z - mu
    var = jnp.mean(zc * zc, axis=-1, keepdims=True)
    out = zc * lax.rsqrt(var + LN_EPS) * g_ref[...] + b_ref[...]
    o_ref[0] = out
    if next_off is not None:
        sh = mn_ref[0, next_off:next_off + 1, :]
        sc = mn_ref[0, next_off + 1:next_off + 2, :]
        on_ref[0] = (out * (1.0 + sc) + sh).astype(on_ref.dtype)


def _res_ln(xs, y, mod6, off, ln_g, ln_b, alpha, n_lat, t_out, next_mod6=None, next_off=None):
    batch, t, d = xs.shape
    tr = ROW_TILE
    row = pl.BlockSpec((1, tr, d), lambda b, i: (b, i, 0))
    modspec = pl.BlockSpec((1, 6, d), _mod_row_map(n_lat // tr, batch))
    vec = pl.BlockSpec((1, d), lambda b, i: (0, 0))
    in_specs = [row, row, modspec, vec, vec]
    args = [xs, y, mod6, ln_g.reshape(1, d), ln_b.reshape(1, d)]
    out_specs, out_shape = row, _sds((batch, t_out, d), F32)
    if next_off is not None:
        in_specs.append(modspec)
        args.append(next_mod6)
        out_specs, out_shape = [row, row], [out_shape, _sds((batch, t_out, d), BF16)]
    return pl.pallas_call(
        functools.partial(_res_ln_kernel, off=off, alpha=alpha, next_off=next_off),
        grid=(batch, t_out // tr),
        in_specs=in_specs,
        out_specs=out_specs,
        out_shape=out_shape,
        compiler_params=_cparams("parallel", "parallel"),
        name="res_ln",
    )(*args)


def _gelu_tanh(x):
    return jax.nn.gelu(x, approximate=True)


def _lru_gates(u2, wa_ref, ba_ref, wx_ref, bx_ref, lam_ref):
    ub = u2.astype(BF16)
    r = jax.nn.sigmoid(_dot(ub, wa_ref[0]) + ba_ref[...])
    i = jax.nn.sigmoid(_dot(ub, wx_ref[0]) + bx_ref[...])
    log_a = (-LRU_C * jax.nn.softplus(-lam_ref[...])) * r
    a = jnp.exp(log_a)
    bt = jnp.sqrt(1.0 - jnp.exp(2.0 * log_a)) * (i * u2)
    return a, bt


def _lru_lat_kernel(*refs, rev, n_cg):
    if rev:
        (x_ref, pv_ref, nx_ref, cw_ref, cb_ref, wa_ref, ba_ref, wx_ref, bx_ref, lam_ref, e0_ref,
         ag_ref, hf_ref, _alias, o_ref, a_s, b_s, carry) = refs
    else:
        (x_ref, pv_ref, nx_ref, cw_ref, cb_ref, wa_ref, ba_ref, wx_ref, bx_ref, lam_ref, e0_ref,
         _alias, o_ref, a_s, b_s, carry) = refs
    s = pl.program_id(1)
    cg = (n_cg - 1 - s) if rev else s
    bsz, rows, ncol, cb = x_ref.shape

    @pl.when(s == 0)
    def _():
        carry[...] = e0_ref[...]

    x = x_ref[...]
    col = lax.broadcasted_iota(jnp.int32, (1, ncol, 1), 1)
    not_first = (cg > 0).astype(F32)
    not_last = (cg < n_cg - 1).astype(F32)
    top = jnp.where(col == 0, pltpu.roll(pv_ref[:, SUBLANE - 1], 1, 1) * not_first, pltpu.roll(x[:, rows - 1], 1, 1))
    bot1 = jnp.where(col == ncol - 1, pltpu.roll(nx_ref[:, 0], ncol - 1, 1) * not_last, pltpu.roll(x[:, 0], ncol - 1, 1))
    bot2 = jnp.where(col == ncol - 1, pltpu.roll(nx_ref[:, 1], ncol - 1, 1) * not_last, pltpu.roll(x[:, 1], ncol - 1, 1))
    xe = jnp.concatenate([top[:, None], x, bot1[:, None], bot2[:, None]], axis=1)
    u = cb_ref[...].reshape(1, 1, 1, cb)
    for k in range(4):
        u = u + cw_ref[k:k + 1, :].reshape(1, 1, 1, cb) * xe[:, k:k + rows]
    a, bt = _lru_gates(u.reshape(bsz * rows * ncol, cb), wa_ref, ba_ref, wx_ref, bx_ref, lam_ref)
    a_s[...] = a.reshape(bsz, rows, ncol, cb)
    b_s[...] = bt.reshape(bsz, rows, ncol, cb)

    def body(t, hp):
        h, p = hp
        r = (rows - 1 - t) if rev else t
        a_t = a_s[:, r]
        h = a_t * h + b_s[:, r]
        p = p * a_t
        b_s[:, r] = h
        a_s[:, r] = p
        return h, p

    h_end, p_end = lax.fori_loop(0, rows, body, (jnp.zeros((bsz, ncol, cb), F32), jnp.ones((bsz, ncol, cb), F32)), unroll=8)
    av, bv = p_end, h_end
    sh = 1
    while sh < ncol:
        if rev:
            valid = col < ncol - sh
            amt = ncol - sh
        else:
            valid = col >= sh
            amt = sh
        b_sh = jnp.where(valid, pltpu.roll(bv, amt, 1), 0.0)
        a_sh = jnp.where(valid, pltpu.roll(av, amt, 1), 1.0)
        bv = bv + av * b_sh
        av = av * a_sh
        sh *= 2
    e_prev = carry[...]
    e = bv + av * e_prev
    if rev:
        c_in = jnp.where(col == ncol - 1, e_prev, pltpu.roll(e, ncol - 1, 1))
        carry[...] = jnp.broadcast_to(e[:, 0:1], e.shape)
    else:
        c_in = jnp.where(col == 0, e_prev, pltpu.roll(e, 1, 1))
        carry[...] = jnp.broadcast_to(e[:, ncol - 1:ncol], e.shape)
    h = b_s[...] + a_s[...] * c_in[:, None]
    if rev:
        o_ref[...] = (hf_ref[...] + h) * _gelu_tanh(ag_ref[...])
    else:
        o_ref[...] = h


def _lru_ctx_kernel(*refs, rev):
    if rev:
        (x_ref, cw_ref, cb_ref, wa_ref, ba_ref, wx_ref, bx_ref, lam_ref, ag_ref, hf_ref, o_ref, e_ref) = refs
    else:
        (x_ref, cw_ref, cb_ref, wa_ref, ba_ref, wx_ref, bx_ref, lam_ref, o_ref, e_ref) = refs
    bsz, nc, cb = x_ref.shape
    x = x_ref[...]
    t = lax.broadcasted_iota(jnp.int32, (1, nc, 1), 1)
    xm1 = jnp.where(t >= 1, pltpu.roll(x, 1, 1), 0.0)
    xp1 = jnp.where(t < nc - 1, pltpu.roll(x, nc - 1, 1), 0.0)
    xp2 = jnp.where(t < nc - 2, pltpu.roll(x, nc - 2, 1), 0.0)
    w = [cw_ref[k:k + 1, :].reshape(1, 1, cb) for k in range(4)]
    u = w[0] * xm1 + w[1] * x + w[2] * xp1 + w[3] * xp2 + cb_ref[...].reshape(1, 1, cb)
    a, bt = _lru_gates(u.reshape(bsz * nc, cb), wa_ref, ba_ref, wx_ref, bx_ref, lam_ref)
    av = a.reshape(bsz, nc, cb)
    bv = bt.reshape(bsz, nc, cb)
    sh = 1
    while sh < nc:
        if rev:
            valid = t < nc - sh
            amt = nc - sh
        else:
            valid = t >= sh
            amt = sh
        b_sh = jnp.where(valid, pltpu.roll(bv, amt, 1), 0.0)
        a_sh = jnp.where(valid, pltpu.roll(av, amt, 1), 1.0)
        bv = bv + av * b_sh
        av = av * a_sh
        sh *= 2
    if rev:
        o_ref[...] = (hf_ref[...] + bv) * _gelu_tanh(ag_ref[...])
        e_ref[...] = jnp.broadcast_to(bv[:, 0:1], (bsz, SUBLANE, cb))
    else:
        o_ref[...] = bv
        e_ref[...] = jnp.broadcast_to(bv[:, nc - 1:nc], (bsz, SUBLANE, cb))


def _rglru(feat, p, d_rnn, n_lat):
    bsz, t, nf = feat.shape
    nc = t - n_lat
    nb, bs, _ = p["wa"][0].shape
    cb = bs
    assert d_rnn == nb * bs and cb % LANE == 0 and n_lat % nc == 0 and nc % GRID_W == 0
    rows = n_lat // GRID_W
    assert rows % SUBLANE == 0 and GRID_W % SUBLANE == 0
    n_cg = GRID_W // SUBLANE
    ag_off = d_rnn // cb
    feat4 = feat.reshape(bsz, t // GRID_W, GRID_W, nf)
    ctx_blk = n_lat // nc

    def wspecs(im):
        return [pl.BlockSpec((4, cb), im(lambda j: (0, j))),
                pl.BlockSpec((1, cb), im(lambda j: (0, j))),
                pl.BlockSpec((1, bs, bs), im(lambda j: (j, 0, 0))),
                pl.BlockSpec((1, cb), im(lambda j: (0, j))),
                pl.BlockSpec((1, bs, bs), im(lambda j: (j, 0, 0))),
                pl.BlockSpec((1, cb), im(lambda j: (0, j))),
                pl.BlockSpec((1, cb), im(lambda j: (0, j)))]

    def wargs(d):
        return [p["conv_w"], p["conv_b"], p["wa"][d], p["ba"][d], p["wx"][d], p["bx"][d], p["lam"][d]]

    im1 = lambda f: (lambda j: f(j))
    im2 = lambda f: (lambda j, s: f(j))
    hf = None
    out = None
    for rev in (False, True):
        in_specs = [pl.BlockSpec((bsz, nc, cb), lambda j: (0, ctx_blk, j))] + wspecs(im1)
        args = [feat] + wargs(int(rev))
        if rev:
            in_specs += [pl.BlockSpec((bsz, nc, cb), lambda j: (0, ctx_blk, ag_off + j)),
                         pl.BlockSpec((bsz, nc, cb), lambda j: (0, ctx_blk, j))]
            args += [feat, hf]
        part, e0 = pl.pallas_call(
            functools.partial(_lru_ctx_kernel, rev=rev),
            grid=(nb,),
            in_specs=in_specs,
            out_specs=[pl.BlockSpec((bsz, nc, cb), lambda j: (0, ctx_blk, j)),
                       pl.BlockSpec((bsz, SUBLANE, cb), lambda j: (0, 0, j))],
            out_shape=[_sds((bsz, t, d_rnn), F32), _sds((bsz, SUBLANE, d_rnn), F32)],
            compiler_params=_cparams("parallel"),
            name="lru_ctx_bwd" if rev else "lru_ctx_fwd",
        )(*args)
        cgm = (lambda s: n_cg - 1 - s) if rev else (lambda s: s)
        in_specs = [pl.BlockSpec((bsz, rows, SUBLANE, cb), lambda j, s: (0, 0, cgm(s), j)),
                    pl.BlockSpec((bsz, SUBLANE, SUBLANE, cb), lambda j, s: (0, rows // SUBLANE - 1, jnp.maximum(cgm(s) - 1, 0), j)),
                    pl.BlockSpec((bsz, SUBLANE, SUBLANE, cb), lambda j, s: (0, 0, jnp.minimum(cgm(s) + 1, n_cg - 1), j))]
        in_specs += wspecs(im2) + [pl.BlockSpec((bsz, SUBLANE, cb), lambda j, s: (0, 0, j))]
        args = [feat4, feat4, feat4] + wargs(int(rev)) + [e0]
        if rev:
            in_specs += [pl.BlockSpec((bsz, rows, SUBLANE, cb), lambda j, s: (0, 0, cgm(s), ag_off + j)),
                         pl.BlockSpec((bsz, rows, SUBLANE, cb), lambda j, s: (0, 0, cgm(s), j))]
            args += [feat4, hf.reshape(bsz, t // GRID_W, GRID_W, d_rnn)]
        in_specs += [pl.BlockSpec(memory_space=pl.ANY)]
        args += [part.reshape(bsz, t // GRID_W, GRID_W, d_rnn)]
        res = pl.pallas_call(
            functools.partial(_lru_lat_kernel, rev=rev, n_cg=n_cg),
            grid=(nb, n_cg),
            in_specs=in_specs,
            out_specs=pl.BlockSpec((bsz, rows, SUBLANE, cb), lambda j, s: (0, 0, cgm(s), j)),
            out_shape=_sds((bsz, t // GRID_W, GRID_W, d_rnn), F32),
            scratch_shapes=[pltpu.VMEM((bsz, rows, SUBLANE, cb), F32),
                            pltpu.VMEM((bsz, rows, SUBLANE, cb), F32),
                            pltpu.VMEM((bsz, SUBLANE, cb), F32)],
            input_output_aliases={len(args) - 1: 0},
            compiler_params=_cparams("parallel", "arbitrary"),
            name="lru_lat_bwd" if rev else "lru_lat_fwd",
        )(*args)
        res = res.reshape(bsz, t, d_rnn)
        if rev:
            out = res
        else:
            hf = res
    return out


def _chunk_of(s, n_ch, n_lat_ch, rev):
    if rev:
        return n_ch - 1 - s
    return jnp.where(s < n_ch - n_lat_ch, s + n_lat_ch, s - (n_ch - n_lat_ch))


def _head_norm(o, g):
    mu = jnp.mean(o, axis=-1, keepdims=True)
    oc = o - mu
    var = jnp.mean(oc * oc, axis=-1, keepdims=True)
    return oc * lax.rsqrt(var + LN_EPS) * g


def _gla_kernel(*refs, rev, heads):
    if rev:
        (q_ref, k_ref, v_ref, lr_ref, wlr_ref, ba_ref, tri_ref, rsel_ref, r_ref, of_ref, ng_ref, o_ref, st) = refs
    else:
        (q_ref, k_ref, v_ref, lr_ref, wlr_ref, ba_ref, tri_ref, rsel_ref, o_ref, st) = refs
    s = pl.program_id(1)

    @pl.when(s == 0)
    def _():
        st[...] = jnp.zeros(st.shape, F32)

    dk = q_ref.shape[-1] // heads
    dv = v_ref.shape[-1] // heads
    tri = tri_ref[...]
    g = jax.nn.log_sigmoid(_dot(lr_ref[0], wlr_ref[...], precision=HI) + ba_ref[...]) / GLA_TAU
    bcum = _dot(tri, g, precision=HI)
    bm = _dot(rsel_ref[...], g, precision=HI)
    q = q_ref[0] * dk ** -0.5
    k = k_ref[0]
    v = v_ref[0]
    for h in range(heads):
        sk = slice(h * dk, (h + 1) * dk)
        sv = slice(h * dv, (h + 1) * dv)
        qh, kh, bh = q[:, sk], k[:, sk], bcum[:, sk]
        vh = v[:, sv].astype(BF16)
        bmid, blast = bm[0:1, sk], bm[1:2, sk]
        att = _dot((qh * jnp.exp(bh - bmid)).astype(BF16), (kh * jnp.exp(bmid - bh)).astype(BF16), NT) * tri
        sth = st[h]
        o = _dot(att.astype(BF16), vh) + _dot((qh * jnp.exp(bh)).astype(BF16), sth.astype(BF16), NT)
        st[h] = jnp.exp(blast) * sth + _dot(vh, (kh * jnp.exp(blast - bh)).astype(BF16), TN)
        if rev:
            o = _head_norm(of_ref[0, :, sv] + o, ng_ref[:, sv]) * jax.nn.silu(r_ref[0, :, sv])
        o_ref[0, :, sv] = o.astype(o_ref.dtype)


def _scan_consts(rev):
    i = jnp.arange(CHUNK)
    tri = (i[None, :] >= i[:, None]) if rev else (i[None, :] <= i[:, None])
    tri = tri.astype(F32)
    mid = CHUNK // 2 if rev else CHUNK // 2 - 1
    last = 0 if rev else CHUNK - 1
    rsel = jnp.zeros((SUBLANE, CHUNK), F32).at[0].set(tri[mid]).at[1].set(tri[last])
    return tri, rsel


def _gla(feat, feat_s, p, offs, n_lat):
    bsz, t, _ = feat.shape
    kg, vg = p["ba"][0].shape[-1], p["norm_g"].shape[-1]
    oq, ok, ov, orr = offs
    assert oq % kg == 0 and ok % kg == 0 and ov % vg == 0 and orr % vg == 0
    n_ch, n_lat_ch = t // CHUNK, n_lat // CHUNK
    of = None
    for rev in (False, True):
        d = int(rev)
        tri, rsel = _scan_consts(rev)
        cm = lambda b, s: (b, _chunk_of(s, n_ch, n_lat_ch, rev))
        const = lambda b, s: (0, 0)
        in_specs = [pl.BlockSpec((1, CHUNK, kg), lambda b, s: cm(b, s) + (oq // kg,)),
                    pl.BlockSpec((1, CHUNK, kg), lambda b, s: cm(b, s) + (ok // kg,)),
                    pl.BlockSpec((1, CHUNK, vg), lambda b, s: cm(b, s) + (ov // vg,)),
                    pl.BlockSpec((1, CHUNK, LANE), lambda b, s: cm(b, s) + (0,)),
                    pl.BlockSpec((LANE, kg), const),
                    pl.BlockSpec((1, kg), const),
                    pl.BlockSpec((CHUNK, CHUNK), const),
                    pl.BlockSpec((SUBLANE, CHUNK), const)]
        args = [feat, feat, feat, feat_s, p["wlr"][d], p["ba"][d], tri, rsel]
        if rev:
            in_specs += [pl.BlockSpec((1, CHUNK, vg), lambda b, s: cm(b, s) + (orr // vg,)),
                         pl.BlockSpec((1, CHUNK, vg), lambda b, s: cm(b, s) + (0,)),
                         pl.BlockSpec((1, vg), const)]
            args += [feat, of, p["norm_g"]]
        res = pl.pallas_call(
            functools.partial(_gla_kernel, rev=rev, heads=GLA_HEADS),
            grid=(bsz, n_ch),
            in_specs=in_specs,
            out_specs=pl.BlockSpec((1, CHUNK, vg), lambda b, s: cm(b, s) + (0,)),
            out_shape=_sds((bsz, t, vg), BF16 if rev else F32),
            scratch_shapes=[pltpu.VMEM((GLA_HEADS, vg // GLA_HEADS, kg // GLA_HEADS), F32)],
            compiler_params=_cparams("parallel", "arbitrary"),
            name="gla_bwd" if rev else "gla_fwd",
        )(*args)
        of = res
    return of


def _conv_rows(x, prev_row, next_rows, w_ref, b_ref, lo, hi):
    n = x.shape[0]
    t = lax.broadcasted_iota(jnp.int32, (n, 1), 0)
    xm1 = jnp.where(t == 0, prev_row, pltpu.roll(x, 1, 0))
    xp1 = jnp.where(t == n - 1, next_rows[0:1], pltpu.roll(x, n - 1, 0))
    xp2 = jnp.where(t == n - 2, next_rows[0:1], jnp.where(t == n - 1, next_rows[1:2], pltpu.roll(x, n - 2, 0)))
    w = w_ref[:, lo:hi]
    return w[0:1] * xm1 + w[1:2] * x + w[2:3] * xp1 + w[3:4] * xp2 + b_ref[:, lo:hi]


def _mlstm_kernel(*refs, rev, heads, n_ch, n_lat_ch, g_off):
    if rev:
        (q_ref, qp_ref, qn_ref, k_ref, kp_ref, kn_ref, v_ref, cg_ref, cw_ref, cbias_ref, tri_ref, trit_ref,
         og_ref, hf_ref, ng_ref, o_ref, c_s, n_s, m_s) = refs
    else:
        (q_ref, qp_ref, qn_ref, k_ref, kp_ref, kn_ref, v_ref, cg_ref, cw_ref, cbias_ref, tri_ref, trit_ref,
         o_ref, c_s, n_s, m_s) = refs
    s = pl.program_id(1)

    @pl.when(s == 0)
    def _():
        c_s[...] = jnp.zeros(c_s.shape, F32)
        n_s[...] = jnp.zeros(n_s.shape, F32)
        m_s[...] = jnp.zeros(m_s.shape, F32)

    ch = _chunk_of(s, n_ch, n_lat_ch, rev)
    not_first = jnp.logical_and(ch != 0, ch != n_lat_ch).astype(F32)
    not_last = jnp.logical_and(ch != n_lat_ch - 1, ch != n_ch - 1).astype(F32)
    wm = q_ref.shape[-1]
    dh = wm // heads
    qc = jax.nn.silu(_conv_rows(q_ref[0], qp_ref[0, SUBLANE - 1:SUBLANE] * not_first, qn_ref[0, 0:2] * not_last,
                                cw_ref, cbias_ref, 0, wm))
    kc = jax.nn.silu(_conv_rows(k_ref[0], kp_ref[0, SUBLANE - 1:SUBLANE] * not_first, kn_ref[0, 0:2] * not_last,
                                cw_ref, cbias_ref, wm, 2 * wm)) * dh ** -0.5
    v = v_ref[0]
    tri = tri_ref[...]
    gts = cg_ref[0]
    gls = jax.nn.log_sigmoid(gts)
    bcol_all = _dot(tri, gls, precision=HI)
    ci0 = g_off + int(rev) * 2 * heads
    cf0 = ci0 + heads
    lane = lax.broadcasted_iota(jnp.int32, (SUBLANE, LANE), 1)
    row = lax.broadcasted_iota(jnp.int32, (SUBLANE, LANE), 0)
    sel_i = (lane == row + ci0).astype(F32)
    sel_f = (lane == row + cf0).astype(F32)
    ig_rows = _dot(sel_i, gts, NT, precision=HI)
    b_rows = _dot(_dot(sel_f, gls, NT, precision=HI), trit_ref[...], precision=HI)
    last = 0 if rev else CHUNK - 1
    for h in range(heads):
        sl = slice(h * dh, (h + 1) * dh)
        qh = qc[:, sl]
        qb, kb, vh = qh.astype(BF16), kc[:, sl].astype(BF16), v[:, sl]
        bc = bcol_all[:, cf0 + h:cf0 + h + 1]
        igc = gts[:, ci0 + h:ci0 + h + 1]
        m = m_s[h:h + 1, 0:1]
        dmat = jnp.where(tri > 0.0, bc - b_rows[h:h + 1, :] + ig_rows[h:h + 1, :], -jnp.inf)
        inter = bc + m
        m_row = jnp.maximum(jnp.max(dmat, axis=-1, keepdims=True), inter)
        pmat = _dot(qb, kb, NT) * jnp.exp(dmat - m_row)
        s_inter = jnp.exp(inter - m_row)
        cm = c_s[h]
        nv = n_s[h:h + 1, :]
        num = _dot(pmat.astype(BF16), vh.astype(BF16)) + s_inter * _dot(qb, cm.astype(BF16), NT)
        den = jnp.sum(pmat, axis=-1, keepdims=True) + s_inter * jnp.sum(qh * nv, axis=-1, keepdims=True)
        hout = num / jnp.maximum(jnp.abs(den), jnp.exp(-m_row))
        b_last = bc[last:last + 1]
        wl = b_last - bc + igc
        m_new = jnp.maximum(b_last + m, jnp.max(wl, axis=0, keepdims=True))
        sw = jnp.exp(wl - m_new)
        decay = jnp.exp(b_last + m - m_new)
        c_s[h] = decay * cm + _dot((sw * vh).astype(BF16), kb, TN)
        n_s[h:h + 1, :] = decay * nv + jnp.sum(sw * kc[:, sl], axis=0, keepdims=True)
        m_s[h:h + 1, :] = jnp.broadcast_to(m_new, (1, LANE))
        if rev:
            hout = _head_norm(hf_ref[0, :, sl] + hout, ng_ref[:, sl]) * jax.nn.sigmoid(og_ref[0, :, sl])
        o_ref[0, :, sl] = hout.astype(o_ref.dtype)


def _mlstm(feat, feat_s, p, offs, g_off, n_lat):
    bsz, t, _ = feat.shape
    wm = p["norm_g"].shape[-1]
    oq, ok, ov, oo = offs
    assert all(o % wm == 0 for o in offs)
    n_ch, n_lat_ch = t // CHUNK, n_lat // CHUNK
    n_r8 = t // SUBLANE
    per = CHUNK // SUBLANE
    hf = None
    for rev in (False, True):
        tri, _ = _scan_consts(rev)
        chm = lambda s: _chunk_of(s, n_ch, n_lat_ch, rev)
        const = lambda b, s: (0, 0)

        def cur(off):
            return pl.BlockSpec((1, CHUNK, wm), lambda b, s: (b, chm(s), off // wm))

        def prv(off):
            return pl.BlockSpec((1, SUBLANE, wm), lambda b, s: (b, jnp.maximum(chm(s) * per - 1, 0), off // wm))

        def nxt(off):
            return pl.BlockSpec((1, SUBLANE, wm), lambda b, s: (b, jnp.minimum((chm(s) + 1) * per, n_r8 - 1), off // wm))

        in_specs = [cur(oq), prv(oq), nxt(oq), cur(ok), prv(ok), nxt(ok), cur(ov),
                    pl.BlockSpec((1, CHUNK, LANE), lambda b, s: (b, chm(s), 0)),
                    pl.BlockSpec((4, 2 * wm), const),
                    pl.BlockSpec((1, 2 * wm), const),
                    pl.BlockSpec((CHUNK, CHUNK), const),
                    pl.BlockSpec((CHUNK, CHUNK), const)]
        args = [feat] * 7 + [feat_s, p["conv_w"], p["conv_b"], tri, tri.T]
        if rev:
            in_specs += [cur(oo), pl.BlockSpec((1, CHUNK, wm), lambda b, s: (b, chm(s), 0)), pl.BlockSpec((1, wm), const)]
            args += [feat, hf, p["norm_g"]]
        dh = wm // MLSTM_HEADS
        hf = pl.pallas_call(
            functools.partial(_mlstm_kernel, rev=rev, heads=MLSTM_HEADS, n_ch=n_ch, n_lat_ch=n_lat_ch, g_off=g_off),
            grid=(bsz, n_ch),
            in_specs=in_specs,
            out_specs=pl.BlockSpec((1, CHUNK, wm), lambda b, s: (b, chm(s), 0)),
            out_shape=_sds((bsz, t, wm), BF16 if rev else F32),
            scratch_shapes=[pltpu.VMEM((MLSTM_HEADS, dh, dh), F32),
                            pltpu.VMEM((SUBLANE, dh), F32),
                            pltpu.VMEM((SUBLANE, LANE), F32)],
            compiler_params=_cparams("parallel", "arbitrary"),
            name="mlstm_bwd" if rev else "mlstm_fwd",
        )(*args)
    return hf


def _merge_kernel(h_ref, wg0_ref, wg1_ref, wg2_ref, bg_ref, y0_ref, y1_ref, y2_ref, wb_ref, o_ref):
    h = h_ref[...]
    acc = None
    for n, (wg_ref, y_ref) in enumerate(((wg0_ref, y0_ref), (wg1_ref, y1_ref), (wg2_ref, y2_ref))):
        g = jax.nn.sigmoid(_dot(h, wg_ref[...]) + bg_ref[n])
        term = g * _dot(y_ref[...].astype(BF16), wb_ref[n])
        acc = term if acc is None else acc + term
    o_ref[...] = acc.astype(o_ref.dtype)


def _merge(h, w_gate, layer, bg, ys, wb):
    m, d = h.shape
    r = ys[0].shape[1]
    tm, tn = 256, _pick_tile(d, 512)
    nj = d // tn
    wspecs = [pl.BlockSpec((None, d, tn), lambda j, i, n=n: (layer, 0, n * nj + j)) for n in range(3)]
    yspec = pl.BlockSpec((tm, r), lambda j, i: (i, 0))
    return pl.pallas_call(
        _merge_kernel,
        grid=(nj, m // tm),
        in_specs=[pl.BlockSpec((tm, d), lambda j, i: (i, 0))] + wspecs +
                 [pl.BlockSpec((3, 1, tn), lambda j, i: (0, 0, j)),
                  yspec, yspec, yspec,
                  pl.BlockSpec((None, 3, r, tn), lambda j, i: (layer, 0, 0, j))],
        out_specs=pl.BlockSpec((tm, tn), lambda j, i: (i, j)),
        out_shape=_sds((m, d), BF16),
        compiler_params=_cparams("parallel", "parallel"),
        name="merge",
    )(h, w_gate, w_gate, w_gate, bg, *ys, wb)


def _router_kernel(x_ref, w_ref, o_ref):
    logits = _dot(w_ref[...], x_ref[0], NT)
    ex = jnp.exp(logits - jnp.max(logits, axis=0, keepdims=True))
    o_ref[0] = ex / jnp.sum(ex, axis=0, keepdims=True)


def _router(xm, w_rt):
    bsz, t, d = xm.shape
    e = w_rt.shape[0]
    tr = ROW_TILE
    return pl.pallas_call(
        _router_kernel,
        grid=(bsz, t // tr),
        in_specs=[pl.BlockSpec((1, tr, d), lambda b, i: (b, i, 0)),
                  pl.BlockSpec((e, d), lambda b, i: (0, 0))],
        out_specs=pl.BlockSpec((1, e, tr), lambda b, i: (b, 0, i)),
        out_shape=_sds((bsz, e, t), F32),
        compiler_params=_cparams("parallel", "parallel"),
        name="router",
    )(xm, w_rt)


def _prefix_excl(src_ref, dst_ref, upper):
    e, n = src_ref.shape
    off = jnp.zeros((e, 1), F32)
    for kb in range(n // LANE):
        blk = src_ref[:, kb * LANE:(kb + 1) * LANE]
        inc = _dot(blk.astype(BF16), upper)
        dst_ref[:, kb * LANE:(kb + 1) * LANE] = inc - blk + off
        off = off + inc[:, LANE - 1:LANE]


def _topk_kernel(a_ref, slot_ref, slot_t_ref, gv_t_ref, m_s, r_s, *, cap):
    aff = a_ref[0]
    e, n = aff.shape
    bits = pltpu.bitcast(aff, jnp.int32)
    thr = jnp.zeros((e, 1), jnp.int32)
    for bit in range(30, -1, -1):
        cand = thr | (1 << bit)
        cnt = jnp.sum((bits >= cand).astype(jnp.int32), axis=1, keepdims=True)
        thr = jnp.where(cnt >= cap, cand, thr)
    gt = (bits > thr).astype(F32)
    eq = (bits == thr).astype(F32)
    need = cap - jnp.sum(gt, axis=1, keepdims=True)
    ii = lax.broadcasted_iota(jnp.int32, (LANE, LANE), 0)
    jj = lax.broadcasted_iota(jnp.int32, (LANE, LANE), 1)
    upper = (ii <= jj).astype(BF16)
    m_s[...] = eq
    _prefix_excl(m_s, r_s, upper)
    sel = gt + eq * (r_s[...] < need).astype(F32)
    m_s[...] = sel
    _prefix_excl(m_s, r_s, upper)
    slot = jnp.where(sel > 0.0, r_s[...], -1.0)
    slot_ref[0] = slot
    eye = (lax.broadcasted_iota(jnp.int32, (e, e), 0) == lax.broadcasted_iota(jnp.int32, (e, e), 1)).astype(F32)
    slot_t_ref[0] = _dot(slot, eye, TN, precision=HI)
    gv_t_ref[0] = _dot(aff, eye, TN, precision=HI)


def _topk(aff_t, blk, ntok, cap):
    bsz, e, _ = aff_t.shape
    return pl.pallas_call(
        functools.partial(_topk_kernel, cap=cap),
        grid=(bsz,),
        in_specs=[pl.BlockSpec((1, e, ntok), lambda b: (b, 0, blk))],
        out_specs=[pl.BlockSpec((1, e, ntok), lambda b: (b, 0, 0)),
                   pl.BlockSpec((1, ntok, e), lambda b: (b, 0, 0)),
                   pl.BlockSpec((1, ntok, e), lambda b: (b, 0, 0))],
        out_shape=[_sds((bsz, e, ntok), F32), _sds((bsz, ntok, e), F32), _sds((bsz, ntok, e), F32)],
        scratch_shapes=[pltpu.VMEM((e, ntok), F32), pltpu.VMEM((e, ntok), F32)],
        compiler_params=_cparams("parallel"),
        name="topk",
    )(aff_t)


def _expert_up_kernel(slot_ref, x_ref, wg_ref, wu_ref, o_ref, p_s, g_s, u_s):
    k = pl.program_id(2)
    sps, capp, n = p_s.shape

    @pl.when(k == 0)
    def _():
        sidx = lax.broadcasted_iota(jnp.int32, (capp, n), 0).astype(F32)
        for i in range(sps):
            srow = slot_ref[i, pl.ds(pl.program_id(1), 1), :]
            p_s[i] = jnp.where(srow == sidx, 1.0, 0.0).astype(BF16)
        g_s[...] = jnp.zeros(g_s.shape, F32)
        u_s[...] = jnp.zeros(u_s.shape, F32)

    xg = jnp.concatenate([_dot(p_s[i], x_ref[i]).astype(BF16) for i in range(sps)], axis=0)
    g_s[...] += _dot(xg, wg_ref[0].astype(BF16))
    u_s[...] += _dot(xg, wu_ref[0].astype(BF16))

    @pl.when(k == pl.num_programs(2) - 1)
    def _():
        hid = (jax.nn.silu(g_s[...]) * u_s[...]).astype(o_ref.dtype)
        for i in range(sps):
            o_ref[i, 0] = hid[i * capp:(i + 1) * capp]


def _expert_up(slot, xm, blk, ntok, capp, sps, w_gate, w_up, layer):
    bsz, e = slot.shape[:2]
    d, ff = w_gate.shape[2:]
    tk = 512
    return pl.pallas_call(
        _expert_up_kernel,
        grid=(bsz // sps, e, d // tk),
        in_specs=[pl.BlockSpec((sps, e, ntok), lambda b, ei, k: (b, 0, 0)),
                  pl.BlockSpec((sps, ntok, tk), lambda b, ei, k: (b, blk, k)),
                  pl.BlockSpec((None, 1, tk, ff), lambda b, ei, k: (layer, ei, k, 0)),
                  pl.BlockSpec((None, 1, tk, ff), lambda b, ei, k: (layer, ei, k, 0))],
        out_specs=pl.BlockSpec((sps, 1, capp, ff), lambda b, ei, k: (b, ei, 0, 0)),
        out_shape=_sds((bsz, e, capp, ff), BF16),
        scratch_shapes=[pltpu.VMEM((sps, capp, ntok), BF16), pltpu.VMEM((sps * capp, ff), F32),
                        pltpu.VMEM((sps * capp, ff), F32)],
        compiler_params=_cparams("parallel", "parallel", "arbitrary"),
        name="expert_up",
    )(slot, xm, w_gate, w_up)


def _expert_down_kernel(hid_ref, wd_ref, o_ref, w_s):
    @pl.when(pl.program_id(2) == 0)
    def _():
        w_s[...] = wd_ref[0].astype(BF16)

    o_ref[0, 0] = _dot(hid_ref[0, 0], w_s[...]).astype(o_ref.dtype)


def _expert_down(hid, w_down, layer):
    bsz, e, capp, ff = hid.shape
    d = w_down.shape[-1]
    tn = _pick_tile(d, 2048)
    return pl.pallas_call(
        _expert_down_kernel,
        grid=(e, d // tn, bsz),
        in_specs=[pl.BlockSpec((1, 1, capp, ff), lambda ei, j, b: (b, ei, 0, 0)),
                  pl.BlockSpec((None, 1, ff, tn), lambda ei, j, b: (layer, ei, 0, j))],
        out_specs=pl.BlockSpec((1, 1, capp, tn), lambda ei, j, b: (b, ei, 0, j)),
        out_shape=_sds((bsz, e, capp, d), BF16),
        scratch_shapes=[pltpu.VMEM((ff, tn), BF16)],
        compiler_params=_cparams("parallel", "parallel", "arbitrary"),
        name="expert_down",
    )(hid, w_down)


COMBINE_TN = 512
COMBINE_VMEM_LIMIT = 56 << 20


def _combine_kernel(*refs):
    ye_ref, slot_t_ref, gv_t_ref = refs[:3]
    o_ref = refs[-1]
    ei = pl.program_id(2)
    n, e = slot_t_ref.shape[1:]
    capp, d = ye_ref.shape[2:]

    @pl.when(ei == 0)
    def _():
        o_ref[...] = jnp.zeros(o_ref.shape, F32)

    pick = (lax.broadcasted_iota(jnp.int32, (e, LANE), 0) == ei).astype(F32)
    slot_b = _dot(slot_t_ref[0], pick, precision=HI)
    gv_b = _dot(gv_t_ref[0], pick, precision=HI)
    lane = lax.broadcasted_iota(jnp.int32, (1, LANE), 1).astype(F32)
    pt = jnp.concatenate([jnp.where(slot_b == lane + float(c * LANE), 1.0, 0.0).astype(BF16) for c in range(capp // LANE)],
                         axis=1)
    tn = COMBINE_TN
    gv = jnp.concatenate([gv_b] * (tn // LANE), axis=1)
    for j in range(d // tn):
        o_ref[0, :, j * tn:(j + 1) * tn] += _dot(pt, ye_ref[0, 0, :, j * tn:(j + 1) * tn]) * gv


def _combine(ye, slot_t, gv_t, tq, blk0, t, prev=None):
    bsz, e, capp, d = ye.shape
    ntok = slot_t.shape[1]
    in_specs = [pl.BlockSpec((1, 1, capp, d), lambda b, q, ei: (b, ei, 0, 0)),
                pl.BlockSpec((1, tq, e), lambda b, q, ei: (b, q, 0)),
                pl.BlockSpec((1, tq, e), lambda b, q, ei: (b, q, 0))]
    args = [ye, slot_t, gv_t]
    aliases = {}
    if prev is not None:
        in_specs.append(pl.BlockSpec(memory_space=pl.ANY))
        args.append(prev)
        aliases = {3: 0}
    return pl.pallas_call(
        _combine_kernel,
        grid=(bsz, ntok // tq, e),
        in_specs=in_specs,
        out_specs=pl.BlockSpec((1, tq, d), lambda b, q, ei: (b, blk0 + q, 0)),
        out_shape=_sds((bsz, t, d), F32),
        input_output_aliases=aliases,
        compiler_params=pltpu.CompilerParams(dimension_semantics=("parallel", "parallel", "arbitrary"),
                                             vmem_limit_bytes=COMBINE_VMEM_LIMIT),
        name="combine",
    )(*args)


def _round_up(x, m):
    return (x + m - 1) // m * m


def _moe_part(aff_t, xm, blk, ntok, sps, tq, t_out, we, layer, prev=None):
    e = aff_t.shape[1]
    cap = EC_CAPACITY * ntok // e
    capp = _round_up(cap, LANE)
    slot, slot_t, gv_t = _topk(aff_t, blk, ntok, cap)
    hid = _expert_up(slot, xm, blk, ntok, capp, sps, we["gate"], we["up"], layer)
    ye = _expert_down(hid, we["down"], layer)
    return _combine(ye, slot_t, gv_t, tq, blk * ntok // tq, t_out, prev)


def kernel(x, c, ctx, c_ctx, w_mod, b_mod, w_in, b_in, conv_a_w, conv_a_b, lru_wa, lru_ba, lru_wx, lru_bx, lru_lam, gla_wa2, gla_ba, gla_norm_g, conv_c_w, conv_c_b, mlstm_norm_g, w_branch, w_out, ln1_g, ln1_b, w_router, w_e_gate, w_e_up, w_e_down, ln2_g, ln2_b):
    bsz, n_lat, d = x.shape
    nc = ctx.shape[1]
    t = n_lat + nc
    depth = w_mod.shape[0]
    d_rnn = conv_a_w.shape[-1]
    kg, vg = gla_ba.shape[-1], gla_norm_g.shape[-1]
    rank = gla_wa2.shape[2]
    wm = mlstm_norm_g.shape[-1]
    n_gate = 4 * MLSTM_HEADS
    n_exp = w_router.shape[-1]
    alpha = (2 * depth) ** 0.25
    assert 2 * rank + n_gate <= LANE and t % ROW_TILE == 0 and n_lat % ROW_TILE == 0 and bsz < SUBLANE

    sizes = (d_rnn, d_rnn, kg, kg, vg, vg, 2 * rank, wm, wm, wm, wm, n_gate)
    offs = [0]
    for sz in sizes:
        offs.append(offs[-1] + sz)
    n_feat = offs[-1]
    take_main = lambda a: jnp.concatenate([a[..., offs[0]:offs[6]], a[..., offs[7]:offs[11]]], axis=-1)
    take_small = lambda a: jnp.concatenate([a[..., offs[6]:offs[7]], a[..., offs[11]:offs[12]]], axis=-1)
    n_main = (offs[6] - offs[0]) + (offs[11] - offs[7])
    pad_s = LANE - (2 * rank + n_gate)
    mo = {"a_x": 0, "a_g": d_rnn, "b_q": 2 * d_rnn, "b_k": 2 * d_rnn + kg, "b_v": 2 * d_rnn + 2 * kg,
          "b_r": 2 * d_rnn + 2 * kg + vg}
    mo["c_q"] = mo["b_r"] + vg
    mo["c_k"], mo["c_v"], mo["c_o"] = mo["c_q"] + wm, mo["c_q"] + 2 * wm, mo["c_q"] + 3 * wm

    cvec = jnp.concatenate([c, c_ctx[None], jnp.zeros((SUBLANE - bsz - 1, d), F32)], axis=0)
    mod = _mod_all(cvec, w_mod, b_mod)
    xs = jnp.concatenate([x, ctx], axis=1)

    tc = LANE
    while tc < 512 and all(v % (2 * tc) == 0 for v in (offs[6], 4 * wm, d)):
        tc *= 2
    w_in_t = jnp.swapaxes(w_in, 1, 2)
    w_main = _wprep(w_in_t, tc, 0, n_main // tc, ((0, offs[6] // tc, 0), (offs[6] // tc, n_main // tc, offs[7] - offs[6])),
                    "wprep_main")
    w_gate = _wprep(w_in_t, tc, n_main // tc, 3 * d // tc, ((0, 3 * d // tc, n_feat - n_main),), "wprep_gate")
    w_small = jnp.pad(take_small(w_in), ((0, 0), (0, 0), (0, pad_s)))
    w_br = w_branch.astype(BF16)
    w_rt = jnp.swapaxes(w_router, 1, 2).astype(BF16)
    we = {"gate": w_e_gate, "up": w_e_up, "down": w_e_down}

    mod6s = [mod[l].reshape(SUBLANE, 6, d) for l in range(depth)]
    h = _modulate(xs, mod6s[0], 0, n_lat)
    for l in range(depth):
        last = l == depth - 1
        mod6 = mod6s[l]
        b_main = take_main(b_in[l]).reshape(1, n_main)
        b_small = jnp.pad(take_small(b_in[l]), (0, pad_s)).reshape(1, LANE)
        bg = b_in[l][n_feat:].reshape(3, 1, d)
        lru_p = {"conv_w": conv_a_w[l], "conv_b": conv_a_b[l].reshape(1, d_rnn),
                 "wa": lru_wa[l].astype(BF16), "ba": lru_ba[l].reshape(2, 1, d_rnn),
                 "wx": lru_wx[l].astype(BF16), "bx": lru_bx[l].reshape(2, 1, d_rnn),
                 "lam": lru_lam[l].reshape(2, 1, d_rnn)}
        wlr = jnp.zeros((2, LANE, kg), F32)
        for dd in range(2):
            wlr = wlr.at[dd, dd * rank:(dd + 1) * rank].set(gla_wa2[l, dd])
        gla_p = {"wlr": wlr, "ba": gla_ba[l].reshape(2, 1, kg), "norm_g": gla_norm_g[l].reshape(1, vg)}
        ml_p = {"conv_w": conv_c_w[l], "conv_b": conv_c_b[l].reshape(1, 2 * wm), "norm_g": mlstm_norm_g[l].reshape(1, wm)}

        h = h.reshape(bsz * t, d)
        feat = _matmul(h, w_main, l, b_main, F32, 512, 1024, "feat").reshape(bsz, t, n_main)
        feat_s = _matmul(h, w_small, l, b_small, F32, 1024, LANE, "feat_small").reshape(bsz, t, LANE)
        y0 = _rglru(feat, lru_p, d_rnn, n_lat)
        y1 = _gla(feat, feat_s, gla_p, (mo["b_q"], mo["b_k"], mo["b_v"], mo["b_r"]), n_lat)
        y2 = _mlstm(feat, feat_s, ml_p, (mo["c_q"], mo["c_k"], mo["c_v"], mo["c_o"]), 2 * rank, n_lat)
        ys = [y.reshape(bsz * t, -1) for y in (y0, y1, y2)]
        merged = _merge(h, w_gate, l, bg, ys, w_br)
        y = _matmul(merged, w_out, l, jnp.zeros((1, d), F32), F32, 512, 512, "out_proj").reshape(bsz, t, d)
        xs, xm = _res_ln(xs, y, mod6, 2, ln1_g[l], ln1_b[l], alpha, n_lat, t, mod6, 3)

        aff_t = _router(xm, w_rt[l])
        t_out = n_lat if last else t
        f = _moe_part(aff_t, xm, 0, n_lat, 1, _pick_tile(n_lat, 1024), t_out, we, l)
        if not last:
            f = _moe_part(aff_t, xm, n_lat // nc, nc, bsz, nc, t_out, we, l, prev=f)
        if last:
            xs = _res_ln(xs, f, mod6, 5, ln2_g[l], ln2_b[l], alpha, n_lat, t_out)
        else:
            xs, h = _res_ln(xs, f, mod6, 5, ln2_g[l], ln2_b[l], alpha, n_lat, t_out, mod6s[l + 1], 0)
    return xs
```

```python
import functools

import jax
import jax.numpy as jnp
from jax import lax
from jax.experimental import pallas as pl
from jax.experimental.pallas import tpu as pltpu

F32 = jnp.float32
BF16 = jnp.bfloat16
HI = lax.Precision.HIGHEST

GRID_W = 64
CHUNK = 64
LRU_C = 8.0
GLA_HEADS = 4
GLA_TAU = 16.0
MLSTM_HEADS = 4
EC_CAPACITY = 2
LN_EPS = 1e-5

LANE = 128
SUBLANE = 8
ROW_TILE = 256
VMEM_LIMIT = 48 << 20

NT = (((1,), (1,)), ((), ()))
TN = (((0,), (0,)), ((), ()))


def _cparams(*sem):
    return pltpu.CompilerParams(dimension_semantics=sem, vmem_limit_bytes=VMEM_LIMIT)


def _sds(shape, dtype):
    return jax.ShapeDtypeStruct(shape, dtype)


def _dot(a, b, dims=None, precision=None):
    if dims is None:
        return jnp.dot(a, b, preferred_element_type=F32, precision=precision)
    return lax.dot_general(a, b, dims, preferred_element_type=F32, precision=precision)


def _mod_kernel(c_ref, w_ref, b_ref, o_ref):
    a = jax.nn.silu(c_ref[...]).astype(BF16)
    o_ref[0] = _dot(a, w_ref[0].astype(BF16)) + b_ref[0]


def _mod_all(cvec, w_mod, b_mod):
    depth, d, n6 = w_mod.shape
    tn = 512
    return pl.pallas_call(
        _mod_kernel,
        grid=(depth, n6 // tn),
        in_specs=[pl.BlockSpec((SUBLANE, d), lambda l, j: (0, 0)),
                  pl.BlockSpec((1, d, tn), lambda l, j: (l, 0, j)),
                  pl.BlockSpec((1, 1, tn), lambda l, j: (l, 0, j))],
        out_specs=pl.BlockSpec((1, SUBLANE, tn), lambda l, j: (l, 0, j)),
        out_shape=_sds((depth, SUBLANE, n6), F32),
        compiler_params=_cparams("parallel", "parallel"),
        name="mod",
    )(cvec, w_mod, b_mod.reshape(depth, 1, n6))


def _modulate_kernel(x_ref, m_ref, o_ref, *, off):
    sh = m_ref[0, off:off + 1, :]
    sc = m_ref[0, off + 1:off + 2, :]
    o_ref[0] = (x_ref[0] * (1.0 + sc) + sh).astype(o_ref.dtype)


def _mod_row_map(n_lat_tiles, batch):
    return lambda b, i: (jnp.where(i < n_lat_tiles, b, batch), 0, 0)


def _modulate(xs, mod6, off, n_lat):
    batch, t, d = xs.shape
    tr = ROW_TILE
    return pl.pallas_call(
        functools.partial(_modulate_kernel, off=off),
        grid=(batch, t // tr),
        in_specs=[pl.BlockSpec((1, tr, d), lambda b, i: (b, i, 0)),
                  pl.BlockSpec((1, 6, d), _mod_row_map(n_lat // tr, batch))],
        out_specs=pl.BlockSpec((1, tr, d), lambda b, i: (b, i, 0)),
        out_shape=_sds((batch, t, d), BF16),
        compiler_params=_cparams("parallel", "parallel"),
        name="modulate",
    )(xs, mod6)


def _mm_kernel(a_ref, w_ref, b_ref, o_ref):
    o_ref[...] = (_dot(a_ref[...], w_ref[...]) + b_ref[...]).astype(o_ref.dtype)


def _pick_tile(n, pref):
    while pref > LANE and n % pref:
        pref //= 2
    assert n % pref == 0
    return pref


def _mm_castw_kernel(a_ref, w_ref, b_ref, o_ref, w_s):
    @pl.when(pl.program_id(1) == 0)
    def _():
        w_s[...] = w_ref[...].astype(BF16)

    o_ref[...] = (_dot(a_ref[...], w_s[...]) + b_ref[...]).astype(o_ref.dtype)


def _matmul(a, w, layer, bias, out_dtype, tm, tn, name):
    m, k = a.shape
    n = w.shape[2]
    tm, tn = _pick_tile(m, tm), _pick_tile(n, tn)
    cast = w.dtype == F32
    return pl.pallas_call(
        _mm_castw_kernel if cast else _mm_kernel,
        grid=(n // tn, m // tm),
        in_specs=[pl.BlockSpec((tm, k), lambda j, i: (i, 0)),
                  pl.BlockSpec((None, k, tn), lambda j, i: (layer, 0, j)),
                  pl.BlockSpec((1, tn), lambda j, i: (0, j))],
        out_specs=pl.BlockSpec((tm, tn), lambda j, i: (i, j)),
        out_shape=_sds((m, n), out_dtype),
        scratch_shapes=[pltpu.VMEM((k, tn), BF16)] if cast else [],
        compiler_params=_cparams("parallel", "arbitrary" if cast else "parallel"),
        name=name,
    )(a, w, bias)


def _wprep_kernel(a_ref, b_ref, o_ref, *, regions):
    j = pl.program_id(2)
    tc = o_ref.shape[-1]
    for lo, hi, shift in regions:
        @pl.when(jnp.logical_and(j >= lo, j < hi))
        def _():
            if shift == 0:
                rows = a_ref[...]
            else:
                rows = jnp.concatenate([a_ref[...], b_ref[:LANE]], axis=0)[shift:shift + tc]
            o_ref[...] = rows.T.astype(BF16)


def _wprep(w_t, tc, base_blk, n_blk, regions, name):
    depth, _, k = w_t.shape
    assert all(0 <= s < LANE and s % SUBLANE == 0 for _, _, s in regions)
    tr = _pick_tile(k, 1024)
    return pl.pallas_call(
        functools.partial(_wprep_kernel, regions=regions),
        grid=(depth, k // tr, n_blk),
        in_specs=[pl.BlockSpec((None, tc, tr), lambda l, i, j: (l, base_blk + j, i)),
                  pl.BlockSpec((None, tc, tr), lambda l, i, j: (l, base_blk + j + 1, i))],
        out_specs=pl.BlockSpec((None, tr, tc), lambda l, i, j: (l, i, j)),
        out_shape=_sds((depth, k, n_blk * tc), BF16),
        compiler_params=_cparams("parallel", "parallel", "parallel"),
        name=name,
    )(w_t, w_t)


def _res_ln_kernel(*refs, off, alpha, next_off):
    if next_off is None:
        x_ref, y_ref, m_ref, g_ref, b_ref, o_ref = refs
    else:
        x_ref, y_ref, m_ref, g_ref, b_ref, mn_ref, o_ref, on_ref = refs
    gate = m_ref[0, off:off + 1, :]
    z = alpha * x_ref[0] + gate * y_ref[0]
    mu = jnp.mean(z, axis=-1, keepdims=True)
    zc = z - mu
    var = jnp.mean(zc * zc, axis=-1, keepdims=True)
    out = zc * lax.rsqrt(var + LN_EPS) * g_ref[...] + b_ref[...]
    o_ref[0] = out
    if next_off is not None:
        sh = mn_ref[0, next_off:next_off + 1, :]
        sc = mn_ref[0, next_off + 1:next_off + 2, :]
        on_ref[0] = (out * (1.0 + sc) + sh).astype(on_ref.dtype)


def _res_ln(xs, y, mod6, off, ln_g, ln_b, alpha, n_lat, t_out, next_mod6=None, next_off=None):
    batch, t, d = xs.shape
    tr = ROW_TILE
    row = pl.BlockSpec((1, tr, d), lambda b, i: (b, i, 0))
    modspec = pl.BlockSpec((1, 6, d), _mod_row_map(n_lat // tr, batch))
    vec = pl.BlockSpec((1, d), lambda b, i: (0, 0))
    in_specs = [row, row, modspec, vec, vec]
    args = [xs, y, mod6, ln_g.reshape(1, d), ln_b.reshape(1, d)]
    out_specs, out_shape = row, _sds((batch, t_out, d), F32)
    if next_off is not None:
        in_specs.append(modspec)
        args.append(next_mod6)
        out_specs, out_shape = [row, row], [out_shape, _sds((batch, t_out, d), BF16)]
    return pl.pallas_call(
        functools.partial(_res_ln_kernel, off=off, alpha=alpha, next_off=next_off),
        grid=(batch, t_out // tr),
        in_specs=in_specs,
        out_specs=out_specs,
        out_shape=out_shape,
        compiler_params=_cparams("parallel", "parallel"),
        name="res_ln",
    )(*args)


def _gelu_tanh(x):
    return jax.nn.gelu(x, approximate=True)


def _lru_gates(u2, wa_ref, ba_ref, wx_ref, bx_ref, lam_ref):
    ub = u2.astype(BF16)
    r = jax.nn.sigmoid(_dot(ub, wa_ref[0]) + ba_ref[...])
    i = jax.nn.sigmoid(_dot(ub, wx_ref[0]) + bx_ref[...])
    log_a = (-LRU_C * jax.nn.softplus(-lam_ref[...])) * r
    a = jnp.exp(log_a)
    bt = jnp.sqrt(1.0 - jnp.exp(2.0 * log_a)) * (i * u2)
    return a, bt


def _lru_lat_kernel(*refs, rev, n_cg):
    if rev:
        (x_ref, pv_ref, nx_ref, cw_ref, cb_ref, wa_ref, ba_ref, wx_ref, bx_ref, lam_ref, e0_ref,
         ag_ref, hf_ref, _alias, o_ref, a_s, b_s, carry) = refs
    else:
        (x_ref, pv_ref, nx_ref, cw_ref, cb_ref, wa_ref, ba_ref, wx_ref, bx_ref, lam_ref, e0_ref,
         _alias, o_ref, a_s, b_s, carry) = refs
    s = pl.program_id(1)
    cg = (n_cg - 1 - s) if rev else s
    bsz, rows, ncol, cb = x_ref.shape

    @pl.when(s == 0)
    def _():
        carry[...] = e0_ref[...]

    x = x_ref[...]
    col = lax.broadcasted_iota(jnp.int32, (1, ncol, 1), 1)
    not_first = (cg > 0).astype(F32)
    not_last = (cg < n_cg - 1).astype(F32)
    top = jnp.where(col == 0, pltpu.roll(pv_ref[:, SUBLANE - 1], 1, 1) * not_first, pltpu.roll(x[:, rows - 1], 1, 1))
    bot1 = jnp.where(col == ncol - 1, pltpu.roll(nx_ref[:, 0], ncol - 1, 1) * not_last, pltpu.roll(x[:, 0], ncol - 1, 1))
    bot2 = jnp.where(col == ncol - 1, pltpu.roll(nx_ref[:, 1], ncol - 1, 1) * not_last, pltpu.roll(x[:, 1], ncol - 1, 1))
    xe = jnp.concatenate([top[:, None], x, bot1[:, None], bot2[:, None]], axis=1)
    u = cb_ref[...].reshape(1, 1, 1, cb)
    for k in range(4):
        u = u + cw_ref[k:k + 1, :].reshape(1, 1, 1, cb) * xe[:, k:k + rows]
    a, bt = _lru_gates(u.reshape(bsz * rows * ncol, cb), wa_ref, ba_ref, wx_ref, bx_ref, lam_ref)
    a_s[...] = a.reshape(bsz, rows, ncol, cb)
    b_s[...] = bt.reshape(bsz, rows, ncol, cb)

    def body(t, hp):
        h, p = hp
        r = (rows - 1 - t) if rev else t
        a_t = a_s[:, r]
        h = a_t * h + b_s[:, r]
        p = p * a_t
        b_s[:, r] = h
        a_s[:, r] = p
        return h, p

    h_end, p_end = lax.fori_loop(0, rows, body, (jnp.zeros((bsz, ncol, cb), F32), jnp.ones((bsz, ncol, cb), F32)), unroll=8)
    av, bv = p_end, h_end
    sh = 1
    while sh < ncol:
        if rev:
            valid = col < ncol - sh
            amt = ncol - sh
        else:
            valid = col >= sh
            amt = sh
        b_sh = jnp.where(valid, pltpu.roll(bv, amt, 1), 0.0)
        a_sh = jnp.where(valid, pltpu.roll(av, amt, 1), 1.0)
        bv = bv + av * b_sh
        av = av * a_sh
        sh *= 2
    e_prev = carry[...]
    e = bv + av * e_prev
    if rev:
        c_in = jnp.where(col == ncol - 1, e_prev, pltpu.roll(e, ncol - 1, 1))
        carry[...] = jnp.broadcast_to(e[:, 0:1], e.shape)
    else:
        c_in = jnp.where(col == 0, e_prev, pltpu.roll(e, 1, 1))
        carry[...] = jnp.broadcast_to(e[:, ncol - 1:ncol], e.shape)
    h = b_s[...] + a_s[...] * c_in[:, None]
    if rev:
        o_ref[...] = (hf_ref[...] + h) * _gelu_tanh(ag_ref[...])
    else:
        o_ref[...] = h


def _lru_ctx_kernel(*refs, rev):
    o_ref = refs[-2]

    @pl.when(pl.program_id(1) == 0)
    def _():
        _lru_ctx_scan(*refs, rev=rev)

    @pl.when(pl.program_id(1) > 0)
    def _():
        o_ref[...] = jnp.zeros(o_ref.shape, o_ref.dtype)


def _lru_ctx_scan(*refs, rev):
    if rev:
        (x_ref, cw_ref, cb_ref, wa_ref, ba_ref, wx_ref, bx_ref, lam_ref, ag_ref, hf_ref, o_ref, e_ref) = refs
    else:
        (x_ref, cw_ref, cb_ref, wa_ref, ba_ref, wx_ref, bx_ref, lam_ref, o_ref, e_ref) = refs
    bsz, nc, cb = x_ref.shape
    x = x_ref[...]
    t = lax.broadcasted_iota(jnp.int32, (1, nc, 1), 1)
    xm1 = jnp.where(t >= 1, pltpu.roll(x, 1, 1), 0.0)
    xp1 = jnp.where(t < nc - 1, pltpu.roll(x, nc - 1, 1), 0.0)
    xp2 = jnp.where(t < nc - 2, pltpu.roll(x, nc - 2, 1), 0.0)
    w = [cw_ref[k:k + 1, :].reshape(1, 1, cb) for k in range(4)]
    u = w[0] * xm1 + w[1] * x + w[2] * xp1 + w[3] * xp2 + cb_ref[...].reshape(1, 1, cb)
    a, bt = _lru_gates(u.reshape(bsz * nc, cb), wa_ref, ba_ref, wx_ref, bx_ref, lam_ref)
    av = a.reshape(bsz, nc, cb)
    bv = bt.reshape(bsz, nc, cb)
    sh = 1
    while sh < nc:
        if rev:
            valid = t < nc - sh
            amt = nc - sh
        else:
            valid = t >= sh
            amt = sh
        b_sh = jnp.where(valid, pltpu.roll(bv, amt, 1), 0.0)
        a_sh = jnp.where(valid, pltpu.roll(av, amt, 1), 1.0)
        bv = bv + av * b_sh
        av = av * a_sh
        sh *= 2
    if rev:
        o_ref[...] = (hf_ref[...] + bv) * _gelu_tanh(ag_ref[...])
        e_ref[...] = jnp.broadcast_to(bv[:, 0:1], (bsz, SUBLANE, cb))
    else:
        o_ref[...] = bv
        e_ref[...] = jnp.broadcast_to(bv[:, nc - 1:nc], (bsz, SUBLANE, cb))


def _rglru(feat, p, d_rnn, n_lat):
    bsz, t, nf = feat.shape
    nc = t - n_lat
    nb, bs, _ = p["wa"][0].shape
    cb = bs
    assert d_rnn == nb * bs and cb % LANE == 0 and n_lat % nc == 0 and nc % GRID_W == 0
    rows = n_lat // GRID_W
    assert rows % SUBLANE == 0 and GRID_W % SUBLANE == 0
    n_cg = GRID_W // SUBLANE
    ag_off = d_rnn // cb
    feat4 = feat.reshape(bsz, t // GRID_W, GRID_W, nf)
    ctx_blk = n_lat // nc

    def wspecs(im):
        return [pl.BlockSpec((4, cb), im(lambda j: (0, j))),
                pl.BlockSpec((1, cb), im(lambda j: (0, j))),
                pl.BlockSpec((1, bs, bs), im(lambda j: (j, 0, 0))),
                pl.BlockSpec((1, cb), im(lambda j: (0, j))),
                pl.BlockSpec((1, bs, bs), im(lambda j: (j, 0, 0))),
                pl.BlockSpec((1, cb), im(lambda j: (0, j))),
                pl.BlockSpec((1, cb), im(lambda j: (0, j)))]

    def wargs(d):
        return [p["conv_w"], p["conv_b"], p["wa"][d], p["ba"][d], p["wx"][d], p["bx"][d], p["lam"][d]]

    im1 = lambda f: (lambda j: f(j))
    im2 = lambda f: (lambda j, s: f(j))
    hf = None
    out = None
    for rev in (False, True):
        in_specs = [pl.BlockSpec((bsz, nc, cb), lambda j, s: (0, ctx_blk, j))] + wspecs(im2)
        args = [feat] + wargs(int(rev))
        if rev:
            in_specs += [pl.BlockSpec((bsz, nc, cb), lambda j, s: (0, ctx_blk, ag_off + j)),
                         pl.BlockSpec((bsz, nc, cb), lambda j, s: (0, ctx_blk, j))]
            args += [feat, hf]
        part, e0 = pl.pallas_call(
            functools.partial(_lru_ctx_kernel, rev=rev),
            grid=(nb, 1 + ctx_blk),
            in_specs=in_specs,
            out_specs=[pl.BlockSpec((bsz, nc, cb), lambda j, s: (0, jnp.where(s == 0, ctx_blk, s - 1), j)),
                       pl.BlockSpec((bsz, SUBLANE, cb), lambda j, s: (0, 0, j))],
            out_shape=[_sds((bsz, t, d_rnn), F32), _sds((bsz, SUBLANE, d_rnn), F32)],
            compiler_params=_cparams("parallel", "arbitrary"),
            name="lru_ctx_bwd" if rev else "lru_ctx_fwd",
        )(*args)
        cgm = (lambda s: n_cg - 1 - s) if rev else (lambda s: s)
        in_specs = [pl.BlockSpec((bsz, rows, SUBLANE, cb), lambda j, s: (0, 0, cgm(s), j)),
                    pl.BlockSpec((bsz, SUBLANE, SUBLANE, cb), lambda j, s: (0, rows // SUBLANE - 1, jnp.maximum(cgm(s) - 1, 0), j)),
                    pl.BlockSpec((bsz, SUBLANE, SUBLANE, cb), lambda j, s: (0, 0, jnp.minimum(cgm(s) + 1, n_cg - 1), j))]
        in_specs += wspecs(im2) + [pl.BlockSpec((bsz, SUBLANE, cb), lambda j, s: (0, 0, j))]
        args = [feat4, feat4, feat4] + wargs(int(rev)) + [e0]
        if rev:
            in_specs += [pl.BlockSpec((bsz, rows, SUBLANE, cb), lambda j, s: (0, 0, cgm(s), ag_off + j)),
                         pl.BlockSpec((bsz, rows, SUBLANE, cb), lambda j, s: (0, 0, cgm(s), j))]
            args += [feat4, hf.reshape(bsz, t // GRID_W, GRID_W, d_rnn)]
        in_specs += [pl.BlockSpec(memory_space=pl.ANY)]
        args += [part.reshape(bsz, t // GRID_W, GRID_W, d_rnn)]
        res = pl.pallas_call(
            functools.partial(_lru_lat_kernel, rev=rev, n_cg=n_cg),
            grid=(nb, n_cg),
            in_specs=in_specs,
            out_specs=pl.BlockSpec((bsz, rows, SUBLANE, cb), lambda j, s: (0, 0, cgm(s), j)),
            out_shape=_sds((bsz, t // GRID_W, GRID_W, d_rnn), F32),
            scratch_shapes=[pltpu.VMEM((bsz, rows, SUBLANE, cb), F32),
                            pltpu.VMEM((bsz, rows, SUBLANE, cb), F32),
                            pltpu.VMEM((bsz, SUBLANE, cb), F32)],
            input_output_aliases={len(args) - 1: 0},
            compiler_params=_cparams("parallel", "arbitrary"),
            name="lru_lat_bwd" if rev else "lru_lat_fwd",
        )(*args)
        res = res.reshape(bsz, t, d_rnn)
        if rev:
            out = res
        else:
            hf = res
    return out


def _chunk_of(s, n_ch, n_lat_ch, rev):
    if rev:
        return n_ch - 1 - s
    return jnp.where(s < n_ch - n_lat_ch, s + n_lat_ch, s - (n_ch - n_lat_ch))


MIXER_SPS = 2


def _mixer_sps(bsz):
    return MIXER_SPS if bsz % MIXER_SPS == 0 else 1


def _head_norm(o, g):
    mu = jnp.mean(o, axis=-1, keepdims=True)
    oc = o - mu
    var = jnp.mean(oc * oc, axis=-1, keepdims=True)
    return oc * lax.rsqrt(var + LN_EPS) * g


def _gla_kernel(*refs, rev, heads):
    if rev:
        (q_ref, k_ref, v_ref, lr_ref, wlr_ref, ba_ref, tri_ref, rsel_ref, r_ref, of_ref, ng_ref, o_ref, st) = refs
    else:
        (q_ref, k_ref, v_ref, lr_ref, wlr_ref, ba_ref, tri_ref, rsel_ref, o_ref, st) = refs
    s = pl.program_id(1)

    @pl.when(s == 0)
    def _():
        st[...] = jnp.zeros(st.shape, F32)

    dk = q_ref.shape[-1] // heads
    dv = v_ref.shape[-1] // heads
    tri = tri_ref[...]
    for i in range(q_ref.shape[0]):
        g = jax.nn.log_sigmoid(_dot(lr_ref[i], wlr_ref[...], precision=HI) + ba_ref[...]) / GLA_TAU
        bcum = _dot(tri, g, precision=HI)
        bm = _dot(rsel_ref[...], g, precision=HI)
        q = q_ref[i] * dk ** -0.5
        k = k_ref[i]
        v = v_ref[i]
        for h in range(heads):
            sk = slice(h * dk, (h + 1) * dk)
            sv = slice(h * dv, (h + 1) * dv)
            qh, kh, bh = q[:, sk], k[:, sk], bcum[:, sk]
            vh = v[:, sv].astype(BF16)
            bmid, blast = bm[0:1, sk], bm[1:2, sk]
            att = _dot((qh * jnp.exp(bh - bmid)).astype(BF16), (kh * jnp.exp(bmid - bh)).astype(BF16), NT) * tri
            sth = st[i, h]
            o = _dot(att.astype(BF16), vh) + _dot((qh * jnp.exp(bh)).astype(BF16), sth.astype(BF16), NT)
            st[i, h] = jnp.exp(blast) * sth + _dot(vh, (kh * jnp.exp(blast - bh)).astype(BF16), TN)
            if rev:
                o = _head_norm(of_ref[i, :, sv] + o, ng_ref[:, sv]) * jax.nn.silu(r_ref[i, :, sv])
            o_ref[i, :, sv] = o.astype(o_ref.dtype)


def _scan_consts(rev):
    i = jnp.arange(CHUNK)
    tri = (i[None, :] >= i[:, None]) if rev else (i[None, :] <= i[:, None])
    tri = tri.astype(F32)
    mid = CHUNK // 2 if rev else CHUNK // 2 - 1
    last = 0 if rev else CHUNK - 1
    rsel = jnp.zeros((SUBLANE, CHUNK), F32).at[0].set(tri[mid]).at[1].set(tri[last])
    return tri, rsel


def _gla(feat, feat_s, p, offs, n_lat):
    bsz, t, _ = feat.shape
    kg, vg = p["ba"][0].shape[-1], p["norm_g"].shape[-1]
    oq, ok, ov, orr = offs
    assert oq % kg == 0 and ok % kg == 0 and ov % vg == 0 and orr % vg == 0
    n_ch, n_lat_ch = t // CHUNK, n_lat // CHUNK
    sps = _mixer_sps(bsz)
    of = None
    for rev in (False, True):
        d = int(rev)
        tri, rsel = _scan_consts(rev)
        cm = lambda b, s: (b, _chunk_of(s, n_ch, n_lat_ch, rev))
        const = lambda b, s: (0, 0)
        in_specs = [pl.BlockSpec((sps, CHUNK, kg), lambda b, s: cm(b, s) + (oq // kg,)),
                    pl.BlockSpec((sps, CHUNK, kg), lambda b, s: cm(b, s) + (ok // kg,)),
                    pl.BlockSpec((sps, CHUNK, vg), lambda b, s: cm(b, s) + (ov // vg,)),
                    pl.BlockSpec((sps, CHUNK, LANE), lambda b, s: cm(b, s) + (0,)),
                    pl.BlockSpec((LANE, kg), const),
                    pl.BlockSpec((1, kg), const),
                    pl.BlockSpec((CHUNK, CHUNK), const),
                    pl.BlockSpec((SUBLANE, CHUNK), const)]
        args = [feat, feat, feat, feat_s, p["wlr"][d], p["ba"][d], tri, rsel]
        if rev:
            in_specs += [pl.BlockSpec((sps, CHUNK, vg), lambda b, s: cm(b, s) + (orr // vg,)),
                         pl.BlockSpec((sps, CHUNK, vg), lambda b, s: cm(b, s) + (0,)),
                         pl.BlockSpec((1, vg), const)]
            args += [feat, of, p["norm_g"]]
        res = pl.pallas_call(
            functools.partial(_gla_kernel, rev=rev, heads=GLA_HEADS),
            grid=(bsz // sps, n_ch),
            in_specs=in_specs,
            out_specs=pl.BlockSpec((sps, CHUNK, vg), lambda b, s: cm(b, s) + (0,)),
            out_shape=_sds((bsz, t, vg), BF16 if rev else F32),
            scratch_shapes=[pltpu.VMEM((sps, GLA_HEADS, vg // GLA_HEADS, kg // GLA_HEADS), F32)],
            compiler_params=_cparams("parallel", "arbitrary"),
            name="gla_bwd" if rev else "gla_fwd",
        )(*args)
        of = res
    return of


def _conv_rows(x, prev_row, next_rows, w_ref, b_ref, lo, hi):
    n = x.shape[0]
    t = lax.broadcasted_iota(jnp.int32, (n, 1), 0)
    xm1 = jnp.where(t == 0, prev_row, pltpu.roll(x, 1, 0))
    xp1 = jnp.where(t == n - 1, next_rows[0:1], pltpu.roll(x, n - 1, 0))
    xp2 = jnp.where(t == n - 2, next_rows[0:1], jnp.where(t == n - 1, next_rows[1:2], pltpu.roll(x, n - 2, 0)))
    w = w_ref[:, lo:hi]
    return w[0:1] * xm1 + w[1:2] * x + w[2:3] * xp1 + w[3:4] * xp2 + b_ref[:, lo:hi]


def _mlstm_kernel(*refs, rev, heads, n_ch, n_lat_ch, g_off):
    if rev:
        (q_ref, qp_ref, qn_ref, k_ref, kp_ref, kn_ref, v_ref, cg_ref, cw_ref, cbias_ref, tri_ref, trit_ref,
         og_ref, hf_ref, ng_ref, o_ref, c_s, n_s, m_s) = refs
    else:
        (q_ref, qp_ref, qn_ref, k_ref, kp_ref, kn_ref, v_ref, cg_ref, cw_ref, cbias_ref, tri_ref, trit_ref,
         o_ref, c_s, n_s, m_s) = refs
    s = pl.program_id(1)

    @pl.when(s == 0)
    def _():
        c_s[...] = jnp.zeros(c_s.shape, F32)
        n_s[...] = jnp.zeros(n_s.shape, F32)
        m_s[...] = jnp.zeros(m_s.shape, F32)

    ch = _chunk_of(s, n_ch, n_lat_ch, rev)
    not_first = jnp.logical_and(ch != 0, ch != n_lat_ch).astype(F32)
    not_last = jnp.logical_and(ch != n_lat_ch - 1, ch != n_ch - 1).astype(F32)
    wm = q_ref.shape[-1]
    dh = wm // heads
    tri = tri_ref[...]
    ci0 = g_off + int(rev) * 2 * heads
    cf0 = ci0 + heads
    lane = lax.broadcasted_iota(jnp.int32, (SUBLANE, LANE), 1)
    row = lax.broadcasted_iota(jnp.int32, (SUBLANE, LANE), 0)
    sel_i = (lane == row + ci0).astype(F32)
    sel_f = (lane == row + cf0).astype(F32)
    last = 0 if rev else CHUNK - 1
    for i in range(q_ref.shape[0]):
        qc = jax.nn.silu(_conv_rows(q_ref[i], qp_ref[i, SUBLANE - 1:SUBLANE] * not_first, qn_ref[i, 0:2] * not_last,
                                    cw_ref, cbias_ref, 0, wm))
        kc = jax.nn.silu(_conv_rows(k_ref[i], kp_ref[i, SUBLANE - 1:SUBLANE] * not_first, kn_ref[i, 0:2] * not_last,
                                    cw_ref, cbias_ref, wm, 2 * wm)) * dh ** -0.5
        v = v_ref[i]
        gts = cg_ref[i]
        gls = jax.nn.log_sigmoid(gts)
        bcol_all = _dot(tri, gls, precision=HI)
        ig_rows = _dot(sel_i, gts, NT, precision=HI)
        b_rows = _dot(_dot(sel_f, gls, NT, precision=HI), trit_ref[...], precision=HI)
        for h in range(heads):
            sl = slice(h * dh, (h + 1) * dh)
            qh = qc[:, sl]
            qb, kb, vh = qh.astype(BF16), kc[:, sl].astype(BF16), v[:, sl]
            bc = bcol_all[:, cf0 + h:cf0 + h + 1]
            igc = gts[:, ci0 + h:ci0 + h + 1]
            m = m_s[i, h:h + 1, 0:1]
            dmat = jnp.where(tri > 0.0, bc - b_rows[h:h + 1, :] + ig_rows[h:h + 1, :], -jnp.inf)
            inter = bc + m
            m_row = jnp.maximum(jnp.max(dmat, axis=-1, keepdims=True), inter)
            pmat = _dot(qb, kb, NT) * jnp.exp(dmat - m_row)
            s_inter = jnp.exp(inter - m_row)
            cm = c_s[i, h]
            nv = n_s[i, h:h + 1, :]
            num = _dot(pmat.astype(BF16), vh.astype(BF16)) + s_inter * _dot(qb, cm.astype(BF16), NT)
            den = jnp.sum(pmat, axis=-1, keepdims=True) + s_inter * jnp.sum(qh * nv, axis=-1, keepdims=True)
            hout = num / jnp.maximum(jnp.abs(den), jnp.exp(-m_row))
            b_last = bc[last:last + 1]
            wl = b_last - bc + igc
            m_new = jnp.maximum(b_last + m, jnp.max(wl, axis=0, keepdims=True))
            sw = jnp.exp(wl - m_new)
            decay = jnp.exp(b_last + m - m_new)
            c_s[i, h] = decay * cm + _dot((sw * vh).astype(BF16), kb, TN)
            n_s[i, h:h + 1, :] = decay * nv + jnp.sum(sw * kc[:, sl], axis=0, keepdims=True)
            m_s[i, h:h + 1, :] = jnp.broadcast_to(m_new, (1, LANE))
            if rev:
                hout = _head_norm(hf_ref[i, :, sl] + hout, ng_ref[:, sl]) * jax.nn.sigmoid(og_ref[i, :, sl])
            o_ref[i, :, sl] = hout.astype(o_ref.dtype)


def _mlstm(feat, feat_s, p, offs, g_off, n_lat):
    bsz, t, _ = feat.shape
    wm = p["norm_g"].shape[-1]
    oq, ok, ov, oo = offs
    assert all(o % wm == 0 for o in offs)
    n_ch, n_lat_ch = t // CHUNK, n_lat // CHUNK
    n_r8 = t // SUBLANE
    per = CHUNK // SUBLANE
    sps = _mixer_sps(bsz)
    hf = None
    for rev in (False, True):
        tri, _ = _scan_consts(rev)
        chm = lambda s: _chunk_of(s, n_ch, n_lat_ch, rev)
        const = lambda b, s: (0, 0)

        def cur(off):
            return pl.BlockSpec((sps, CHUNK, wm), lambda b, s: (b, chm(s), off // wm))

        def prv(off):
            return pl.BlockSpec((sps, SUBLANE, wm), lambda b, s: (b, jnp.maximum(chm(s) * per - 1, 0), off // wm))

        def nxt(off):
            return pl.BlockSpec((sps, SUBLANE, wm), lambda b, s: (b, jnp.minimum((chm(s) + 1) * per, n_r8 - 1), off // wm))

        in_specs = [cur(oq), prv(oq), nxt(oq), cur(ok), prv(ok), nxt(ok), cur(ov),
                    pl.BlockSpec((sps, CHUNK, LANE), lambda b, s: (b, chm(s), 0)),
                    pl.BlockSpec((4, 2 * wm), const),
                    pl.BlockSpec((1, 2 * wm), const),
                    pl.BlockSpec((CHUNK, CHUNK), const),
                    pl.BlockSpec((CHUNK, CHUNK), const)]
        args = [feat] * 7 + [feat_s, p["conv_w"], p["conv_b"], tri, tri.T]
        if rev:
            in_specs += [cur(oo), pl.BlockSpec((sps, CHUNK, wm), lambda b, s: (b, chm(s), 0)), pl.BlockSpec((1, wm), const)]
            args += [feat, hf, p["norm_g"]]
        dh = wm // MLSTM_HEADS
        hf = pl.pallas_call(
            functools.partial(_mlstm_kernel, rev=rev, heads=MLSTM_HEADS, n_ch=n_ch, n_lat_ch=n_lat_ch, g_off=g_off),
            grid=(bsz // sps, n_ch),
            in_specs=in_specs,
            out_specs=pl.BlockSpec((sps, CHUNK, wm), lambda b, s: (b, chm(s), 0)),
            out_shape=_sds((bsz, t, wm), BF16 if rev else F32),
            scratch_shapes=[pltpu.VMEM((sps, MLSTM_HEADS, dh, dh), F32),
                            pltpu.VMEM((sps, SUBLANE, dh), F32),
                            pltpu.VMEM((sps, SUBLANE, LANE), F32)],
            compiler_params=_cparams("parallel", "arbitrary"),
            name="mlstm_bwd" if rev else "mlstm_fwd",
        )(*args)
    return hf


def _merge_kernel(h_ref, wg0_ref, wg1_ref, wg2_ref, bg_ref, y0_ref, y1_ref, y2_ref, wb_ref, o_ref):
    h = h_ref[...]
    acc = None
    for n, (wg_ref, y_ref) in enumerate(((wg0_ref, y0_ref), (wg1_ref, y1_ref), (wg2_ref, y2_ref))):
        g = jax.nn.sigmoid(_dot(h, wg_ref[...]) + bg_ref[n])
        term = g * _dot(y_ref[...].astype(BF16), wb_ref[n])
        acc = term if acc is None else acc + term
    o_ref[...] = acc.astype(o_ref.dtype)


def _merge(h, w_gate, layer, bg, ys, wb):
    m, d = h.shape
    r = ys[0].shape[1]
    tm, tn = 256, _pick_tile(d, 512)
    nj = d // tn
    wspecs = [pl.BlockSpec((None, d, tn), lambda j, i, n=n: (layer, 0, n * nj + j)) for n in range(3)]
    yspec = pl.BlockSpec((tm, r), lambda j, i: (i, 0))
    return pl.pallas_call(
        _merge_kernel,
        grid=(nj, m // tm),
        in_specs=[pl.BlockSpec((tm, d), lambda j, i: (i, 0))] + wspecs +
                 [pl.BlockSpec((3, 1, tn), lambda j, i: (0, 0, j)),
                  yspec, yspec, yspec,
                  pl.BlockSpec((None, 3, r, tn), lambda j, i: (layer, 0, 0, j))],
        out_specs=pl.BlockSpec((tm, tn), lambda j, i: (i, j)),
        out_shape=_sds((m, d), BF16),
        compiler_params=_cparams("parallel", "parallel"),
        name="merge",
    )(h, w_gate, w_gate, w_gate, bg, *ys, wb)


def _router_kernel(x_ref, w_ref, o_ref):
    logits = _dot(w_ref[...], x_ref[0], NT)
    ex = jnp.exp(logits - jnp.max(logits, axis=0, keepdims=True))
    o_ref[0] = ex / jnp.sum(ex, axis=0, keepdims=True)


def _router(xm, w_rt):
    bsz, t, d = xm.shape
    e = w_rt.shape[0]
    tr = ROW_TILE
    return pl.pallas_call(
        _router_kernel,
        grid=(bsz, t // tr),
        in_specs=[pl.BlockSpec((1, tr, d), lambda b, i: (b, i, 0)),
                  pl.BlockSpec((e, d), lambda b, i: (0, 0))],
        out_specs=pl.BlockSpec((1, e, tr), lambda b, i: (b, 0, i)),
        out_shape=_sds((bsz, e, t), F32),
        compiler_params=_cparams("parallel", "parallel"),
        name="router",
    )(xm, w_rt)


def _prefix_excl(src_ref, dst_ref, upper):
    e, n = src_ref.shape
    off = jnp.zeros((e, 1), F32)
    for kb in range(n // LANE):
        blk = src_ref[:, kb * LANE:(kb + 1) * LANE]
        inc = _dot(blk.astype(BF16), upper)
        dst_ref[:, kb * LANE:(kb + 1) * LANE] = inc - blk + off
        off = off + inc[:, LANE - 1:LANE]


TOKEN_SPLIT_BITS = 6
TOKEN_SPLIT = 1 << TOKEN_SPLIT_BITS


def _topk_kernel(a_ref, slot_ref, slot_t_ref, gv_t_ref, idx_ref, m_s, r_s, *, cap):
    aff = a_ref[0]
    e, n = aff.shape
    bits = pltpu.bitcast(aff, jnp.int32)
    thr = jnp.zeros((e, 1), jnp.int32)
    for bit in range(30, -1, -1):
        cand = thr | (1 << bit)
        cnt = jnp.sum((bits >= cand).astype(jnp.int32), axis=1, keepdims=True)
        thr = jnp.where(cnt >= cap, cand, thr)
    gt = (bits > thr).astype(F32)
    eq = (bits == thr).astype(F32)
    need = cap - jnp.sum(gt, axis=1, keepdims=True)
    ii = lax.broadcasted_iota(jnp.int32, (LANE, LANE), 0)
    jj = lax.broadcasted_iota(jnp.int32, (LANE, LANE), 1)
    upper = (ii <= jj).astype(BF16)
    m_s[...] = eq
    _prefix_excl(m_s, r_s, upper)
    sel = gt + eq * (r_s[...] < need).astype(F32)
    m_s[...] = sel
    _prefix_excl(m_s, r_s, upper)
    slot = jnp.where(sel > 0.0, r_s[...], -1.0)
    slot_ref[0] = slot
    eye = (lax.broadcasted_iota(jnp.int32, (e, e), 0) == lax.broadcasted_iota(jnp.int32, (e, e), 1)).astype(F32)
    slot_t_ref[0] = _dot(slot, eye, TN, precision=HI)
    gv_t_ref[0] = _dot(aff, eye, TN, precision=HI)
    capp = idx_ref.shape[-1]
    tok = lax.broadcasted_iota(jnp.int32, (SUBLANE, n), 1)
    part = lax.broadcasted_iota(jnp.int32, (SUBLANE, n), 0)
    tvals = jnp.where(part == 0, tok >> TOKEN_SPLIT_BITS, tok & (TOKEN_SPLIT - 1)).astype(F32).astype(BF16)
    sidx = lax.broadcasted_iota(jnp.int32, (capp, n), 0).astype(F32)
    for ei in range(e):
        onehot = jnp.where(slot[ei:ei + 1, :] == sidx, 1.0, 0.0).astype(BF16)
        parts = _dot(tvals, onehot, NT)
        idx_ref[0, ei:ei + 1, :] = (parts[0:1] * TOKEN_SPLIT + parts[1:2]).astype(jnp.int32)


def _topk(aff_t, blk, ntok, cap, capp):
    bsz, e, _ = aff_t.shape
    assert ntok <= TOKEN_SPLIT * 256
    return pl.pallas_call(
        functools.partial(_topk_kernel, cap=cap),
        grid=(bsz,),
        in_specs=[pl.BlockSpec((1, e, ntok), lambda b: (b, 0, blk))],
        out_specs=[pl.BlockSpec((1, e, ntok), lambda b: (b, 0, 0)),
                   pl.BlockSpec((1, ntok, e), lambda b: (b, 0, 0)),
                   pl.BlockSpec((1, ntok, e), lambda b: (b, 0, 0)),
                   pl.BlockSpec((1, e, capp), lambda b: (b, 0, 0))],
        out_shape=[_sds((bsz, e, ntok), F32), _sds((bsz, ntok, e), F32), _sds((bsz, ntok, e), F32),
                   _sds((bsz, e, capp), jnp.int32)],
        scratch_shapes=[pltpu.VMEM((e, ntok), F32), pltpu.VMEM((e, ntok), F32)],
        compiler_params=_cparams("parallel"),
        name="topk",
    )(aff_t)


def _expert_up_kernel(slot_ref, x_ref, wg_ref, wu_ref, o_ref, p_s, g_s, u_s):
    k = pl.program_id(2)
    sps, capp, n = p_s.shape

    @pl.when(k == 0)
    def _():
        sidx = lax.broadcasted_iota(jnp.int32, (capp, n), 0).astype(F32)
        for i in range(sps):
            srow = slot_ref[i, pl.ds(pl.program_id(1), 1), :]
            p_s[i] = jnp.where(srow == sidx, 1.0, 0.0).astype(BF16)
        g_s[...] = jnp.zeros(g_s.shape, F32)
        u_s[...] = jnp.zeros(u_s.shape, F32)

    xg = jnp.concatenate([_dot(p_s[i], x_ref[i]).astype(BF16) for i in range(sps)], axis=0)
    g_s[...] += _dot(xg, wg_ref[0].astype(BF16))
    u_s[...] += _dot(xg, wu_ref[0].astype(BF16))

    @pl.when(k == pl.num_programs(2) - 1)
    def _():
        hid = (jax.nn.silu(g_s[...]) * u_s[...]).astype(o_ref.dtype)
        for i in range(sps):
            o_ref[i, 0] = hid[i * capp:(i + 1) * capp]


def _expert_up(slot, xm, blk, ntok, capp, sps, w_gate, w_up, layer):
    bsz, e = slot.shape[:2]
    d, ff = w_gate.shape[2:]
    tk = 512
    return pl.pallas_call(
        _expert_up_kernel,
        grid=(bsz // sps, e, d // tk),
        in_specs=[pl.BlockSpec((sps, e, ntok), lambda b, ei, k: (b, 0, 0)),
                  pl.BlockSpec((sps, ntok, tk), lambda b, ei, k: (b, blk, k)),
                  pl.BlockSpec((None, 1, tk, ff), lambda b, ei, k: (layer, ei, k, 0)),
                  pl.BlockSpec((None, 1, tk, ff), lambda b, ei, k: (layer, ei, k, 0))],
        out_specs=pl.BlockSpec((sps, 1, capp, ff), lambda b, ei, k: (b, ei, 0, 0)),
        out_shape=_sds((bsz, e, capp, ff), BF16),
        scratch_shapes=[pltpu.VMEM((sps, capp, ntok), BF16), pltpu.VMEM((sps * capp, ff), F32),
                        pltpu.VMEM((sps * capp, ff), F32)],
        compiler_params=_cparams("parallel", "parallel", "arbitrary"),
        name="expert_up",
    )(slot, xm, w_gate, w_up)


def _expert_up_gather_kernel(idx_ref, xs_hbm, m_ref, wg_ref, wu_ref, o_ref, x32_s, xb_s, g_s, u_s, sem, *, off, n_exp):
    b, ei, k = pl.program_id(0), pl.program_id(1), pl.program_id(2)
    cap = x32_s.shape[0]
    tk = wg_ref.shape[1]

    def row_copy(s, row):
        return pltpu.make_async_copy(xs_hbm.at[b, pl.ds(row, 1), :], x32_s.at[pl.ds(s, 1), :], sem)

    @pl.when(k == 0)
    def _():
        def issue(s, carry):
            row_copy(s, idx_ref[b * n_exp + ei, s]).start()
            return carry

        lax.fori_loop(0, cap, issue, 0)

        def wait(s, carry):
            row_copy(s, 0).wait()
            return carry

        lax.fori_loop(0, cap, wait, 0)
        sh = m_ref[0, off:off + 1, :]
        sc = m_ref[0, off + 1:off + 2, :]
        xb_s[...] = (x32_s[...] * (1.0 + sc) + sh).astype(BF16)
        g_s[...] = jnp.zeros(g_s.shape, F32)
        u_s[...] = jnp.zeros(u_s.shape, F32)

    xk = xb_s[:, pl.ds(pl.multiple_of(k * tk, tk), tk)]
    g_s[...] += _dot(xk, wg_ref[0].astype(BF16))
    u_s[...] += _dot(xk, wu_ref[0].astype(BF16))

    @pl.when(k == pl.num_programs(2) - 1)
    def _():
        o_ref[0, 0] = (jax.nn.silu(g_s[...]) * u_s[...]).astype(o_ref.dtype)


def _expert_up_gather(idx, xs, mod6, off, cap, w_gate, w_up, layer):
    bsz, e, _ = idx.shape
    d, ff = w_gate.shape[2:]
    tk = 512
    grid_spec = pltpu.PrefetchScalarGridSpec(
        num_scalar_prefetch=1,
        grid=(bsz, e, d // tk),
        in_specs=[pl.BlockSpec(memory_space=pl.ANY),
                  pl.BlockSpec((1, 6, d), lambda b, ei, k, idx_ref: (b, 0, 0)),
                  pl.BlockSpec((None, 1, tk, ff), lambda b, ei, k, idx_ref: (layer, ei, k, 0)),
                  pl.BlockSpec((None, 1, tk, ff), lambda b, ei, k, idx_ref: (layer, ei, k, 0))],
        out_specs=pl.BlockSpec((1, 1, cap, ff), lambda b, ei, k, idx_ref: (b, ei, 0, 0)),
        scratch_shapes=[pltpu.VMEM((cap, d), F32), pltpu.VMEM((cap, d), BF16),
                        pltpu.VMEM((cap, ff), F32), pltpu.VMEM((cap, ff), F32),
                        pltpu.SemaphoreType.DMA(())])
    return pl.pallas_call(
        functools.partial(_expert_up_gather_kernel, off=off, n_exp=e),
        grid_spec=grid_spec,
        out_shape=_sds((bsz, e, cap, ff), BF16),
        compiler_params=_cparams("parallel", "parallel", "arbitrary"),
        name="expert_up_gather",
    )(idx.reshape(bsz * e, idx.shape[-1]), xs, mod6, w_gate, w_up)


def _expert_down_kernel(hid_ref, wd_ref, o_ref, w_s):
    @pl.when(pl.program_id(2) == 0)
    def _():
        w_s[...] = wd_ref[0].astype(BF16)

    o_ref[0, 0] = _dot(hid_ref[0, 0], w_s[...]).astype(o_ref.dtype)


def _expert_down(hid, w_down, layer):
    bsz, e, capp, ff = hid.shape
    d = w_down.shape[-1]
    tn = _pick_tile(d, 2048)
    return pl.pallas_call(
        _expert_down_kernel,
        grid=(e, d // tn, bsz),
        in_specs=[pl.BlockSpec((1, 1, capp, ff), lambda ei, j, b: (b, ei, 0, 0)),
                  pl.BlockSpec((None, 1, ff, tn), lambda ei, j, b: (layer, ei, 0, j))],
        out_specs=pl.BlockSpec((1, 1, capp, tn), lambda ei, j, b: (b, ei, 0, j)),
        out_shape=_sds((bsz, e, capp, d), BF16),
        scratch_shapes=[pltpu.VMEM((ff, tn), BF16)],
        compiler_params=_cparams("parallel", "parallel", "arbitrary"),
        name="expert_down",
    )(hid, w_down)


COMBINE_TN = 512
COMBINE_VMEM_LIMIT = 56 << 20


def _combine_kernel(*refs, nq):
    ye_ref, slot_t_ref, gv_t_ref = refs[:3]
    o_ref = refs[-1]
    q = pl.program_id(1)
    ei = pl.program_id(2)
    n, e = slot_t_ref.shape[1:]
    capp, d = ye_ref.shape[2:]

    @pl.when(ei == 0)
    def _():
        o_ref[...] = jnp.zeros(o_ref.shape, F32)

    @pl.when(q < nq)
    def _():
        pick = (lax.broadcasted_iota(jnp.int32, (e, LANE), 0) == ei).astype(F32)
        slot_b = _dot(slot_t_ref[0], pick, precision=HI)
        gv_b = _dot(gv_t_ref[0], pick, precision=HI)
        lane = lax.broadcasted_iota(jnp.int32, (1, LANE), 1).astype(F32)
        pt = jnp.concatenate([jnp.where(slot_b == lane + float(c * LANE), 1.0, 0.0).astype(BF16)
                              for c in range(capp // LANE)], axis=1)
        tn = COMBINE_TN
        gv = jnp.concatenate([gv_b] * (tn // LANE), axis=1)
        for j in range(d // tn):
            o_ref[0, :, j * tn:(j + 1) * tn] += _dot(pt, ye_ref[0, 0, :, j * tn:(j + 1) * tn]) * gv


def _combine(ye, slot_t, gv_t, tq, blk0, t, prev=None):
    bsz, e, capp, d = ye.shape
    ntok = slot_t.shape[1]
    nq = ntok // tq
    n_steps = nq if prev is not None else pl.cdiv(t, tq) - blk0
    qc = lambda q: jnp.minimum(q, nq - 1)
    in_specs = [pl.BlockSpec((1, 1, capp, d), lambda b, q, ei: (b, jnp.where(q < nq, ei, e - 1), 0, 0)),
                pl.BlockSpec((1, tq, e), lambda b, q, ei: (b, qc(q), 0)),
                pl.BlockSpec((1, tq, e), lambda b, q, ei: (b, qc(q), 0))]
    args = [ye, slot_t, gv_t]
    aliases = {}
    if prev is not None:
        in_specs.append(pl.BlockSpec(memory_space=pl.ANY))
        args.append(prev)
        aliases = {3: 0}
    return pl.pallas_call(
        functools.partial(_combine_kernel, nq=nq),
        grid=(bsz, n_steps, e),
        in_specs=in_specs,
        out_specs=pl.BlockSpec((1, tq, d), lambda b, q, ei: (b, blk0 + q, 0)),
        out_shape=_sds((bsz, t, d), F32),
        input_output_aliases=aliases,
        compiler_params=pltpu.CompilerParams(dimension_semantics=("parallel", "parallel", "arbitrary"),
                                             vmem_limit_bytes=COMBINE_VMEM_LIMIT),
        name="combine",
    )(*args)


def _round_up(x, m):
    return (x + m - 1) // m * m


def _moe_part(aff_t, xm, blk, ntok, sps, tq, t_out, we, layer, prev=None, gather_src=None):
    e = aff_t.shape[1]
    cap = EC_CAPACITY * ntok // e
    capp = _round_up(cap, LANE)
    slot, slot_t, gv_t, idx = _topk(aff_t, blk, ntok, cap, capp)
    if gather_src is not None:
        assert blk == 0 and cap == capp
        hid = _expert_up_gather(idx, *gather_src, cap, we["gate"], we["up"], layer)
    else:
        hid = _expert_up(slot, xm, blk, ntok, capp, sps, we["gate"], we["up"], layer)
    ye = _expert_down(hid, we["down"], layer)
    return _combine(ye, slot_t, gv_t, tq, blk * ntok // tq, t_out, prev)


def kernel(x, c, ctx, c_ctx, w_mod, b_mod, w_in, b_in, conv_a_w, conv_a_b, lru_wa, lru_ba, lru_wx, lru_bx, lru_lam, gla_wa2, gla_ba, gla_norm_g, conv_c_w, conv_c_b, mlstm_norm_g, w_branch, w_out, ln1_g, ln1_b, w_router, w_e_gate, w_e_up, w_e_down, ln2_g, ln2_b):
    bsz, n_lat, d = x.shape
    nc = ctx.shape[1]
    t = n_lat + nc
    depth = w_mod.shape[0]
    d_rnn = conv_a_w.shape[-1]
    kg, vg = gla_ba.shape[-1], gla_norm_g.shape[-1]
    rank = gla_wa2.shape[2]
    wm = mlstm_norm_g.shape[-1]
    n_gate = 4 * MLSTM_HEADS
    n_exp = w_router.shape[-1]
    alpha = (2 * depth) ** 0.25
    assert 2 * rank + n_gate <= LANE and t % ROW_TILE == 0 and n_lat % ROW_TILE == 0 and bsz < SUBLANE

    sizes = (d_rnn, d_rnn, kg, kg, vg, vg, 2 * rank, wm, wm, wm, wm, n_gate)
    offs = [0]
    for sz in sizes:
        offs.append(offs[-1] + sz)
    n_feat = offs[-1]
    take_main = lambda a: jnp.concatenate([a[..., offs[0]:offs[6]], a[..., offs[7]:offs[11]]], axis=-1)
    take_small = lambda a: jnp.concatenate([a[..., offs[6]:offs[7]], a[..., offs[11]:offs[12]]], axis=-1)
    n_main = (offs[6] - offs[0]) + (offs[11] - offs[7])
    pad_s = LANE - (2 * rank + n_gate)
    mo = {"a_x": 0, "a_g": d_rnn, "b_q": 2 * d_rnn, "b_k": 2 * d_rnn + kg, "b_v": 2 * d_rnn + 2 * kg,
          "b_r": 2 * d_rnn + 2 * kg + vg}
    mo["c_q"] = mo["b_r"] + vg
    mo["c_k"], mo["c_v"], mo["c_o"] = mo["c_q"] + wm, mo["c_q"] + 2 * wm, mo["c_q"] + 3 * wm

    cvec = jnp.concatenate([c, c_ctx[None], jnp.zeros((SUBLANE - bsz - 1, d), F32)], axis=0)
    mod = _mod_all(cvec, w_mod, b_mod)
    xs = jnp.concatenate([x, ctx], axis=1)

    tc = LANE
    while tc < 512 and all(v % (2 * tc) == 0 for v in (offs[6], 4 * wm, d)):
        tc *= 2
    w_in_t = jnp.swapaxes(w_in, 1, 2)
    w_main = _wprep(w_in_t, tc, 0, n_main // tc, ((0, offs[6] // tc, 0), (offs[6] // tc, n_main // tc, offs[7] - offs[6])),
                    "wprep_main")
    w_gate = _wprep(w_in_t, tc, n_main // tc, 3 * d // tc, ((0, 3 * d // tc, n_feat - n_main),), "wprep_gate")
    w_small = jnp.pad(take_small(w_in), ((0, 0), (0, 0), (0, pad_s)))
    w_br = w_branch.astype(BF16)
    w_rt = jnp.swapaxes(w_router, 1, 2).astype(BF16)
    we = {"gate": w_e_gate, "up": w_e_up, "down": w_e_down}

    mod6s = [mod[l].reshape(SUBLANE, 6, d) for l in range(depth)]
    h = _modulate(xs, mod6s[0], 0, n_lat)
    for l in range(depth):
        last = l == depth - 1
        mod6 = mod6s[l]
        b_main = take_main(b_in[l]).reshape(1, n_main)
        b_small = jnp.pad(take_small(b_in[l]), (0, pad_s)).reshape(1, LANE)
        bg = b_in[l][n_feat:].reshape(3, 1, d)
        lru_p = {"conv_w": conv_a_w[l], "conv_b": conv_a_b[l].reshape(1, d_rnn),
                 "wa": lru_wa[l].astype(BF16), "ba": lru_ba[l].reshape(2, 1, d_rnn),
                 "wx": lru_wx[l].astype(BF16), "bx": lru_bx[l].reshape(2, 1, d_rnn),
                 "lam": lru_lam[l].reshape(2, 1, d_rnn)}
        wlr = jnp.zeros((2, LANE, kg), F32)
        for dd in range(2):
            wlr = wlr.at[dd, dd * rank:(dd + 1) * rank].set(gla_wa2[l, dd])
        gla_p = {"wlr": wlr, "ba": gla_ba[l].reshape(2, 1, kg), "norm_g": gla_norm_g[l].reshape(1, vg)}
        ml_p = {"conv_w": conv_c_w[l], "conv_b": conv_c_b[l].reshape(1, 2 * wm), "norm_g": mlstm_norm_g[l].reshape(1, wm)}

        h = h.reshape(bsz * t, d)
        feat = _matmul(h, w_main, l, b_main, F32, 512, 1024, "feat").reshape(bsz, t, n_main)
        feat_s = _matmul(h, w_small, l, b_small, F32, 1024, LANE, "feat_small").reshape(bsz, t, LANE)
        y0 = _rglru(feat, lru_p, d_rnn, n_lat)
        y1 = _gla(feat, feat_s, gla_p, (mo["b_q"], mo["b_k"], mo["b_v"], mo["b_r"]), n_lat)
        y2 = _mlstm(feat, feat_s, ml_p, (mo["c_q"], mo["c_k"], mo["c_v"], mo["c_o"]), 2 * rank, n_lat)
        ys = [y.reshape(bsz * t, -1) for y in (y0, y1, y2)]
        merged = _merge(h, w_gate, l, bg, ys, w_br)
        y = _matmul(merged, w_out, l, jnp.zeros((1, d), F32), F32, 512, 512, "out_proj").reshape(bsz, t, d)
        xs, xm = _res_ln(xs, y, mod6, 2, ln1_g[l], ln1_b[l], alpha, n_lat, t, mod6, 3)

        aff_t = _router(xm, w_rt[l])
        t_out = n_lat if last else t
        f = _moe_part(aff_t, xm, 0, n_lat, 1, _pick_tile(n_lat, 1024), t_out, we, l, gather_src=(xs, mod6, 3))
        if not last:
            f = _moe_part(aff_t, xm, n_lat // nc, nc, bsz, nc, t_out, we, l, prev=f)
        if last:
            xs = _res_ln(xs, f, mod6, 5, ln2_g[l], ln2_b[l], alpha, n_lat, t_out)
        else:
            xs, h = _res_ln(xs, f, mod6, 5, ln2_g[l], ln2_b[l], alpha, n_lat, t_out, mod6s[l + 1], 0)
    return xs
```

```python
import functools

import jax
import jax.numpy as jnp
from jax import lax
from jax.experimental import pallas as pl
from jax.experimental.pallas import tpu as pltpu

F32 = jnp.float32
BF16 = jnp.bfloat16
HI = lax.Precision.HIGHEST

GRID_W = 64
CHUNK = 64
LRU_C = 8.0
GLA_HEADS = 4
GLA_TAU = 16.0
MLSTM_HEADS = 4
EC_CAPACITY = 2
LN_EPS = 1e-5

LANE = 128
SUBLANE = 8
ROW_TILE = 256
VMEM_LIMIT = 48 << 20
VMEM_LIMIT_BIG = 56 << 20

NT = (((1,), (1,)), ((), ()))
TN = (((0,), (0,)), ((), ()))


def _cparams(*sem, vmem=VMEM_LIMIT):
    return pltpu.CompilerParams(dimension_semantics=sem, vmem_limit_bytes=vmem)


def _sds(shape, dtype):
    return jax.ShapeDtypeStruct(shape, dtype)


def _dot(a, b, dims=None, precision=None):
    if dims is None:
        return jnp.dot(a, b, preferred_element_type=F32, precision=precision)
    return lax.dot_general(a, b, dims, preferred_element_type=F32, precision=precision)


def _mod_kernel(c_ref, w_ref, b_ref, o_ref):
    a = jax.nn.silu(c_ref[...]).astype(BF16)
    o_ref[0] = _dot(a, w_ref[0].astype(BF16)) + b_ref[0]


def _mod_all(cvec, w_mod, b_mod):
    depth, d, n6 = w_mod.shape
    tn = 512
    return pl.pallas_call(
        _mod_kernel,
        grid=(depth, n6 // tn),
        in_specs=[pl.BlockSpec((SUBLANE, d), lambda l, j: (0, 0)),
                  pl.BlockSpec((1, d, tn), lambda l, j: (l, 0, j)),
                  pl.BlockSpec((1, 1, tn), lambda l, j: (l, 0, j))],
        out_specs=pl.BlockSpec((1, SUBLANE, tn), lambda l, j: (l, 0, j)),
        out_shape=_sds((depth, SUBLANE, n6), F32),
        compiler_params=_cparams("parallel", "parallel"),
        name="mod",
    )(cvec, w_mod, b_mod.reshape(depth, 1, n6))


def _modulate_kernel(x_ref, m_ref, o_ref, *, off):
    sh = m_ref[0, off:off + 1, :]
    sc = m_ref[0, off + 1:off + 2, :]
    o_ref[0] = (x_ref[0] * (1.0 + sc) + sh).astype(o_ref.dtype)


def _mod_row_map(n_lat_tiles, batch):
    return lambda b, i: (jnp.where(i < n_lat_tiles, b, batch), 0, 0)


def _modulate(xs, mod6, off, n_lat):
    batch, t, d = xs.shape
    tr = ROW_TILE
    return pl.pallas_call(
        functools.partial(_modulate_kernel, off=off),
        grid=(batch, t // tr),
        in_specs=[pl.BlockSpec((1, tr, d), lambda b, i: (b, i, 0)),
                  pl.BlockSpec((1, 6, d), _mod_row_map(n_lat // tr, batch))],
        out_specs=pl.BlockSpec((1, tr, d), lambda b, i: (b, i, 0)),
        out_shape=_sds((batch, t, d), BF16),
        compiler_params=_cparams("parallel", "parallel"),
        name="modulate",
    )(xs, mod6)


def _mm_kernel(a_ref, w_ref, b_ref, o_ref):
    o_ref[...] = (_dot(a_ref[...], w_ref[...]) + b_ref[...]).astype(o_ref.dtype)


def _pick_tile(n, pref):
    while pref > LANE and n % pref:
        pref //= 2
    assert n % pref == 0
    return pref


def _mm_castw_kernel(a_ref, w_ref, b_ref, o_ref, w_s):
    @pl.when(pl.program_id(1) == 0)
    def _():
        w_s[...] = w_ref[...].astype(BF16)

    o_ref[...] = (_dot(a_ref[...], w_s[...]) + b_ref[...]).astype(o_ref.dtype)


def _matmul(a, w, layer, bias, out_dtype, tm, tn, name):
    m, k = a.shape
    n = w.shape[2]
    tm, tn = _pick_tile(m, tm), _pick_tile(n, tn)
    cast = w.dtype == F32
    return pl.pallas_call(
        _mm_castw_kernel if cast else _mm_kernel,
        grid=(n // tn, m // tm),
        in_specs=[pl.BlockSpec((tm, k), lambda j, i: (i, 0)),
                  pl.BlockSpec((None, k, tn), lambda j, i: (layer, 0, j)),
                  pl.BlockSpec((1, tn), lambda j, i: (0, j))],
        out_specs=pl.BlockSpec((tm, tn), lambda j, i: (i, j)),
        out_shape=_sds((m, n), out_dtype),
        scratch_shapes=[pltpu.VMEM((k, tn), BF16)] if cast else [],
        compiler_params=_cparams("parallel", "arbitrary" if cast else "parallel"),
        name=name,
    )(a, w, bias)


def _wprep_kernel(a_ref, b_ref, o_ref, *, regions):
    j = pl.program_id(2)
    tc = o_ref.shape[-1]
    for lo, hi, shift in regions:
        @pl.when(jnp.logical_and(j >= lo, j < hi))
        def _():
            if shift == 0:
                rows = a_ref[...]
            else:
                rows = jnp.concatenate([a_ref[...], b_ref[:LANE]], axis=0)[shift:shift + tc]
            o_ref[...] = rows.T.astype(BF16)


def _wprep(w_t, tc, base_blk, n_blk, regions, name):
    depth, _, k = w_t.shape
    assert all(0 <= s < LANE and s % SUBLANE == 0 for _, _, s in regions)
    tr = _pick_tile(k, 1024)
    return pl.pallas_call(
        functools.partial(_wprep_kernel, regions=regions),
        grid=(depth, k // tr, n_blk),
        in_specs=[pl.BlockSpec((None, tc, tr), lambda l, i, j: (l, base_blk + j, i)),
                  pl.BlockSpec((None, tc, tr), lambda l, i, j: (l, base_blk + j + 1, i))],
        out_specs=pl.BlockSpec((None, tr, tc), lambda l, i, j: (l, i, j)),
        out_shape=_sds((depth, k, n_blk * tc), BF16),
        compiler_params=_cparams("parallel", "parallel", "parallel"),
        name=name,
    )(w_t, w_t)


def _res_ln_kernel(*refs, off, alpha, next_off):
    if next_off is None:
        x_ref, y_ref, m_ref, g_ref, b_ref, o_ref = refs
    else:
        x_ref, y_ref, m_ref, g_ref, b_ref, mn_ref, o_ref, on_ref = refs
    gate = m_ref[0, off:off + 1, :]
    z = alpha * x_ref[0] + gate * y_ref[0]
    mu = jnp.mean(z, axis=-1, keepdims=True)
    zc = z - mu
    var = jnp.mean(zc * zc, axis=-1, keepdims=True)
    out = zc * lax.rsqrt(var + LN_EPS) * g_ref[...] + b_ref[...]
    o_ref[0] = out
    if next_off is not None:
        sh = mn_ref[0, next_off:next_off + 1, :]
        sc = mn_ref[0, next_off + 1:next_off + 2, :]
        on_ref[0] = (out * (1.0 + sc) + sh).astype(on_ref.dtype)


def _res_ln(xs, y, mod6, off, ln_g, ln_b, alpha, n_lat, t_out, next_mod6=None, next_off=None):
    batch, t, d = xs.shape
    tr = ROW_TILE
    row = pl.BlockSpec((1, tr, d), lambda b, i: (b, i, 0))
    modspec = pl.BlockSpec((1, 6, d), _mod_row_map(n_lat // tr, batch))
    vec = pl.BlockSpec((1, d), lambda b, i: (0, 0))
    in_specs = [row, row, modspec, vec, vec]
    args = [xs, y, mod6, ln_g.reshape(1, d), ln_b.reshape(1, d)]
    out_specs, out_shape = row, _sds((batch, t_out, d), F32)
    if next_off is not None:
        in_specs.append(modspec)
        args.append(next_mod6)
        out_specs, out_shape = [row, row], [out_shape, _sds((batch, t_out, d), BF16)]
    return pl.pallas_call(
        functools.partial(_res_ln_kernel, off=off, alpha=alpha, next_off=next_off),
        grid=(batch, t_out // tr),
        in_specs=in_specs,
        out_specs=out_specs,
        out_shape=out_shape,
        compiler_params=_cparams("parallel", "parallel"),
        name="res_ln",
    )(*args)


def _gelu_tanh(x):
    return jax.nn.gelu(x, approximate=True)


def _lru_gates(u2, wa_ref, ba_ref, wx_ref, bx_ref, lam_ref):
    ub = u2.astype(BF16)
    r = jax.nn.sigmoid(_dot(ub, wa_ref[0]) + ba_ref[...])
    i = jax.nn.sigmoid(_dot(ub, wx_ref[0]) + bx_ref[...])
    log_a = (-LRU_C * jax.nn.softplus(-lam_ref[...])) * r
    a = jnp.exp(log_a)
    bt = jnp.sqrt(1.0 - jnp.exp(2.0 * log_a)) * (i * u2)
    return a, bt


def _lru_lat_kernel(*refs, rev, n_cg):
    if rev:
        (x_ref, pv_ref, nx_ref, cw_ref, cb_ref, wa_ref, ba_ref, wx_ref, bx_ref, lam_ref, e0_ref,
         ag_ref, hf_ref, _alias, o_ref, a_s, b_s, carry) = refs
    else:
        (x_ref, pv_ref, nx_ref, cw_ref, cb_ref, wa_ref, ba_ref, wx_ref, bx_ref, lam_ref, e0_ref,
         _alias, o_ref, a_s, b_s, carry) = refs
    s = pl.program_id(1)
    cg = (n_cg - 1 - s) if rev else s
    bsz, rows, ncol, cb = x_ref.shape

    @pl.when(s == 0)
    def _():
        carry[...] = e0_ref[...]

    x = x_ref[...]
    col = lax.broadcasted_iota(jnp.int32, (1, ncol, 1), 1)
    not_first = (cg > 0).astype(F32)
    not_last = (cg < n_cg - 1).astype(F32)
    top = jnp.where(col == 0, pltpu.roll(pv_ref[:, SUBLANE - 1], 1, 1) * not_first, pltpu.roll(x[:, rows - 1], 1, 1))
    bot1 = jnp.where(col == ncol - 1, pltpu.roll(nx_ref[:, 0], ncol - 1, 1) * not_last, pltpu.roll(x[:, 0], ncol - 1, 1))
    bot2 = jnp.where(col == ncol - 1, pltpu.roll(nx_ref[:, 1], ncol - 1, 1) * not_last, pltpu.roll(x[:, 1], ncol - 1, 1))
    xe = jnp.concatenate([top[:, None], x, bot1[:, None], bot2[:, None]], axis=1)
    u = cb_ref[...].reshape(1, 1, 1, cb)
    for k in range(4):
        u = u + cw_ref[k:k + 1, :].reshape(1, 1, 1, cb) * xe[:, k:k + rows]
    a, bt = _lru_gates(u.reshape(bsz * rows * ncol, cb), wa_ref, ba_ref, wx_ref, bx_ref, lam_ref)
    a_s[...] = a.reshape(bsz, rows, ncol, cb)
    b_s[...] = bt.reshape(bsz, rows, ncol, cb)

    def body(t, hp):
        h, p = hp
        r = (rows - 1 - t) if rev else t
        a_t = a_s[:, r]
        h = a_t * h + b_s[:, r]
        p = p * a_t
        b_s[:, r] = h
        a_s[:, r] = p
        return h, p

    h_end, p_end = lax.fori_loop(0, rows, body, (jnp.zeros((bsz, ncol, cb), F32), jnp.ones((bsz, ncol, cb), F32)), unroll=8)
    av, bv = p_end, h_end
    sh = 1
    while sh < ncol:
        if rev:
            valid = col < ncol - sh
            amt = ncol - sh
        else:
            valid = col >= sh
            amt = sh
        b_sh = jnp.where(valid, pltpu.roll(bv, amt, 1), 0.0)
        a_sh = jnp.where(valid, pltpu.roll(av, amt, 1), 1.0)
        bv = bv + av * b_sh
        av = av * a_sh
        sh *= 2
    e_prev = carry[...]
    e = bv + av * e_prev
    if rev:
        c_in = jnp.where(col == ncol - 1, e_prev, pltpu.roll(e, ncol - 1, 1))
        carry[...] = jnp.broadcast_to(e[:, 0:1], e.shape)
    else:
        c_in = jnp.where(col == 0, e_prev, pltpu.roll(e, 1, 1))
        carry[...] = jnp.broadcast_to(e[:, ncol - 1:ncol], e.shape)
    h = b_s[...] + a_s[...] * c_in[:, None]
    if rev:
        o_ref[...] = (hf_ref[...] + h) * _gelu_tanh(ag_ref[...])
    else:
        o_ref[...] = h


def _lru_ctx_kernel(*refs, rev):
    if rev:
        (x_ref, cw_ref, cb_ref, wa_ref, ba_ref, wx_ref, bx_ref, lam_ref, ag_ref, hf_ref, _base, o_ref, e_ref) = refs
    else:
        (x_ref, cw_ref, cb_ref, wa_ref, ba_ref, wx_ref, bx_ref, lam_ref, _base, o_ref, e_ref) = refs
    bsz, nc, cb = x_ref.shape
    x = x_ref[...]
    t = lax.broadcasted_iota(jnp.int32, (1, nc, 1), 1)
    xm1 = jnp.where(t >= 1, pltpu.roll(x, 1, 1), 0.0)
    xp1 = jnp.where(t < nc - 1, pltpu.roll(x, nc - 1, 1), 0.0)
    xp2 = jnp.where(t < nc - 2, pltpu.roll(x, nc - 2, 1), 0.0)
    w = [cw_ref[k:k + 1, :].reshape(1, 1, cb) for k in range(4)]
    u = w[0] * xm1 + w[1] * x + w[2] * xp1 + w[3] * xp2 + cb_ref[...].reshape(1, 1, cb)
    a, bt = _lru_gates(u.reshape(bsz * nc, cb), wa_ref, ba_ref, wx_ref, bx_ref, lam_ref)
    av = a.reshape(bsz, nc, cb)
    bv = bt.reshape(bsz, nc, cb)
    sh = 1
    while sh < nc:
        if rev:
            valid = t < nc - sh
            amt = nc - sh
        else:
            valid = t >= sh
            amt = sh
        b_sh = jnp.where(valid, pltpu.roll(bv, amt, 1), 0.0)
        a_sh = jnp.where(valid, pltpu.roll(av, amt, 1), 1.0)
        bv = bv + av * b_sh
        av = av * a_sh
        sh *= 2
    if rev:
        o_ref[...] = (hf_ref[...] + bv) * _gelu_tanh(ag_ref[...])
        e_ref[...] = jnp.broadcast_to(bv[:, 0:1], (bsz, SUBLANE, cb))
    else:
        o_ref[...] = bv
        e_ref[...] = jnp.broadcast_to(bv[:, nc - 1:nc], (bsz, SUBLANE, cb))


def _rglru(feat, p, d_rnn, n_lat):
    bsz, t, nf = feat.shape
    nc = t - n_lat
    nb, bs, _ = p["wa"][0].shape
    cb = bs
    assert d_rnn == nb * bs and cb % LANE == 0 and n_lat % nc == 0 and nc % GRID_W == 0
    rows = n_lat // GRID_W
    assert rows % SUBLANE == 0 and GRID_W % SUBLANE == 0
    n_cg = GRID_W // SUBLANE
    ag_off = d_rnn // cb
    feat4 = feat.reshape(bsz, t // GRID_W, GRID_W, nf)
    ctx_blk = n_lat // nc

    def wspecs(im):
        return [pl.BlockSpec((4, cb), im(lambda j: (0, j))),
                pl.BlockSpec((1, cb), im(lambda j: (0, j))),
                pl.BlockSpec((1, bs, bs), im(lambda j: (j, 0, 0))),
                pl.BlockSpec((1, cb), im(lambda j: (0, j))),
                pl.BlockSpec((1, bs, bs), im(lambda j: (j, 0, 0))),
                pl.BlockSpec((1, cb), im(lambda j: (0, j))),
                pl.BlockSpec((1, cb), im(lambda j: (0, j)))]

    def wargs(d):
        return [p["conv_w"], p["conv_b"], p["wa"][d], p["ba"][d], p["wx"][d], p["bx"][d], p["lam"][d]]

    im1 = lambda f: (lambda j: f(j))
    im2 = lambda f: (lambda j, s: f(j))
    hf = None
    out = None
    for rev in (False, True):
        in_specs = [pl.BlockSpec((bsz, nc, cb), lambda j: (0, ctx_blk, j))] + wspecs(im1)
        args = [feat] + wargs(int(rev))
        if rev:
            in_specs += [pl.BlockSpec((bsz, nc, cb), lambda j: (0, ctx_blk, ag_off + j)),
                         pl.BlockSpec((bsz, nc, cb), lambda j: (0, ctx_blk, j))]
            args += [feat, hf]
        in_specs += [pl.BlockSpec(memory_space=pl.ANY)]
        args += [jnp.zeros((bsz, t, d_rnn), F32)]
        part, e0 = pl.pallas_call(
            functools.partial(_lru_ctx_kernel, rev=rev),
            grid=(nb,),
            in_specs=in_specs,
            out_specs=[pl.BlockSpec((bsz, nc, cb), lambda j: (0, ctx_blk, j)),
                       pl.BlockSpec((bsz, SUBLANE, cb), lambda j: (0, 0, j))],
            out_shape=[_sds((bsz, t, d_rnn), F32), _sds((bsz, SUBLANE, d_rnn), F32)],
            input_output_aliases={len(args) - 1: 0},
            compiler_params=_cparams("parallel"),
            name="lru_ctx_bwd" if rev else "lru_ctx_fwd",
        )(*args)
        cgm = (lambda s: n_cg - 1 - s) if rev else (lambda s: s)
        in_specs = [pl.BlockSpec((bsz, rows, SUBLANE, cb), lambda j, s: (0, 0, cgm(s), j)),
                    pl.BlockSpec((bsz, SUBLANE, SUBLANE, cb), lambda j, s: (0, rows // SUBLANE - 1, jnp.maximum(cgm(s) - 1, 0), j)),
                    pl.BlockSpec((bsz, SUBLANE, SUBLANE, cb), lambda j, s: (0, 0, jnp.minimum(cgm(s) + 1, n_cg - 1), j))]
        in_specs += wspecs(im2) + [pl.BlockSpec((bsz, SUBLANE, cb), lambda j, s: (0, 0, j))]
        args = [feat4, feat4, feat4] + wargs(int(rev)) + [e0]
        if rev:
            in_specs += [pl.BlockSpec((bsz, rows, SUBLANE, cb), lambda j, s: (0, 0, cgm(s), ag_off + j)),
                         pl.BlockSpec((bsz, rows, SUBLANE, cb), lambda j, s: (0, 0, cgm(s), j))]
            args += [feat4, hf.reshape(bsz, t // GRID_W, GRID_W, d_rnn)]
        in_specs += [pl.BlockSpec(memory_space=pl.ANY)]
        args += [part.reshape(bsz, t // GRID_W, GRID_W, d_rnn)]
        res = pl.pallas_call(
            functools.partial(_lru_lat_kernel, rev=rev, n_cg=n_cg),
            grid=(nb, n_cg),
            in_specs=in_specs,
            out_specs=pl.BlockSpec((bsz, rows, SUBLANE, cb), lambda j, s: (0, 0, cgm(s), j)),
            out_shape=_sds((bsz, t // GRID_W, GRID_W, d_rnn), F32),
            scratch_shapes=[pltpu.VMEM((bsz, rows, SUBLANE, cb), F32),
                            pltpu.VMEM((bsz, rows, SUBLANE, cb), F32),
                            pltpu.VMEM((bsz, SUBLANE, cb), F32)],
            input_output_aliases={len(args) - 1: 0},
            compiler_params=_cparams("parallel", "arbitrary"),
            name="lru_lat_bwd" if rev else "lru_lat_fwd",
        )(*args)
        res = res.reshape(bsz, t, d_rnn)
        if rev:
            out = res
        else:
            hf = res
    return out


def _chunk_of(s, n_ch, n_lat_ch, rev):
    if rev:
        return n_ch - 1 - s
    return jnp.where(s < n_ch - n_lat_ch, s + n_lat_ch, s - (n_ch - n_lat_ch))


MIXER_SPS = 2


def _mixer_sps(bsz):
    return MIXER_SPS if bsz % MIXER_SPS == 0 else 1


def _head_norm(o, g):
    mu = jnp.mean(o, axis=-1, keepdims=True)
    oc = o - mu
    var = jnp.mean(oc * oc, axis=-1, keepdims=True)
    return oc * lax.rsqrt(var + LN_EPS) * g


def _gla_kernel(*refs, rev, heads):
    if rev:
        (q_ref, k_ref, v_ref, lr_ref, wlr_ref, ba_ref, tri_ref, rsel_ref, r_ref, of_ref, ng_ref, o_ref, st) = refs
    else:
        (q_ref, k_ref, v_ref, lr_ref, wlr_ref, ba_ref, tri_ref, rsel_ref, o_ref, st) = refs
    s = pl.program_id(1)

    @pl.when(s == 0)
    def _():
        st[...] = jnp.zeros(st.shape, F32)

    dk = q_ref.shape[-1] // heads
    dv = v_ref.shape[-1] // heads
    tri = tri_ref[...]
    for i in range(q_ref.shape[0]):
        g = jax.nn.log_sigmoid(_dot(lr_ref[i], wlr_ref[...], precision=HI) + ba_ref[...]) / GLA_TAU
        bcum = _dot(tri, g, precision=HI)
        bm = _dot(rsel_ref[...], g, precision=HI)
        q = q_ref[i] * dk ** -0.5
        k = k_ref[i]
        v = v_ref[i]
        for h in range(heads):
            sk = slice(h * dk, (h + 1) * dk)
            sv = slice(h * dv, (h + 1) * dv)
            qh, kh, bh = q[:, sk], k[:, sk], bcum[:, sk]
            vh = v[:, sv].astype(BF16)
            bmid, blast = bm[0:1, sk], bm[1:2, sk]
            att = _dot((qh * jnp.exp(bh - bmid)).astype(BF16), (kh * jnp.exp(bmid - bh)).astype(BF16), NT) * tri
            sth = st[i, h]
            o = _dot(att.astype(BF16), vh) + _dot((qh * jnp.exp(bh)).astype(BF16), sth.astype(BF16), NT)
            st[i, h] = jnp.exp(blast) * sth + _dot(vh, (kh * jnp.exp(blast - bh)).astype(BF16), TN)
            if rev:
                o = _head_norm(of_ref[i, :, sv] + o, ng_ref[:, sv]) * jax.nn.silu(r_ref[i, :, sv])
            o_ref[i, :, sv] = o.astype(o_ref.dtype)


def _scan_consts(rev):
    i = jnp.arange(CHUNK)
    tri = (i[None, :] >= i[:, None]) if rev else (i[None, :] <= i[:, None])
    tri = tri.astype(F32)
    mid = CHUNK // 2 if rev else CHUNK // 2 - 1
    last = 0 if rev else CHUNK - 1
    rsel = jnp.zeros((SUBLANE, CHUNK), F32).at[0].set(tri[mid]).at[1].set(tri[last])
    return tri, rsel


def _gla(feat, feat_s, p, offs, n_lat):
    bsz, t, _ = feat.shape
    kg, vg = p["ba"][0].shape[-1], p["norm_g"].shape[-1]
    oq, ok, ov, orr = offs
    assert oq % kg == 0 and ok % kg == 0 and ov % vg == 0 and orr % vg == 0
    n_ch, n_lat_ch = t // CHUNK, n_lat // CHUNK
    sps = _mixer_sps(bsz)
    of = None
    for rev in (False, True):
        d = int(rev)
        tri, rsel = _scan_consts(rev)
        cm = lambda b, s: (b, _chunk_of(s, n_ch, n_lat_ch, rev))
        const = lambda b, s: (0, 0)
        in_specs = [pl.BlockSpec((sps, CHUNK, kg), lambda b, s: cm(b, s) + (oq // kg,)),
                    pl.BlockSpec((sps, CHUNK, kg), lambda b, s: cm(b, s) + (ok // kg,)),
                    pl.BlockSpec((sps, CHUNK, vg), lambda b, s: cm(b, s) + (ov // vg,)),
                    pl.BlockSpec((sps, CHUNK, LANE), lambda b, s: cm(b, s) + (0,)),
                    pl.BlockSpec((LANE, kg), const),
                    pl.BlockSpec((1, kg), const),
                    pl.BlockSpec((CHUNK, CHUNK), const),
                    pl.BlockSpec((SUBLANE, CHUNK), const)]
        args = [feat, feat, feat, feat_s, p["wlr"][d], p["ba"][d], tri, rsel]
        if rev:
            in_specs += [pl.BlockSpec((sps, CHUNK, vg), lambda b, s: cm(b, s) + (orr // vg,)),
                         pl.BlockSpec((sps, CHUNK, vg), lambda b, s: cm(b, s) + (0,)),
                         pl.BlockSpec((1, vg), const)]
            args += [feat, of, p["norm_g"]]
        res = pl.pallas_call(
            functools.partial(_gla_kernel, rev=rev, heads=GLA_HEADS),
            grid=(bsz // sps, n_ch),
            in_specs=in_specs,
            out_specs=pl.BlockSpec((sps, CHUNK, vg), lambda b, s: cm(b, s) + (0,)),
            out_shape=_sds((bsz, t, vg), BF16 if rev else F32),
            scratch_shapes=[pltpu.VMEM((sps, GLA_HEADS, vg // GLA_HEADS, kg // GLA_HEADS), F32)],
            compiler_params=_cparams("parallel", "arbitrary"),
            name="gla_bwd" if rev else "gla_fwd",
        )(*args)
        of = res
    return of


def _conv_rows(x, prev_row, next_rows, w_ref, b_ref, lo, hi):
    n = x.shape[0]
    t = lax.broadcasted_iota(jnp.int32, (n, 1), 0)
    xm1 = jnp.where(t == 0, prev_row, pltpu.roll(x, 1, 0))
    xp1 = jnp.where(t == n - 1, next_rows[0:1], pltpu.roll(x, n - 1, 0))
    xp2 = jnp.where(t == n - 2, next_rows[0:1], jnp.where(t == n - 1, next_rows[1:2], pltpu.roll(x, n - 2, 0)))
    w = w_ref[:, lo:hi]
    return w[0:1] * xm1 + w[1:2] * x + w[2:3] * xp1 + w[3:4] * xp2 + b_ref[:, lo:hi]


def _mlstm_kernel(*refs, rev, heads, n_ch, n_lat_ch, g_off):
    if rev:
        (q_ref, qp_ref, qn_ref, k_ref, kp_ref, kn_ref, v_ref, cg_ref, cw_ref, cbias_ref, tri_ref, trit_ref,
         og_ref, hf_ref, ng_ref, o_ref, c_s, n_s, m_s) = refs
    else:
        (q_ref, qp_ref, qn_ref, k_ref, kp_ref, kn_ref, v_ref, cg_ref, cw_ref, cbias_ref, tri_ref, trit_ref,
         o_ref, c_s, n_s, m_s) = refs
    s = pl.program_id(1)

    @pl.when(s == 0)
    def _():
        c_s[...] = jnp.zeros(c_s.shape, F32)
        n_s[...] = jnp.zeros(n_s.shape, F32)
        m_s[...] = jnp.zeros(m_s.shape, F32)

    ch = _chunk_of(s, n_ch, n_lat_ch, rev)
    not_first = jnp.logical_and(ch != 0, ch != n_lat_ch).astype(F32)
    not_last = jnp.logical_and(ch != n_lat_ch - 1, ch != n_ch - 1).astype(F32)
    wm = q_ref.shape[-1]
    dh = wm // heads
    tri = tri_ref[...]
    ci0 = g_off + int(rev) * 2 * heads
    cf0 = ci0 + heads
    lane = lax.broadcasted_iota(jnp.int32, (SUBLANE, LANE), 1)
    row = lax.broadcasted_iota(jnp.int32, (SUBLANE, LANE), 0)
    sel_i = (lane == row + ci0).astype(F32)
    sel_f = (lane == row + cf0).astype(F32)
    last = 0 if rev else CHUNK - 1
    for i in range(q_ref.shape[0]):
        qc = jax.nn.silu(_conv_rows(q_ref[i], qp_ref[i, SUBLANE - 1:SUBLANE] * not_first, qn_ref[i, 0:2] * not_last,
                                    cw_ref, cbias_ref, 0, wm))
        kc = jax.nn.silu(_conv_rows(k_ref[i], kp_ref[i, SUBLANE - 1:SUBLANE] * not_first, kn_ref[i, 0:2] * not_last,
                                    cw_ref, cbias_ref, wm, 2 * wm)) * dh ** -0.5
        v = v_ref[i]
        gts = cg_ref[i]
        gls = jax.nn.log_sigmoid(gts)
        bcol_all = _dot(tri, gls, precision=HI)
        ig_rows = _dot(sel_i, gts, NT, precision=HI)
        b_rows = _dot(_dot(sel_f, gls, NT, precision=HI), trit_ref[...], precision=HI)
        for h in range(heads):
            sl = slice(h * dh, (h + 1) * dh)
            qh = qc[:, sl]
            qb, kb, vh = qh.astype(BF16), kc[:, sl].astype(BF16), v[:, sl]
            bc = bcol_all[:, cf0 + h:cf0 + h + 1]
            igc = gts[:, ci0 + h:ci0 + h + 1]
            m = m_s[i, h:h + 1, 0:1]
            dmat = jnp.where(tri > 0.0, bc - b_rows[h:h + 1, :] + ig_rows[h:h + 1, :], -jnp.inf)
            inter = bc + m
            m_row = jnp.maximum(jnp.max(dmat, axis=-1, keepdims=True), inter)
            pmat = _dot(qb, kb, NT) * jnp.exp(dmat - m_row)
            s_inter = jnp.exp(inter - m_row)
            cm = c_s[i, h]
            nv = n_s[i, h:h + 1, :]
            num = _dot(pmat.astype(BF16), vh.astype(BF16)) + s_inter * _dot(qb, cm.astype(BF16), NT)
            den = jnp.sum(pmat, axis=-1, keepdims=True) + s_inter * jnp.sum(qh * nv, axis=-1, keepdims=True)
            hout = num / jnp.maximum(jnp.abs(den), jnp.exp(-m_row))
            b_last = bc[last:last + 1]
            wl = b_last - bc + igc
            m_new = jnp.maximum(b_last + m, jnp.max(wl, axis=0, keepdims=True))
            sw = jnp.exp(wl - m_new)
            decay = jnp.exp(b_last + m - m_new)
            c_s[i, h] = decay * cm + _dot((sw * vh).astype(BF16), kb, TN)
            n_s[i, h:h + 1, :] = decay * nv + jnp.sum(sw * kc[:, sl], axis=0, keepdims=True)
            m_s[i, h:h + 1, :] = jnp.broadcast_to(m_new, (1, LANE))
            if rev:
                hout = _head_norm(hf_ref[i, :, sl] + hout, ng_ref[:, sl]) * jax.nn.sigmoid(og_ref[i, :, sl])
            o_ref[i, :, sl] = hout.astype(o_ref.dtype)


def _mlstm(feat, feat_s, p, offs, g_off, n_lat):
    bsz, t, _ = feat.shape
    wm = p["norm_g"].shape[-1]
    oq, ok, ov, oo = offs
    assert all(o % wm == 0 for o in offs)
    n_ch, n_lat_ch = t // CHUNK, n_lat // CHUNK
    n_r8 = t // SUBLANE
    per = CHUNK // SUBLANE
    sps = _mixer_sps(bsz)
    hf = None
    for rev in (False, True):
        tri, _ = _scan_consts(rev)
        chm = lambda s: _chunk_of(s, n_ch, n_lat_ch, rev)
        const = lambda b, s: (0, 0)

        def cur(off):
            return pl.BlockSpec((sps, CHUNK, wm), lambda b, s: (b, chm(s), off // wm))

        def prv(off):
            return pl.BlockSpec((sps, SUBLANE, wm), lambda b, s: (b, jnp.maximum(chm(s) * per - 1, 0), off // wm))

        def nxt(off):
            return pl.BlockSpec((sps, SUBLANE, wm), lambda b, s: (b, jnp.minimum((chm(s) + 1) * per, n_r8 - 1), off // wm))

        in_specs = [cur(oq), prv(oq), nxt(oq), cur(ok), prv(ok), nxt(ok), cur(ov),
                    pl.BlockSpec((sps, CHUNK, LANE), lambda b, s: (b, chm(s), 0)),
                    pl.BlockSpec((4, 2 * wm), const),
                    pl.BlockSpec((1, 2 * wm), const),
                    pl.BlockSpec((CHUNK, CHUNK), const),
                    pl.BlockSpec((CHUNK, CHUNK), const)]
        args = [feat] * 7 + [feat_s, p["conv_w"], p["conv_b"], tri, tri.T]
        if rev:
            in_specs += [cur(oo), pl.BlockSpec((sps, CHUNK, wm), lambda b, s: (b, chm(s), 0)), pl.BlockSpec((1, wm), const)]
            args += [feat, hf, p["norm_g"]]
        dh = wm // MLSTM_HEADS
        hf = pl.pallas_call(
            functools.partial(_mlstm_kernel, rev=rev, heads=MLSTM_HEADS, n_ch=n_ch, n_lat_ch=n_lat_ch, g_off=g_off),
            grid=(bsz // sps, n_ch),
            in_specs=in_specs,
            out_specs=pl.BlockSpec((sps, CHUNK, wm), lambda b, s: (b, chm(s), 0)),
            out_shape=_sds((bsz, t, wm), BF16 if rev else F32),
            scratch_shapes=[pltpu.VMEM((sps, MLSTM_HEADS, dh, dh), F32),
                            pltpu.VMEM((sps, SUBLANE, dh), F32),
                            pltpu.VMEM((sps, SUBLANE, LANE), F32)],
            compiler_params=_cparams("parallel", "arbitrary"),
            name="mlstm_bwd" if rev else "mlstm_fwd",
        )(*args)
    return hf


def _merge_kernel(h_ref, wg0_ref, wg1_ref, wg2_ref, bg_ref, y0_ref, y1_ref, y2_ref, wb_ref, o_ref):
    h = h_ref[...]
    acc = None
    for n, (wg_ref, y_ref) in enumerate(((wg0_ref, y0_ref), (wg1_ref, y1_ref), (wg2_ref, y2_ref))):
        g = jax.nn.sigmoid(_dot(h, wg_ref[...]) + bg_ref[n])
        term = g * _dot(y_ref[...].astype(BF16), wb_ref[n])
        acc = term if acc is None else acc + term
    o_ref[...] = acc.astype(o_ref.dtype)


def _merge(h, w_gate, layer, bg, ys, wb):
    m, d = h.shape
    r = ys[0].shape[1]
    tm, tn = _pick_tile(m, 512), _pick_tile(d, 512)
    nj = d // tn
    wspecs = [pl.BlockSpec((None, d, tn), lambda j, i, n=n: (layer, 0, n * nj + j)) for n in range(3)]
    yspec = pl.BlockSpec((tm, r), lambda j, i: (i, 0))
    return pl.pallas_call(
        _merge_kernel,
        grid=(nj, m // tm),
        in_specs=[pl.BlockSpec((tm, d), lambda j, i: (i, 0))] + wspecs +
                 [pl.BlockSpec((3, 1, tn), lambda j, i: (0, 0, j)),
                  yspec, yspec, yspec,
                  pl.BlockSpec((None, 3, r, tn), lambda j, i: (layer, 0, 0, j))],
        out_specs=pl.BlockSpec((tm, tn), lambda j, i: (i, j)),
        out_shape=_sds((m, d), BF16),
        compiler_params=_cparams("parallel", "parallel", vmem=VMEM_LIMIT_BIG),
        name="merge",
    )(h, w_gate, w_gate, w_gate, bg, *ys, wb)


def _router_kernel(x_ref, w_ref, o_ref):
    logits = _dot(w_ref[...], x_ref[0], NT)
    ex = jnp.exp(logits - jnp.max(logits, axis=0, keepdims=True))
    o_ref[0] = ex / jnp.sum(ex, axis=0, keepdims=True)


def _router(xm, w_rt):
    bsz, t, d = xm.shape
    e = w_rt.shape[0]
    tr = ROW_TILE
    return pl.pallas_call(
        _router_kernel,
        grid=(bsz, t // tr),
        in_specs=[pl.BlockSpec((1, tr, d), lambda b, i: (b, i, 0)),
                  pl.BlockSpec((e, d), lambda b, i: (0, 0))],
        out_specs=pl.BlockSpec((1, e, tr), lambda b, i: (b, 0, i)),
        out_shape=_sds((bsz, e, t), F32),
        compiler_params=_cparams("parallel", "parallel"),
        name="router",
    )(xm, w_rt)


def _prefix_excl(src_ref, dst_ref, upper):
    e, n = src_ref.shape
    off = jnp.zeros((e, 1), F32)
    for kb in range(n // LANE):
        blk = src_ref[:, kb * LANE:(kb + 1) * LANE]
        inc = _dot(blk.astype(BF16), upper)
        dst_ref[:, kb * LANE:(kb + 1) * LANE] = inc - blk + off
        off = off + inc[:, LANE - 1:LANE]


TOKEN_SPLIT_BITS = 6
TOKEN_SPLIT = 1 << TOKEN_SPLIT_BITS


def _topk_kernel(a_ref, slot_ref, slot_t_ref, gv_t_ref, idx_ref, m_s, r_s, *, cap):
    aff = a_ref[0]
    e, n = aff.shape
    bits = pltpu.bitcast(aff, jnp.int32)
    thr = jnp.zeros((e, 1), jnp.int32)
    for bit in range(30, -1, -1):
        cand = thr | (1 << bit)
        cnt = jnp.sum((bits >= cand).astype(jnp.int32), axis=1, keepdims=True)
        thr = jnp.where(cnt >= cap, cand, thr)
    gt = (bits > thr).astype(F32)
    eq = (bits == thr).astype(F32)
    need = cap - jnp.sum(gt, axis=1, keepdims=True)
    ii = lax.broadcasted_iota(jnp.int32, (LANE, LANE), 0)
    jj = lax.broadcasted_iota(jnp.int32, (LANE, LANE), 1)
    upper = (ii <= jj).astype(BF16)
    m_s[...] = eq
    _prefix_excl(m_s, r_s, upper)
    sel = gt + eq * (r_s[...] < need).astype(F32)
    m_s[...] = sel
    _prefix_excl(m_s, r_s, upper)
    slot = jnp.where(sel > 0.0, r_s[...], -1.0)
    slot_ref[0] = slot
    eye = (lax.broadcasted_iota(jnp.int32, (e, e), 0) == lax.broadcasted_iota(jnp.int32, (e, e), 1)).astype(F32)
    slot_t_ref[0] = _dot(slot, eye, TN, precision=HI)
    gv_t_ref[0] = _dot(aff, eye, TN, precision=HI)
    capp = idx_ref.shape[-1]
    tok = lax.broadcasted_iota(jnp.int32, (SUBLANE, n), 1)
    part = lax.broadcasted_iota(jnp.int32, (SUBLANE, n), 0)
    tvals = jnp.where(part == 0, tok >> TOKEN_SPLIT_BITS, tok & (TOKEN_SPLIT - 1)).astype(F32).astype(BF16)
    sidx = lax.broadcasted_iota(jnp.int32, (capp, n), 0).astype(F32)
    for ei in range(e):
        onehot = jnp.where(slot[ei:ei + 1, :] == sidx, 1.0, 0.0).astype(BF16)
        parts = _dot(tvals, onehot, NT)
        idx_ref[0, ei:ei + 1, :] = (parts[0:1] * TOKEN_SPLIT + parts[1:2]).astype(jnp.int32)


def _topk(aff_t, blk, ntok, cap, capp):
    bsz, e, _ = aff_t.shape
    assert ntok <= TOKEN_SPLIT * 256
    return pl.pallas_call(
        functools.partial(_topk_kernel, cap=cap),
        grid=(bsz,),
        in_specs=[pl.BlockSpec((1, e, ntok), lambda b: (b, 0, blk))],
        out_specs=[pl.BlockSpec((1, e, ntok), lambda b: (b, 0, 0)),
                   pl.BlockSpec((1, ntok, e), lambda b: (b, 0, 0)),
                   pl.BlockSpec((1, ntok, e), lambda b: (b, 0, 0)),
                   pl.BlockSpec((1, e, capp), lambda b: (b, 0, 0))],
        out_shape=[_sds((bsz, e, ntok), F32), _sds((bsz, ntok, e), F32), _sds((bsz, ntok, e), F32),
                   _sds((bsz, e, capp), jnp.int32)],
        scratch_shapes=[pltpu.VMEM((e, ntok), F32), pltpu.VMEM((e, ntok), F32)],
        compiler_params=_cparams("parallel"),
        name="topk",
    )(aff_t)


def _expert_up_kernel(slot_ref, x_ref, wg_ref, wu_ref, o_ref, p_s, g_s, u_s):
    k = pl.program_id(2)
    sps, capp, n = p_s.shape

    @pl.when(k == 0)
    def _():
        sidx = lax.broadcasted_iota(jnp.int32, (capp, n), 0).astype(F32)
        for i in range(sps):
            srow = slot_ref[i, pl.ds(pl.program_id(1), 1), :]
            p_s[i] = jnp.where(srow == sidx, 1.0, 0.0).astype(BF16)
        g_s[...] = jnp.zeros(g_s.shape, F32)
        u_s[...] = jnp.zeros(u_s.shape, F32)

    xg = jnp.concatenate([_dot(p_s[i], x_ref[i]).astype(BF16) for i in range(sps)], axis=0)
    g_s[...] += _dot(xg, wg_ref[0].astype(BF16))
    u_s[...] += _dot(xg, wu_ref[0].astype(BF16))

    @pl.when(k == pl.num_programs(2) - 1)
    def _():
        hid = (jax.nn.silu(g_s[...]) * u_s[...]).astype(o_ref.dtype)
        for i in range(sps):
            o_ref[i, 0] = hid[i * capp:(i + 1) * capp]


def _expert_up(slot, xm, blk, ntok, capp, sps, w_gate, w_up, layer):
    bsz, e = slot.shape[:2]
    d, ff = w_gate.shape[2:]
    tk = 512
    return pl.pallas_call(
        _expert_up_kernel,
        grid=(bsz // sps, e, d // tk),
        in_specs=[pl.BlockSpec((sps, e, ntok), lambda b, ei, k: (b, 0, 0)),
                  pl.BlockSpec((sps, ntok, tk), lambda b, ei, k: (b, blk, k)),
                  pl.BlockSpec((None, 1, tk, ff), lambda b, ei, k: (layer, ei, k, 0)),
                  pl.BlockSpec((None, 1, tk, ff), lambda b, ei, k: (layer, ei, k, 0))],
        out_specs=pl.BlockSpec((sps, 1, capp, ff), lambda b, ei, k: (b, ei, 0, 0)),
        out_shape=_sds((bsz, e, capp, ff), BF16),
        scratch_shapes=[pltpu.VMEM((sps, capp, ntok), BF16), pltpu.VMEM((sps * capp, ff), F32),
                        pltpu.VMEM((sps * capp, ff), F32)],
        compiler_params=_cparams("parallel", "parallel", "arbitrary"),
        name="expert_up",
    )(slot, xm, w_gate, w_up)


def _expert_up_gather_kernel(idx_ref, xs_hbm, m_ref, wg_ref, wu_ref, o_ref, x32_s, xb_s, g_s, u_s, sem, *, off, n_exp):
    b, ei, k = pl.program_id(0), pl.program_id(1), pl.program_id(2)
    cap = x32_s.shape[1]
    tk = wg_ref.shape[1]
    n = b * n_exp + ei
    n_total = pl.num_programs(0) * n_exp

    def row_copy(step, s, row):
        buf = step % 2
        return pltpu.make_async_copy(xs_hbm.at[step // n_exp, pl.ds(row, 1), :], x32_s.at[buf, pl.ds(s, 1), :], sem.at[buf])

    def gather(step):
        def issue(s, carry):
            row_copy(step, s, idx_ref[step, s]).start()
            return carry

        lax.fori_loop(0, cap, issue, 0)

    @pl.when(k == 0)
    def _():
        @pl.when(n == 0)
        def _():
            gather(n)

        @pl.when(n + 1 < n_total)
        def _():
            gather(n + 1)

        def wait(s, carry):
            row_copy(n, s, 0).wait()
            return carry

        lax.fori_loop(0, cap, wait, 0)
        sh = m_ref[0, off:off + 1, :]
        sc = m_ref[0, off + 1:off + 2, :]
        xb_s[...] = (x32_s[n % 2] * (1.0 + sc) + sh).astype(BF16)
        g_s[...] = jnp.zeros(g_s.shape, F32)
        u_s[...] = jnp.zeros(u_s.shape, F32)

    xk = xb_s[:, pl.ds(pl.multiple_of(k * tk, tk), tk)]
    g_s[...] += _dot(xk, wg_ref[0].astype(BF16))
    u_s[...] += _dot(xk, wu_ref[0].astype(BF16))

    @pl.when(k == pl.num_programs(2) - 1)
    def _():
        o_ref[0, 0] = (jax.nn.silu(g_s[...]) * u_s[...]).astype(o_ref.dtype)


def _expert_up_gather(idx, xs, mod6, off, cap, w_gate, w_up, layer):
    bsz, e, _ = idx.shape
    d, ff = w_gate.shape[2:]
    tk = 512
    grid_spec = pltpu.PrefetchScalarGridSpec(
        num_scalar_prefetch=1,
        grid=(bsz, e, d // tk),
        in_specs=[pl.BlockSpec(memory_space=pl.ANY),
                  pl.BlockSpec((1, 6, d), lambda b, ei, k, idx_ref: (b, 0, 0)),
                  pl.BlockSpec((None, 1, tk, ff), lambda b, ei, k, idx_ref: (layer, ei, k, 0)),
                  pl.BlockSpec((None, 1, tk, ff), lambda b, ei, k, idx_ref: (layer, ei, k, 0))],
        out_specs=pl.BlockSpec((1, 1, cap, ff), lambda b, ei, k, idx_ref: (b, ei, 0, 0)),
        scratch_shapes=[pltpu.VMEM((2, cap, d), F32), pltpu.VMEM((cap, d), BF16),
                        pltpu.VMEM((cap, ff), F32), pltpu.VMEM((cap, ff), F32),
                        pltpu.SemaphoreType.DMA((2,))])
    return pl.pallas_call(
        functools.partial(_expert_up_gather_kernel, off=off, n_exp=e),
        grid_spec=grid_spec,
        out_shape=_sds((bsz, e, cap, ff), BF16),
        compiler_params=_cparams("arbitrary", "arbitrary", "arbitrary"),
        name="expert_up_gather",
    )(idx.reshape(bsz * e, idx.shape[-1]), xs, mod6, w_gate, w_up)


def _expert_down_kernel(hid_ref, wd_ref, o_ref, w_s):
    @pl.when(pl.program_id(2) == 0)
    def _():
        w_s[...] = wd_ref[0].astype(BF16)

    o_ref[0, 0] = _dot(hid_ref[0, 0], w_s[...]).astype(o_ref.dtype)


def _expert_down(hid, w_down, layer):
    bsz, e, capp, ff = hid.shape
    d = w_down.shape[-1]
    tn = _pick_tile(d, 2048)
    return pl.pallas_call(
        _expert_down_kernel,
        grid=(e, d // tn, bsz),
        in_specs=[pl.BlockSpec((1, 1, capp, ff), lambda ei, j, b: (b, ei, 0, 0)),
                  pl.BlockSpec((None, 1, ff, tn), lambda ei, j, b: (layer, ei, 0, j))],
        out_specs=pl.BlockSpec((1, 1, capp, tn), lambda ei, j, b: (b, ei, 0, j)),
        out_shape=_sds((bsz, e, capp, d), BF16),
        scratch_shapes=[pltpu.VMEM((ff, tn), BF16)],
        compiler_params=_cparams("parallel", "parallel", "arbitrary"),
        name="expert_down",
    )(hid, w_down)


COMBINE_TN = 512


def _combine_kernel(*refs, nq):
    ye_ref, slot_t_ref, gv_t_ref = refs[:3]
    o_ref = refs[-1]
    q = pl.program_id(1)
    ei = pl.program_id(2)
    n, e = slot_t_ref.shape[1:]
    capp, d = ye_ref.shape[2:]

    @pl.when(ei == 0)
    def _():
        o_ref[...] = jnp.zeros(o_ref.shape, F32)

    @pl.when(q < nq)
    def _():
        pick = (lax.broadcasted_iota(jnp.int32, (e, LANE), 0) == ei).astype(F32)
        slot_b = _dot(slot_t_ref[0], pick, precision=HI)
        gv_b = _dot(gv_t_ref[0], pick, precision=HI)
        lane = lax.broadcasted_iota(jnp.int32, (1, LANE), 1).astype(F32)
        pt = jnp.concatenate([jnp.where(slot_b == lane + float(c * LANE), 1.0, 0.0).astype(BF16)
                              for c in range(capp // LANE)], axis=1)
        tn = COMBINE_TN
        gv = jnp.concatenate([gv_b] * (tn // LANE), axis=1)
        for j in range(d // tn):
            o_ref[0, :, j * tn:(j + 1) * tn] += _dot(pt, ye_ref[0, 0, :, j * tn:(j + 1) * tn]) * gv


def _combine(ye, slot_t, gv_t, tq, blk0, t, prev=None):
    bsz, e, capp, d = ye.shape
    ntok = slot_t.shape[1]
    nq = ntok // tq
    n_steps = nq if prev is not None else pl.cdiv(t, tq) - blk0
    qc = lambda q: jnp.minimum(q, nq - 1)
    in_specs = [pl.BlockSpec((1, 1, capp, d), lambda b, q, ei: (b, jnp.where(q < nq, ei, e - 1), 0, 0)),
                pl.BlockSpec((1, tq, e), lambda b, q, ei: (b, qc(q), 0)),
                pl.BlockSpec((1, tq, e), lambda b, q, ei: (b, qc(q), 0))]
    args = [ye, slot_t, gv_t]
    aliases = {}
    if prev is not None:
        in_specs.append(pl.BlockSpec(memory_space=pl.ANY))
        args.append(prev)
        aliases = {3: 0}
    return pl.pallas_call(
        functools.partial(_combine_kernel, nq=nq),
        grid=(bsz, n_steps, e),
        in_specs=in_specs,
        out_specs=pl.BlockSpec((1, tq, d), lambda b, q, ei: (b, blk0 + q, 0)),
        out_shape=_sds((bsz, t, d), F32),
        input_output_aliases=aliases,
        compiler_params=_cparams("parallel", "parallel", "arbitrary", vmem=VMEM_LIMIT_BIG),
        name="combine",
    )(*args)


def _round_up(x, m):
    return (x + m - 1) // m * m


def _moe_part(aff_t, xm, blk, ntok, sps, tq, t_out, we, layer, prev=None, gather_src=None):
    e = aff_t.shape[1]
    cap = EC_CAPACITY * ntok // e
    capp = _round_up(cap, LANE)
    slot, slot_t, gv_t, idx = _topk(aff_t, blk, ntok, cap, capp)
    if gather_src is not None:
        assert blk == 0 and cap == capp
        hid = _expert_up_gather(idx, *gather_src, cap, we["gate"], we["up"], layer)
    else:
        hid = _expert_up(slot, xm, blk, ntok, capp, sps, we["gate"], we["up"], layer)
    ye = _expert_down(hid, we["down"], layer)
    return _combine(ye, slot_t, gv_t, tq, blk * ntok // tq, t_out, prev)


def kernel(x, c, ctx, c_ctx, w_mod, b_mod, w_in, b_in, conv_a_w, conv_a_b, lru_wa, lru_ba, lru_wx, lru_bx, lru_lam, gla_wa2, gla_ba, gla_norm_g, conv_c_w, conv_c_b, mlstm_norm_g, w_branch, w_out, ln1_g, ln1_b, w_router, w_e_gate, w_e_up, w_e_down, ln2_g, ln2_b):
    bsz, n_lat, d = x.shape
    nc = ctx.shape[1]
    t = n_lat + nc
    depth = w_mod.shape[0]
    d_rnn = conv_a_w.shape[-1]
    kg, vg = gla_ba.shape[-1], gla_norm_g.shape[-1]
    rank = gla_wa2.shape[2]
    wm = mlstm_norm_g.shape[-1]
    n_gate = 4 * MLSTM_HEADS
    n_exp = w_router.shape[-1]
    alpha = (2 * depth) ** 0.25
    assert 2 * rank + n_gate <= LANE and t % ROW_TILE == 0 and n_lat % ROW_TILE == 0 and bsz < SUBLANE

    sizes = (d_rnn, d_rnn, kg, kg, vg, vg, 2 * rank, wm, wm, wm, wm, n_gate)
    offs = [0]
    for sz in sizes:
        offs.append(offs[-1] + sz)
    n_feat = offs[-1]
    take_main = lambda a: jnp.concatenate([a[..., offs[0]:offs[6]], a[..., offs[7]:offs[11]]], axis=-1)
    take_small = lambda a: jnp.concatenate([a[..., offs[6]:offs[7]], a[..., offs[11]:offs[12]]], axis=-1)
    n_main = (offs[6] - offs[0]) + (offs[11] - offs[7])
    pad_s = LANE - (2 * rank + n_gate)
    mo = {"a_x": 0, "a_g": d_rnn, "b_q": 2 * d_rnn, "b_k": 2 * d_rnn + kg, "b_v": 2 * d_rnn + 2 * kg,
          "b_r": 2 * d_rnn + 2 * kg + vg}
    mo["c_q"] = mo["b_r"] + vg
    mo["c_k"], mo["c_v"], mo["c_o"] = mo["c_q"] + wm, mo["c_q"] + 2 * wm, mo["c_q"] + 3 * wm

    cvec = jnp.concatenate([c, c_ctx[None], jnp.zeros((SUBLANE - bsz - 1, d), F32)], axis=0)
    mod = _mod_all(cvec, w_mod, b_mod)
    xs = jnp.concatenate([x, ctx], axis=1)

    tc = LANE
    while tc < 512 and all(v % (2 * tc) == 0 for v in (offs[6], 4 * wm, d)):
        tc *= 2
    w_in_t = jnp.swapaxes(w_in, 1, 2)
    w_main = _wprep(w_in_t, tc, 0, n_main // tc, ((0, offs[6] // tc, 0), (offs[6] // tc, n_main // tc, offs[7] - offs[6])),
                    "wprep_main")
    w_gate = _wprep(w_in_t, tc, n_main // tc, 3 * d // tc, ((0, 3 * d // tc, n_feat - n_main),), "wprep_gate")
    w_small = jnp.pad(take_small(w_in), ((0, 0), (0, 0), (0, pad_s)))
    w_br = w_branch.astype(BF16)
    w_rt = jnp.swapaxes(w_router, 1, 2).astype(BF16)
    we = {"gate": w_e_gate, "up": w_e_up, "down": w_e_down}

    mod6s = [mod[l].reshape(SUBLANE, 6, d) for l in range(depth)]
    h = _modulate(xs, mod6s[0], 0, n_lat)
    for l in range(depth):
        last = l == depth - 1
        mod6 = mod6s[l]
        b_main = take_main(b_in[l]).reshape(1, n_main)
        b_small = jnp.pad(take_small(b_in[l]), (0, pad_s)).reshape(1, LANE)
        bg = b_in[l][n_feat:].reshape(3, 1, d)
        lru_p = {"conv_w": conv_a_w[l], "conv_b": conv_a_b[l].reshape(1, d_rnn),
                 "wa": lru_wa[l].astype(BF16), "ba": lru_ba[l].reshape(2, 1, d_rnn),
                 "wx": lru_wx[l].astype(BF16), "bx": lru_bx[l].reshape(2, 1, d_rnn),
                 "lam": lru_lam[l].reshape(2, 1, d_rnn)}
        wlr = jnp.zeros((2, LANE, kg), F32)
        for dd in range(2):
            wlr = wlr.at[dd, dd * rank:(dd + 1) * rank].set(gla_wa2[l, dd])
        gla_p = {"wlr": wlr, "ba": gla_ba[l].reshape(2, 1, kg), "norm_g": gla_norm_g[l].reshape(1, vg)}
        ml_p = {"conv_w": conv_c_w[l], "conv_b": conv_c_b[l].reshape(1, 2 * wm), "norm_g": mlstm_norm_g[l].reshape(1, wm)}

        h = h.reshape(bsz * t, d)
        feat = _matmul(h, w_main, l, b_main, F32, 512, 1024, "feat").reshape(bsz, t, n_main)
        feat_s = _matmul(h, w_small, l, b_small, F32, 1024, LANE, "feat_small").reshape(bsz, t, LANE)
        y0 = _rglru(feat, lru_p, d_rnn, n_lat)
        y1 = _gla(feat, feat_s, gla_p, (mo["b_q"], mo["b_k"], mo["b_v"], mo["b_r"]), n_lat)
        y2 = _mlstm(feat, feat_s, ml_p, (mo["c_q"], mo["c_k"], mo["c_v"], mo["c_o"]), 2 * rank, n_lat)
        ys = [y.reshape(bsz * t, -1) for y in (y0, y1, y2)]
        merged = _merge(h, w_gate, l, bg, ys, w_br)
        y = _matmul(merged, w_out, l, jnp.zeros((1, d), F32), F32, 512, 512, "out_proj").reshape(bsz, t, d)
        xs, xm = _res_ln(xs, y, mod6, 2, ln1_g[l], ln1_b[l], alpha, n_lat, t, mod6, 3)

        aff_t = _router(xm, w_rt[l])
        t_out = n_lat if last else t
        f = _moe_part(aff_t, xm, 0, n_lat, 1, _pick_tile(n_lat, 1024), t_out, we, l, gather_src=(xs, mod6, 3))
        if not last:
            f = _moe_part(aff_t, xm, n_lat // nc, nc, bsz, nc, t_out, we, l, prev=f)
        if last:
            xs = _res_ln(xs, f, mod6, 5, ln2_g[l], ln2_b[l], alpha, n_lat, t_out)
        else:
            xs, h = _res_ln(xs, f, mod6, 5, ln2_g[l], ln2_b[l], alpha, n_lat, t_out, mod6s[l + 1], 0)
    return xs
```

```python
import functools

import jax
import jax.numpy as jnp
from jax import lax
from jax.experimental import pallas as pl
from jax.experimental.pallas import tpu as pltpu

F32 = jnp.float32
BF16 = jnp.bfloat16
HI = lax.Precision.HIGHEST

GRID_W = 64
CHUNK = 64
LRU_C = 8.0
GLA_HEADS = 4
GLA_TAU = 16.0
MLSTM_HEADS = 4
EC_CAPACITY = 2
LN_EPS = 1e-5

LANE = 128
SUBLANE = 8
ROW_TILE = 256
VMEM_LIMIT = 48 << 20
VMEM_LIMIT_BIG = 56 << 20

NT = (((1,), (1,)), ((), ()))
TN = (((0,), (0,)), ((), ()))


def _cparams(*sem, vmem=VMEM_LIMIT):
    return pltpu.CompilerParams(dimension_semantics=sem, vmem_limit_bytes=vmem)


def _sds(shape, dtype):
    return jax.ShapeDtypeStruct(shape, dtype)


def _dot(a, b, dims=None, precision=None):
    if dims is None:
        return jnp.dot(a, b, preferred_element_type=F32, precision=precision)
    return lax.dot_general(a, b, dims, preferred_element_type=F32, precision=precision)


def _mod_kernel(c_ref, w_ref, b_ref, o_ref):
    a = jax.nn.silu(c_ref[...]).astype(BF16)
    o_ref[0] = _dot(a, w_ref[0].astype(BF16)) + b_ref[0]


def _mod_all(cvec, w_mod, b_mod):
    depth, d, n6 = w_mod.shape
    tn = 512
    return pl.pallas_call(
        _mod_kernel,
        grid=(depth, n6 // tn),
        in_specs=[pl.BlockSpec((SUBLANE, d), lambda l, j: (0, 0)),
                  pl.BlockSpec((1, d, tn), lambda l, j: (l, 0, j)),
                  pl.BlockSpec((1, 1, tn), lambda l, j: (l, 0, j))],
        out_specs=pl.BlockSpec((1, SUBLANE, tn), lambda l, j: (l, 0, j)),
        out_shape=_sds((depth, SUBLANE, n6), F32),
        compiler_params=_cparams("parallel", "parallel"),
        name="mod",
    )(cvec, w_mod, b_mod.reshape(depth, 1, n6))


def _modulate_kernel(x_ref, m_ref, o_ref, *, off):
    sh = m_ref[0, off:off + 1, :]
    sc = m_ref[0, off + 1:off + 2, :]
    o_ref[0] = (x_ref[0] * (1.0 + sc) + sh).astype(o_ref.dtype)


def _mod_row_map(n_lat_tiles, batch):
    return lambda b, i: (jnp.where(i < n_lat_tiles, b, batch), 0, 0)


def _modulate(xs, mod6, off, n_lat):
    batch, t, d = xs.shape
    tr = ROW_TILE
    return pl.pallas_call(
        functools.partial(_modulate_kernel, off=off),
        grid=(batch, t // tr),
        in_specs=[pl.BlockSpec((1, tr, d), lambda b, i: (b, i, 0)),
                  pl.BlockSpec((1, 6, d), _mod_row_map(n_lat // tr, batch))],
        out_specs=pl.BlockSpec((1, tr, d), lambda b, i: (b, i, 0)),
        out_shape=_sds((batch, t, d), BF16),
        compiler_params=_cparams("parallel", "parallel"),
        name="modulate",
    )(xs, mod6)


def _mm_kernel(a_ref, w_ref, b_ref, o_ref):
    o_ref[...] = (_dot(a_ref[...], w_ref[...]) + b_ref[...]).astype(o_ref.dtype)


def _pick_tile(n, pref):
    while pref > LANE and n % pref:
        pref //= 2
    assert n % pref == 0
    return pref


def _mm_castw_kernel(a_ref, w_ref, b_ref, o_ref, w_s):
    @pl.when(pl.program_id(1) == 0)
    def _():
        w_s[...] = w_ref[...].astype(BF16)

    o_ref[...] = (_dot(a_ref[...], w_s[...]) + b_ref[...]).astype(o_ref.dtype)


def _matmul(a, w, layer, bias, out_dtype, tm, tn, name):
    m, k = a.shape
    n = w.shape[2]
    tm, tn = _pick_tile(m, tm), _pick_tile(n, tn)
    cast = w.dtype == F32
    return pl.pallas_call(
        _mm_castw_kernel if cast else _mm_kernel,
        grid=(n // tn, m // tm),
        in_specs=[pl.BlockSpec((tm, k), lambda j, i: (i, 0)),
                  pl.BlockSpec((None, k, tn), lambda j, i: (layer, 0, j)),
                  pl.BlockSpec((1, tn), lambda j, i: (0, j))],
        out_specs=pl.BlockSpec((tm, tn), lambda j, i: (i, j)),
        out_shape=_sds((m, n), out_dtype),
        scratch_shapes=[pltpu.VMEM((k, tn), BF16)] if cast else [],
        compiler_params=_cparams("parallel", "arbitrary" if cast else "parallel"),
        name=name,
    )(a, w, bias)


def _wprep_kernel(a_ref, b_ref, o_ref, *, regions):
    j = pl.program_id(2)
    tc = o_ref.shape[-1]
    for lo, hi, shift in regions:
        @pl.when(jnp.logical_and(j >= lo, j < hi))
        def _():
            if shift == 0:
                rows = a_ref[...]
            else:
                rows = jnp.concatenate([a_ref[...], b_ref[:LANE]], axis=0)[shift:shift + tc]
            o_ref[...] = rows.T.astype(BF16)


def _wprep(w_t, tc, base_blk, n_blk, regions, name):
    depth, _, k = w_t.shape
    assert all(0 <= s < LANE and s % SUBLANE == 0 for _, _, s in regions)
    tr = _pick_tile(k, 1024)
    return pl.pallas_call(
        functools.partial(_wprep_kernel, regions=regions),
        grid=(depth, k // tr, n_blk),
        in_specs=[pl.BlockSpec((None, tc, tr), lambda l, i, j: (l, base_blk + j, i)),
                  pl.BlockSpec((None, tc, tr), lambda l, i, j: (l, base_blk + j + 1, i))],
        out_specs=pl.BlockSpec((None, tr, tc), lambda l, i, j: (l, i, j)),
        out_shape=_sds((depth, k, n_blk * tc), BF16),
        compiler_params=_cparams("parallel", "parallel", "parallel"),
        name=name,
    )(w_t, w_t)


def _res_ln_kernel(*refs, off, alpha, next_off):
    if next_off is None:
        x_ref, y_ref, m_ref, g_ref, b_ref, o_ref = refs
    else:
        x_ref, y_ref, m_ref, g_ref, b_ref, mn_ref, o_ref, on_ref = refs
    gate = m_ref[0, off:off + 1, :]
    z = alpha * x_ref[0] + gate * y_ref[0]
    mu = jnp.mean(z, axis=-1, keepdims=True)
    zc = z - mu
    var = jnp.mean(zc * zc, axis=-1, keepdims=True)
    out = zc * lax.rsqrt(var + LN_EPS) * g_ref[...] + b_ref[...]
    o_ref[0] = out
    if next_off is not None:
        sh = mn_ref[0, next_off:next_off + 1, :]
        sc = mn_ref[0, next_off + 1:next_off + 2, :]
        on_ref[0] = (out * (1.0 + sc) + sh).astype(on_ref.dtype)


def _res_ln(xs, y, mod6, off, ln_g, ln_b, alpha, n_lat, t_out, next_mod6=None, next_off=None):
    batch, t, d = xs.shape
    tr = ROW_TILE
    row = pl.BlockSpec((1, tr, d), lambda b, i: (b, i, 0))
    modspec = pl.BlockSpec((1, 6, d), _mod_row_map(n_lat // tr, batch))
    vec = pl.BlockSpec((1, d), lambda b, i: (0, 0))
    in_specs = [row, row, modspec, vec, vec]
    args = [xs, y, mod6, ln_g.reshape(1, d), ln_b.reshape(1, d)]
    out_specs, out_shape = row, _sds((batch, t_out, d), F32)
    if next_off is not None:
        in_specs.append(modspec)
        args.append(next_mod6)
        out_specs, out_shape = [row, row], [out_shape, _sds((batch, t_out, d), BF16)]
    return pl.pallas_call(
        functools.partial(_res_ln_kernel, off=off, alpha=alpha, next_off=next_off),
        grid=(batch, t_out // tr),
        in_specs=in_specs,
        out_specs=out_specs,
        out_shape=out_shape,
        compiler_params=_cparams("parallel", "parallel"),
        name="res_ln",
    )(*args)


def _gelu_tanh(x):
    return jax.nn.gelu(x, approximate=True)


def _lru_gates(u2, wa_ref, ba_ref, wx_ref, bx_ref, lam_ref):
    ub = u2.astype(BF16)
    r = jax.nn.sigmoid(_dot(ub, wa_ref[0]) + ba_ref[...])
    i = jax.nn.sigmoid(_dot(ub, wx_ref[0]) + bx_ref[...])
    log_a = (-LRU_C * jax.nn.softplus(-lam_ref[...])) * r
    a = jnp.exp(log_a)
    bt = jnp.sqrt(1.0 - jnp.exp(2.0 * log_a)) * (i * u2)
    return a, bt


def _lru_lat_kernel(*refs, rev, n_cg):
    if rev:
        (x_ref, pv_ref, nx_ref, cw_ref, cb_ref, wa_ref, ba_ref, wx_ref, bx_ref, lam_ref, e0_ref,
         ag_ref, hf_ref, _alias, o_ref, a_s, b_s, carry) = refs
    else:
        (x_ref, pv_ref, nx_ref, cw_ref, cb_ref, wa_ref, ba_ref, wx_ref, bx_ref, lam_ref, e0_ref,
         _alias, o_ref, a_s, b_s, carry) = refs
    s = pl.program_id(1)
    cg = (n_cg - 1 - s) if rev else s
    bsz, rows, ncol, cb = x_ref.shape

    @pl.when(s == 0)
    def _():
        carry[...] = e0_ref[...]

    x = x_ref[...]
    col = lax.broadcasted_iota(jnp.int32, (1, ncol, 1), 1)
    not_first = (cg > 0).astype(F32)
    not_last = (cg < n_cg - 1).astype(F32)
    top = jnp.where(col == 0, pltpu.roll(pv_ref[:, SUBLANE - 1], 1, 1) * not_first, pltpu.roll(x[:, rows - 1], 1, 1))
    bot1 = jnp.where(col == ncol - 1, pltpu.roll(nx_ref[:, 0], ncol - 1, 1) * not_last, pltpu.roll(x[:, 0], ncol - 1, 1))
    bot2 = jnp.where(col == ncol - 1, pltpu.roll(nx_ref[:, 1], ncol - 1, 1) * not_last, pltpu.roll(x[:, 1], ncol - 1, 1))
    xe = jnp.concatenate([top[:, None], x, bot1[:, None], bot2[:, None]], axis=1)
    u = cb_ref[...].reshape(1, 1, 1, cb)
    for k in range(4):
        u = u + cw_ref[k:k + 1, :].reshape(1, 1, 1, cb) * xe[:, k:k + rows]
    a, bt = _lru_gates(u.reshape(bsz * rows * ncol, cb), wa_ref, ba_ref, wx_ref, bx_ref, lam_ref)
    a_s[...] = a.reshape(bsz, rows, ncol, cb)
    b_s[...] = bt.reshape(bsz, rows, ncol, cb)

    def body(t, hp):
        h, p = hp
        r = (rows - 1 - t) if rev else t
        a_t = a_s[:, r]
        h = a_t * h + b_s[:, r]
        p = p * a_t
        b_s[:, r] = h
        a_s[:, r] = p
        return h, p

    h_end, p_end = lax.fori_loop(0, rows, body, (jnp.zeros((bsz, ncol, cb), F32), jnp.ones((bsz, ncol, cb), F32)), unroll=8)
    av, bv = p_end, h_end
    sh = 1
    while sh < ncol:
        if rev:
            valid = col < ncol - sh
            amt = ncol - sh
        else:
            valid = col >= sh
            amt = sh
        b_sh = jnp.where(valid, pltpu.roll(bv, amt, 1), 0.0)
        a_sh = jnp.where(valid, pltpu.roll(av, amt, 1), 1.0)
        bv = bv + av * b_sh
        av = av * a_sh
        sh *= 2
    e_prev = carry[...]
    e = bv + av * e_prev
    if rev:
        c_in = jnp.where(col == ncol - 1, e_prev, pltpu.roll(e, ncol - 1, 1))
        carry[...] = jnp.broadcast_to(e[:, 0:1], e.shape)
    else:
        c_in = jnp.where(col == 0, e_prev, pltpu.roll(e, 1, 1))
        carry[...] = jnp.broadcast_to(e[:, ncol - 1:ncol], e.shape)
    h = b_s[...] + a_s[...] * c_in[:, None]
    if rev:
        o_ref[...] = (hf_ref[...] + h) * _gelu_tanh(ag_ref[...])
    else:
        o_ref[...] = h


def _lru_ctx_kernel(*refs, rev):
    if rev:
        (x_ref, cw_ref, cb_ref, wa_ref, ba_ref, wx_ref, bx_ref, lam_ref, ag_ref, hf_ref, _base, o_ref, e_ref) = refs
    else:
        (x_ref, cw_ref, cb_ref, wa_ref, ba_ref, wx_ref, bx_ref, lam_ref, _base, o_ref, e_ref) = refs
    bsz, nc, cb = x_ref.shape
    x = x_ref[...]
    t = lax.broadcasted_iota(jnp.int32, (1, nc, 1), 1)
    xm1 = jnp.where(t >= 1, pltpu.roll(x, 1, 1), 0.0)
    xp1 = jnp.where(t < nc - 1, pltpu.roll(x, nc - 1, 1), 0.0)
    xp2 = jnp.where(t < nc - 2, pltpu.roll(x, nc - 2, 1), 0.0)
    w = [cw_ref[k:k + 1, :].reshape(1, 1, cb) for k in range(4)]
    u = w[0] * xm1 + w[1] * x + w[2] * xp1 + w[3] * xp2 + cb_ref[...].reshape(1, 1, cb)
    a, bt = _lru_gates(u.reshape(bsz * nc, cb), wa_ref, ba_ref, wx_ref, bx_ref, lam_ref)
    av = a.reshape(bsz, nc, cb)
    bv = bt.reshape(bsz, nc, cb)
    sh = 1
    while sh < nc:
        if rev:
            valid = t < nc - sh
            amt = nc - sh
        else:
            valid = t >= sh
            amt = sh
        b_sh = jnp.where(valid, pltpu.roll(bv, amt, 1), 0.0)
        a_sh = jnp.where(valid, pltpu.roll(av, amt, 1), 1.0)
        bv = bv + av * b_sh
        av = av * a_sh
        sh *= 2
    if rev:
        o_ref[...] = (hf_ref[...] + bv) * _gelu_tanh(ag_ref[...])
        e_ref[...] = jnp.broadcast_to(bv[:, 0:1], (bsz, SUBLANE, cb))
    else:
        o_ref[...] = bv
        e_ref[...] = jnp.broadcast_to(bv[:, nc - 1:nc], (bsz, SUBLANE, cb))


def _rglru(feat, p, d_rnn, n_lat):
    bsz, t, nf = feat.shape
    nc = t - n_lat
    nb, bs, _ = p["wa"][0].shape
    cb = bs
    assert d_rnn == nb * bs and cb % LANE == 0 and n_lat % nc == 0 and nc % GRID_W == 0
    rows = n_lat // GRID_W
    assert rows % SUBLANE == 0 and GRID_W % SUBLANE == 0
    n_cg = GRID_W // SUBLANE
    ag_off = d_rnn // cb
    feat4 = feat.reshape(bsz, t // GRID_W, GRID_W, nf)
    ctx_blk = n_lat // nc

    def wspecs(im):
        return [pl.BlockSpec((4, cb), im(lambda j: (0, j))),
                pl.BlockSpec((1, cb), im(lambda j: (0, j))),
                pl.BlockSpec((1, bs, bs), im(lambda j: (j, 0, 0))),
                pl.BlockSpec((1, cb), im(lambda j: (0, j))),
                pl.BlockSpec((1, bs, bs), im(lambda j: (j, 0, 0))),
                pl.BlockSpec((1, cb), im(lambda j: (0, j))),
                pl.BlockSpec((1, cb), im(lambda j: (0, j)))]

    def wargs(d):
        return [p["conv_w"], p["conv_b"], p["wa"][d], p["ba"][d], p["wx"][d], p["bx"][d], p["lam"][d]]

    im1 = lambda f: (lambda j: f(j))
    im2 = lambda f: (lambda j, s: f(j))
    hf = None
    out = None
    for rev in (False, True):
        in_specs = [pl.BlockSpec((bsz, nc, cb), lambda j: (0, ctx_blk, j))] + wspecs(im1)
        args = [feat] + wargs(int(rev))
        if rev:
            in_specs += [pl.BlockSpec((bsz, nc, cb), lambda j: (0, ctx_blk, ag_off + j)),
                         pl.BlockSpec((bsz, nc, cb), lambda j: (0, ctx_blk, j))]
            args += [feat, hf]
        in_specs += [pl.BlockSpec(memory_space=pl.ANY)]
        args += [jnp.zeros((bsz, t, d_rnn), F32)]
        part, e0 = pl.pallas_call(
            functools.partial(_lru_ctx_kernel, rev=rev),
            grid=(nb,),
            in_specs=in_specs,
            out_specs=[pl.BlockSpec((bsz, nc, cb), lambda j: (0, ctx_blk, j)),
                       pl.BlockSpec((bsz, SUBLANE, cb), lambda j: (0, 0, j))],
            out_shape=[_sds((bsz, t, d_rnn), F32), _sds((bsz, SUBLANE, d_rnn), F32)],
            input_output_aliases={len(args) - 1: 0},
            compiler_params=_cparams("parallel"),
            name="lru_ctx_bwd" if rev else "lru_ctx_fwd",
        )(*args)
        cgm = (lambda s: n_cg - 1 - s) if rev else (lambda s: s)
        in_specs = [pl.BlockSpec((bsz, rows, SUBLANE, cb), lambda j, s: (0, 0, cgm(s), j)),
                    pl.BlockSpec((bsz, SUBLANE, SUBLANE, cb), lambda j, s: (0, rows // SUBLANE - 1, jnp.maximum(cgm(s) - 1, 0), j)),
                    pl.BlockSpec((bsz, SUBLANE, SUBLANE, cb), lambda j, s: (0, 0, jnp.minimum(cgm(s) + 1, n_cg - 1), j))]
        in_specs += wspecs(im2) + [pl.BlockSpec((bsz, SUBLANE, cb), lambda j, s: (0, 0, j))]
        args = [feat4, feat4, feat4] + wargs(int(rev)) + [e0]
        if rev:
            in_specs += [pl.BlockSpec((bsz, rows, SUBLANE, cb), lambda j, s: (0, 0, cgm(s), ag_off + j)),
                         pl.BlockSpec((bsz, rows, SUBLANE, cb), lambda j, s: (0, 0, cgm(s), j))]
            args += [feat4, hf.reshape(bsz, t // GRID_W, GRID_W, d_rnn)]
        in_specs += [pl.BlockSpec(memory_space=pl.ANY)]
        args += [part.reshape(bsz, t // GRID_W, GRID_W, d_rnn)]
        res = pl.pallas_call(
            functools.partial(_lru_lat_kernel, rev=rev, n_cg=n_cg),
            grid=(nb, n_cg),
            in_specs=in_specs,
            out_specs=pl.BlockSpec((bsz, rows, SUBLANE, cb), lambda j, s: (0, 0, cgm(s), j)),
            out_shape=_sds((bsz, t // GRID_W, GRID_W, d_rnn), F32),
            scratch_shapes=[pltpu.VMEM((bsz, rows, SUBLANE, cb), F32),
                            pltpu.VMEM((bsz, rows, SUBLANE, cb), F32),
                            pltpu.VMEM((bsz, SUBLANE, cb), F32)],
            input_output_aliases={len(args) - 1: 0},
            compiler_params=_cparams("parallel", "arbitrary"),
            name="lru_lat_bwd" if rev else "lru_lat_fwd",
        )(*args)
        res = res.reshape(bsz, t, d_rnn)
        if rev:
            out = res
        else:
            hf = res
    return out


def _chunk_of(s, n_ch, n_lat_ch, rev):
    if rev:
        return n_ch - 1 - s
    return jnp.where(s < n_ch - n_lat_ch, s + n_lat_ch, s - (n_ch - n_lat_ch))


MIXER_SPS = 4


def _mixer_sps(bsz):
    sps = MIXER_SPS
    while bsz % sps:
        sps //= 2
    return sps


def _head_norm(o, g):
    mu = jnp.mean(o, axis=-1, keepdims=True)
    oc = o - mu
    var = jnp.mean(oc * oc, axis=-1, keepdims=True)
    return oc * lax.rsqrt(var + LN_EPS) * g


def _gla_kernel(*refs, rev, heads):
    if rev:
        (q_ref, k_ref, v_ref, lr_ref, wlr_ref, ba_ref, tri_ref, rsel_ref, r_ref, of_ref, ng_ref, o_ref, st) = refs
    else:
        (q_ref, k_ref, v_ref, lr_ref, wlr_ref, ba_ref, tri_ref, rsel_ref, o_ref, st) = refs
    s = pl.program_id(1)

    @pl.when(s == 0)
    def _():
        st[...] = jnp.zeros(st.shape, F32)

    dk = q_ref.shape[-1] // heads
    dv = v_ref.shape[-1] // heads
    tri = tri_ref[...]
    for i in range(q_ref.shape[0]):
        g = jax.nn.log_sigmoid(_dot(lr_ref[i], wlr_ref[...], precision=HI) + ba_ref[...]) / GLA_TAU
        bcum = _dot(tri, g, precision=HI)
        bm = _dot(rsel_ref[...], g, precision=HI)
        q = q_ref[i] * dk ** -0.5
        k = k_ref[i]
        v = v_ref[i]
        for h in range(heads):
            sk = slice(h * dk, (h + 1) * dk)
            sv = slice(h * dv, (h + 1) * dv)
            qh, kh, bh = q[:, sk], k[:, sk], bcum[:, sk]
            vh = v[:, sv].astype(BF16)
            bmid, blast = bm[0:1, sk], bm[1:2, sk]
            att = _dot((qh * jnp.exp(bh - bmid)).astype(BF16), (kh * jnp.exp(bmid - bh)).astype(BF16), NT) * tri
            sth = st[i, h]
            o = _dot(att.astype(BF16), vh) + _dot((qh * jnp.exp(bh)).astype(BF16), sth.astype(BF16), NT)
            st[i, h] = jnp.exp(blast) * sth + _dot(vh, (kh * jnp.exp(blast - bh)).astype(BF16), TN)
            if rev:
                o = _head_norm(of_ref[i, :, sv] + o, ng_ref[:, sv]) * jax.nn.silu(r_ref[i, :, sv])
            o_ref[i, :, sv] = o.astype(o_ref.dtype)


def _scan_consts(rev):
    i = jnp.arange(CHUNK)
    tri = (i[None, :] >= i[:, None]) if rev else (i[None, :] <= i[:, None])
    tri = tri.astype(F32)
    mid = CHUNK // 2 if rev else CHUNK // 2 - 1
    last = 0 if rev else CHUNK - 1
    rsel = jnp.zeros((SUBLANE, CHUNK), F32).at[0].set(tri[mid]).at[1].set(tri[last])
    return tri, rsel


def _gla(feat, feat_s, p, offs, n_lat):
    bsz, t, _ = feat.shape
    kg, vg = p["ba"][0].shape[-1], p["norm_g"].shape[-1]
    oq, ok, ov, orr = offs
    assert oq % kg == 0 and ok % kg == 0 and ov % vg == 0 and orr % vg == 0
    n_ch, n_lat_ch = t // CHUNK, n_lat // CHUNK
    sps = _mixer_sps(bsz)
    of = None
    for rev in (False, True):
        d = int(rev)
        tri, rsel = _scan_consts(rev)
        cm = lambda b, s: (b, _chunk_of(s, n_ch, n_lat_ch, rev))
        const = lambda b, s: (0, 0)
        in_specs = [pl.BlockSpec((sps, CHUNK, kg), lambda b, s: cm(b, s) + (oq // kg,)),
                    pl.BlockSpec((sps, CHUNK, kg), lambda b, s: cm(b, s) + (ok // kg,)),
                    pl.BlockSpec((sps, CHUNK, vg), lambda b, s: cm(b, s) + (ov // vg,)),
                    pl.BlockSpec((sps, CHUNK, LANE), lambda b, s: cm(b, s) + (0,)),
                    pl.BlockSpec((LANE, kg), const),
                    pl.BlockSpec((1, kg), const),
                    pl.BlockSpec((CHUNK, CHUNK), const),
                    pl.BlockSpec((SUBLANE, CHUNK), const)]
        args = [feat, feat, feat, feat_s, p["wlr"][d], p["ba"][d], tri, rsel]
        if rev:
            in_specs += [pl.BlockSpec((sps, CHUNK, vg), lambda b, s: cm(b, s) + (orr // vg,)),
                         pl.BlockSpec((sps, CHUNK, vg), lambda b, s: cm(b, s) + (0,)),
                         pl.BlockSpec((1, vg), const)]
            args += [feat, of, p["norm_g"]]
        res = pl.pallas_call(
            functools.partial(_gla_kernel, rev=rev, heads=GLA_HEADS),
            grid=(bsz // sps, n_ch),
            in_specs=in_specs,
            out_specs=pl.BlockSpec((sps, CHUNK, vg), lambda b, s: cm(b, s) + (0,)),
            out_shape=_sds((bsz, t, vg), BF16 if rev else F32),
            scratch_shapes=[pltpu.VMEM((sps, GLA_HEADS, vg // GLA_HEADS, kg // GLA_HEADS), F32)],
            compiler_params=_cparams("parallel", "arbitrary"),
            name="gla_bwd" if rev else "gla_fwd",
        )(*args)
        of = res
    return of


def _conv_rows(x, prev_row, next_rows, w_ref, b_ref, lo, hi):
    n = x.shape[0]
    t = lax.broadcasted_iota(jnp.int32, (n, 1), 0)
    xm1 = jnp.where(t == 0, prev_row, pltpu.roll(x, 1, 0))
    xp1 = jnp.where(t == n - 1, next_rows[0:1], pltpu.roll(x, n - 1, 0))
    xp2 = jnp.where(t == n - 2, next_rows[0:1], jnp.where(t == n - 1, next_rows[1:2], pltpu.roll(x, n - 2, 0)))
    w = w_ref[:, lo:hi]
    return w[0:1] * xm1 + w[1:2] * x + w[2:3] * xp1 + w[3:4] * xp2 + b_ref[:, lo:hi]


def _mlstm_kernel(*refs, rev, heads, n_ch, n_lat_ch, g_off):
    if rev:
        (q_ref, qp_ref, qn_ref, k_ref, kp_ref, kn_ref, v_ref, cg_ref, cw_ref, cbias_ref, tri_ref, trit_ref,
         og_ref, hf_ref, ng_ref, o_ref, c_s, n_s, m_s) = refs
    else:
        (q_ref, qp_ref, qn_ref, k_ref, kp_ref, kn_ref, v_ref, cg_ref, cw_ref, cbias_ref, tri_ref, trit_ref,
         o_ref, c_s, n_s, m_s) = refs
    s = pl.program_id(1)

    @pl.when(s == 0)
    def _():
        c_s[...] = jnp.zeros(c_s.shape, F32)
        n_s[...] = jnp.zeros(n_s.shape, F32)
        m_s[...] = jnp.zeros(m_s.shape, F32)

    ch = _chunk_of(s, n_ch, n_lat_ch, rev)
    not_first = jnp.logical_and(ch != 0, ch != n_lat_ch).astype(F32)
    not_last = jnp.logical_and(ch != n_lat_ch - 1, ch != n_ch - 1).astype(F32)
    wm = q_ref.shape[-1]
    dh = wm // heads
    tri = tri_ref[...]
    ci0 = g_off + int(rev) * 2 * heads
    cf0 = ci0 + heads
    lane = lax.broadcasted_iota(jnp.int32, (SUBLANE, LANE), 1)
    row = lax.broadcasted_iota(jnp.int32, (SUBLANE, LANE), 0)
    sel_i = (lane == row + ci0).astype(F32)
    sel_f = (lane == row + cf0).astype(F32)
    last = 0 if rev else CHUNK - 1
    for i in range(q_ref.shape[0]):
        qc = jax.nn.silu(_conv_rows(q_ref[i], qp_ref[i, SUBLANE - 1:SUBLANE] * not_first, qn_ref[i, 0:2] * not_last,
                                    cw_ref, cbias_ref, 0, wm))
        kc = jax.nn.silu(_conv_rows(k_ref[i], kp_ref[i, SUBLANE - 1:SUBLANE] * not_first, kn_ref[i, 0:2] * not_last,
                                    cw_ref, cbias_ref, wm, 2 * wm)) * dh ** -0.5
        v = v_ref[i]
        gts = cg_ref[i]
        gls = jax.nn.log_sigmoid(gts)
        bcol_all = _dot(tri, gls, precision=HI)
        ig_rows = _dot(sel_i, gts, NT, precision=HI)
        b_rows = _dot(_dot(sel_f, gls, NT, precision=HI), trit_ref[...], precision=HI)
        for h in range(heads):
            sl = slice(h * dh, (h + 1) * dh)
            qh = qc[:, sl]
            qb, kb, vh = qh.astype(BF16), kc[:, sl].astype(BF16), v[:, sl]
            bc = bcol_all[:, cf0 + h:cf0 + h + 1]
            igc = gts[:, ci0 + h:ci0 + h + 1]
            m = m_s[i, h:h + 1, 0:1]
            dmat = jnp.where(tri > 0.0, bc - b_rows[h:h + 1, :] + ig_rows[h:h + 1, :], -jnp.inf)
            inter = bc + m
            m_row = jnp.maximum(jnp.max(dmat, axis=-1, keepdims=True), inter)
            pmat = _dot(qb, kb, NT) * jnp.exp(dmat - m_row)
            s_inter = jnp.exp(inter - m_row)
            cm = c_s[i, h]
            nv = n_s[i, h:h + 1, :]
            num = _dot(pmat.astype(BF16), vh.astype(BF16)) + s_inter * _dot(qb, cm.astype(BF16), NT)
            den = jnp.sum(pmat, axis=-1, keepdims=True) + s_inter * jnp.sum(qh * nv, axis=-1, keepdims=True)
            hout = num / jnp.maximum(jnp.abs(den), jnp.exp(-m_row))
            b_last = bc[last:last + 1]
            wl = b_last - bc + igc
            m_new = jnp.maximum(b_last + m, jnp.max(wl, axis=0, keepdims=True))
            sw = jnp.exp(wl - m_new)
            decay = jnp.exp(b_last + m - m_new)
            c_s[i, h] = decay * cm + _dot((sw * vh).astype(BF16), kb, TN)
            n_s[i, h:h + 1, :] = decay * nv + jnp.sum(sw * kc[:, sl], axis=0, keepdims=True)
            m_s[i, h:h + 1, :] = jnp.broadcast_to(m_new, (1, LANE))
            if rev:
                hout = _head_norm(hf_ref[i, :, sl] + hout, ng_ref[:, sl]) * jax.nn.sigmoid(og_ref[i, :, sl])
            o_ref[i, :, sl] = hout.astype(o_ref.dtype)


def _mlstm(feat, feat_s, p, offs, g_off, n_lat):
    bsz, t, _ = feat.shape
    wm = p["norm_g"].shape[-1]
    oq, ok, ov, oo = offs
    assert all(o % wm == 0 for o in offs)
    n_ch, n_lat_ch = t // CHUNK, n_lat // CHUNK
    n_r8 = t // SUBLANE
    per = CHUNK // SUBLANE
    sps = _mixer_sps(bsz)
    hf = None
    for rev in (False, True):
        tri, _ = _scan_consts(rev)
        chm = lambda s: _chunk_of(s, n_ch, n_lat_ch, rev)
        const = lambda b, s: (0, 0)

        def cur(off):
            return pl.BlockSpec((sps, CHUNK, wm), lambda b, s: (b, chm(s), off // wm))

        def prv(off):
            return pl.BlockSpec((sps, SUBLANE, wm), lambda b, s: (b, jnp.maximum(chm(s) * per - 1, 0), off // wm))

        def nxt(off):
            return pl.BlockSpec((sps, SUBLANE, wm), lambda b, s: (b, jnp.minimum((chm(s) + 1) * per, n_r8 - 1), off // wm))

        in_specs = [cur(oq), prv(oq), nxt(oq), cur(ok), prv(ok), nxt(ok), cur(ov),
                    pl.BlockSpec((sps, CHUNK, LANE), lambda b, s: (b, chm(s), 0)),
                    pl.BlockSpec((4, 2 * wm), const),
                    pl.BlockSpec((1, 2 * wm), const),
                    pl.BlockSpec((CHUNK, CHUNK), const),
                    pl.BlockSpec((CHUNK, CHUNK), const)]
        args = [feat] * 7 + [feat_s, p["conv_w"], p["conv_b"], tri, tri.T]
        if rev:
            in_specs += [cur(oo), pl.BlockSpec((sps, CHUNK, wm), lambda b, s: (b, chm(s), 0)), pl.BlockSpec((1, wm), const)]
            args += [feat, hf, p["norm_g"]]
        dh = wm // MLSTM_HEADS
        hf = pl.pallas_call(
            functools.partial(_mlstm_kernel, rev=rev, heads=MLSTM_HEADS, n_ch=n_ch, n_lat_ch=n_lat_ch, g_off=g_off),
            grid=(bsz // sps, n_ch),
            in_specs=in_specs,
            out_specs=pl.BlockSpec((sps, CHUNK, wm), lambda b, s: (b, chm(s), 0)),
            out_shape=_sds((bsz, t, wm), BF16 if rev else F32),
            scratch_shapes=[pltpu.VMEM((sps, MLSTM_HEADS, dh, dh), F32),
                            pltpu.VMEM((sps, SUBLANE, dh), F32),
                            pltpu.VMEM((sps, SUBLANE, LANE), F32)],
            compiler_params=_cparams("parallel", "arbitrary"),
            name="mlstm_bwd" if rev else "mlstm_fwd",
        )(*args)
    return hf


def _merge_kernel(h_ref, wg0_ref, wg1_ref, wg2_ref, bg_ref, y0_ref, y1_ref, y2_ref, wb_ref, o_ref):
    h = h_ref[...]
    acc = None
    for n, (wg_ref, y_ref) in enumerate(((wg0_ref, y0_ref), (wg1_ref, y1_ref), (wg2_ref, y2_ref))):
        g = jax.nn.sigmoid(_dot(h, wg_ref[...]) + bg_ref[n])
        term = g * _dot(y_ref[...].astype(BF16), wb_ref[n])
        acc = term if acc is None else acc + term
    o_ref[...] = acc.astype(o_ref.dtype)


def _merge(h, w_gate, layer, bg, ys, wb):
    m, d = h.shape
    r = ys[0].shape[1]
    tm, tn = _pick_tile(m, 512), _pick_tile(d, 512)
    nj = d // tn
    wspecs = [pl.BlockSpec((None, d, tn), lambda j, i, n=n: (layer, 0, n * nj + j)) for n in range(3)]
    yspec = pl.BlockSpec((tm, r), lambda j, i: (i, 0))
    return pl.pallas_call(
        _merge_kernel,
        grid=(nj, m // tm),
        in_specs=[pl.BlockSpec((tm, d), lambda j, i: (i, 0))] + wspecs +
                 [pl.BlockSpec((3, 1, tn), lambda j, i: (0, 0, j)),
                  yspec, yspec, yspec,
                  pl.BlockSpec((None, 3, r, tn), lambda j, i: (layer, 0, 0, j))],
        out_specs=pl.BlockSpec((tm, tn), lambda j, i: (i, j)),
        out_shape=_sds((m, d), BF16),
        compiler_params=_cparams("parallel", "parallel", vmem=VMEM_LIMIT_BIG),
        name="merge",
    )(h, w_gate, w_gate, w_gate, bg, *ys, wb)


def _router_kernel(x_ref, w_ref, o_ref):
    logits = _dot(w_ref[...], x_ref[0], NT)
    ex = jnp.exp(logits - jnp.max(logits, axis=0, keepdims=True))
    o_ref[0] = ex / jnp.sum(ex, axis=0, keepdims=True)


def _router(xm, w_rt):
    bsz, t, d = xm.shape
    e = w_rt.shape[0]
    tr = ROW_TILE
    return pl.pallas_call(
        _router_kernel,
        grid=(bsz, t // tr),
        in_specs=[pl.BlockSpec((1, tr, d), lambda b, i: (b, i, 0)),
                  pl.BlockSpec((e, d), lambda b, i: (0, 0))],
        out_specs=pl.BlockSpec((1, e, tr), lambda b, i: (b, 0, i)),
        out_shape=_sds((bsz, e, t), F32),
        compiler_params=_cparams("parallel", "parallel"),
        name="router",
    )(xm, w_rt)


def _prefix_excl(src_ref, dst_ref, upper):
    e, n = src_ref.shape
    off = jnp.zeros((e, 1), F32)
    for kb in range(n // LANE):
        blk = src_ref[:, kb * LANE:(kb + 1) * LANE]
        inc = _dot(blk.astype(BF16), upper)
        dst_ref[:, kb * LANE:(kb + 1) * LANE] = inc - blk + off
        off = off + inc[:, LANE - 1:LANE]


TOKEN_SPLIT_BITS = 6
TOKEN_SPLIT = 1 << TOKEN_SPLIT_BITS


def _topk_kernel(a_ref, slot_ref, slot_t_ref, gv_t_ref, idx_ref, m_s, r_s, *, cap):
    aff = a_ref[0]
    e, n = aff.shape
    bits = pltpu.bitcast(aff, jnp.int32)
    thr = jnp.zeros((e, 1), jnp.int32)
    for bit in range(30, -1, -1):
        cand = thr | (1 << bit)
        cnt = jnp.sum((bits >= cand).astype(jnp.int32), axis=1, keepdims=True)
        thr = jnp.where(cnt >= cap, cand, thr)
    gt = (bits > thr).astype(F32)
    eq = (bits == thr).astype(F32)
    need = cap - jnp.sum(gt, axis=1, keepdims=True)
    ii = lax.broadcasted_iota(jnp.int32, (LANE, LANE), 0)
    jj = lax.broadcasted_iota(jnp.int32, (LANE, LANE), 1)
    upper = (ii <= jj).astype(BF16)
    m_s[...] = eq
    _prefix_excl(m_s, r_s, upper)
    sel = gt + eq * (r_s[...] < need).astype(F32)
    m_s[...] = sel
    _prefix_excl(m_s, r_s, upper)
    slot = jnp.where(sel > 0.0, r_s[...], -1.0)
    slot_ref[0] = slot
    eye = (lax.broadcasted_iota(jnp.int32, (e, e), 0) == lax.broadcasted_iota(jnp.int32, (e, e), 1)).astype(F32)
    slot_t_ref[0] = _dot(slot, eye, TN, precision=HI)
    gv_t_ref[0] = _dot(aff, eye, TN, precision=HI)
    capp = idx_ref.shape[-1]
    tok = lax.broadcasted_iota(jnp.int32, (SUBLANE, n), 1)
    part = lax.broadcasted_iota(jnp.int32, (SUBLANE, n), 0)
    tvals = jnp.where(part == 0, tok >> TOKEN_SPLIT_BITS, tok & (TOKEN_SPLIT - 1)).astype(F32).astype(BF16)
    sidx = lax.broadcasted_iota(jnp.int32, (capp, n), 0).astype(F32)
    for ei in range(e):
        onehot = jnp.where(slot[ei:ei + 1, :] == sidx, 1.0, 0.0).astype(BF16)
        parts = _dot(tvals, onehot, NT)
        idx_ref[0, ei:ei + 1, :] = (parts[0:1] * TOKEN_SPLIT + parts[1:2]).astype(jnp.int32)


def _topk(aff_t, blk, ntok, cap, capp):
    bsz, e, _ = aff_t.shape
    assert ntok <= TOKEN_SPLIT * 256
    return pl.pallas_call(
        functools.partial(_topk_kernel, cap=cap),
        grid=(bsz,),
        in_specs=[pl.BlockSpec((1, e, ntok), lambda b: (b, 0, blk))],
        out_specs=[pl.BlockSpec((1, e, ntok), lambda b: (b, 0, 0)),
                   pl.BlockSpec((1, ntok, e), lambda b: (b, 0, 0)),
                   pl.BlockSpec((1, ntok, e), lambda b: (b, 0, 0)),
                   pl.BlockSpec((1, e, capp), lambda b: (b, 0, 0))],
        out_shape=[_sds((bsz, e, ntok), F32), _sds((bsz, ntok, e), F32), _sds((bsz, ntok, e), F32),
                   _sds((bsz, e, capp), jnp.int32)],
        scratch_shapes=[pltpu.VMEM((e, ntok), F32), pltpu.VMEM((e, ntok), F32)],
        compiler_params=_cparams("parallel"),
        name="topk",
    )(aff_t)


def _expert_up_kernel(slot_ref, x_ref, wg_ref, wu_ref, o_ref, p_s, g_s, u_s):
    k = pl.program_id(2)
    sps, capp, n = p_s.shape

    @pl.when(k == 0)
    def _():
        sidx = lax.broadcasted_iota(jnp.int32, (capp, n), 0).astype(F32)
        for i in range(sps):
            srow = slot_ref[i, pl.ds(pl.program_id(1), 1), :]
            p_s[i] = jnp.where(srow == sidx, 1.0, 0.0).astype(BF16)
        g_s[...] = jnp.zeros(g_s.shape, F32)
        u_s[...] = jnp.zeros(u_s.shape, F32)

    xg = jnp.concatenate([_dot(p_s[i], x_ref[i]).astype(BF16) for i in range(sps)], axis=0)
    g_s[...] += _dot(xg, wg_ref[0].astype(BF16))
    u_s[...] += _dot(xg, wu_ref[0].astype(BF16))

    @pl.when(k == pl.num_programs(2) - 1)
    def _():
        hid = (jax.nn.silu(g_s[...]) * u_s[...]).astype(o_ref.dtype)
        for i in range(sps):
            o_ref[i, 0] = hid[i * capp:(i + 1) * capp]


def _expert_up(slot, xm, blk, ntok, capp, sps, w_gate, w_up, layer):
    bsz, e = slot.shape[:2]
    d, ff = w_gate.shape[2:]
    tk = 512
    return pl.pallas_call(
        _expert_up_kernel,
        grid=(bsz // sps, e, d // tk),
        in_specs=[pl.BlockSpec((sps, e, ntok), lambda b, ei, k: (b, 0, 0)),
                  pl.BlockSpec((sps, ntok, tk), lambda b, ei, k: (b, blk, k)),
                  pl.BlockSpec((None, 1, tk, ff), lambda b, ei, k: (layer, ei, k, 0)),
                  pl.BlockSpec((None, 1, tk, ff), lambda b, ei, k: (layer, ei, k, 0))],
        out_specs=pl.BlockSpec((sps, 1, capp, ff), lambda b, ei, k: (b, ei, 0, 0)),
        out_shape=_sds((bsz, e, capp, ff), BF16),
        scratch_shapes=[pltpu.VMEM((sps, capp, ntok), BF16), pltpu.VMEM((sps * capp, ff), F32),
                        pltpu.VMEM((sps * capp, ff), F32)],
        compiler_params=_cparams("parallel", "parallel", "arbitrary"),
        name="expert_up",
    )(slot, xm, w_gate, w_up)


def _expert_up_gather_kernel(idx_ref, xs_hbm, m_ref, wg_ref, wu_ref, o_ref, x32_s, xb_s, g_s, u_s, sem, *, off, n_exp):
    b, ei, k = pl.program_id(0), pl.program_id(1), pl.program_id(2)
    cap = x32_s.shape[0]
    tk = wg_ref.shape[1]

    def row_copy(s, row):
        return pltpu.make_async_copy(xs_hbm.at[b, pl.ds(row, 1), :], x32_s.at[pl.ds(s, 1), :], sem)

    @pl.when(k == 0)
    def _():
        def issue(s, carry):
            row_copy(s, idx_ref[b * n_exp + ei, s]).start()
            return carry

        lax.fori_loop(0, cap, issue, 0, unroll=8)

        def wait(s, carry):
            row_copy(s, 0).wait()
            return carry

        lax.fori_loop(0, cap, wait, 0, unroll=8)
        sh = m_ref[0, off:off + 1, :]
        sc = m_ref[0, off + 1:off + 2, :]
        xb_s[...] = (x32_s[...] * (1.0 + sc) + sh).astype(BF16)
        g_s[...] = jnp.zeros(g_s.shape, F32)
        u_s[...] = jnp.zeros(u_s.shape, F32)

    xk = xb_s[:, pl.ds(pl.multiple_of(k * tk, tk), tk)]
    g_s[...] += _dot(xk, wg_ref[0].astype(BF16))
    u_s[...] += _dot(xk, wu_ref[0].astype(BF16))

    @pl.when(k == pl.num_programs(2) - 1)
    def _():
        o_ref[0, 0] = (jax.nn.silu(g_s[...]) * u_s[...]).astype(o_ref.dtype)


def _expert_up_gather(idx, xs, mod6, off, cap, w_gate, w_up, layer):
    bsz, e, _ = idx.shape
    d, ff = w_gate.shape[2:]
    tk = 512
    grid_spec = pltpu.PrefetchScalarGridSpec(
        num_scalar_prefetch=1,
        grid=(bsz, e, d // tk),
        in_specs=[pl.BlockSpec(memory_space=pl.ANY),
                  pl.BlockSpec((1, 6, d), lambda b, ei, k, idx_ref: (b, 0, 0)),
                  pl.BlockSpec((None, 1, tk, ff), lambda b, ei, k, idx_ref: (layer, ei, k, 0)),
                  pl.BlockSpec((None, 1, tk, ff), lambda b, ei, k, idx_ref: (layer, ei, k, 0))],
        out_specs=pl.BlockSpec((1, 1, cap, ff), lambda b, ei, k, idx_ref: (b, ei, 0, 0)),
        scratch_shapes=[pltpu.VMEM((cap, d), F32), pltpu.VMEM((cap, d), BF16),
                        pltpu.VMEM((cap, ff), F32), pltpu.VMEM((cap, ff), F32),
                        pltpu.SemaphoreType.DMA(())])
    return pl.pallas_call(
        functools.partial(_expert_up_gather_kernel, off=off, n_exp=e),
        grid_spec=grid_spec,
        out_shape=_sds((bsz, e, cap, ff), BF16),
        compiler_params=_cparams("parallel", "parallel", "arbitrary"),
        name="expert_up_gather",
    )(idx.reshape(bsz * e, idx.shape[-1]), xs, mod6, w_gate, w_up)


def _expert_down_kernel(hid_ref, wd_ref, o_ref, w_s):
    @pl.when(pl.program_id(2) == 0)
    def _():
        w_s[...] = wd_ref[0].astype(BF16)

    o_ref[0, 0] = _dot(hid_ref[0, 0], w_s[...]).astype(o_ref.dtype)


def _expert_down(hid, w_down, layer):
    bsz, e, capp, ff = hid.shape
    d = w_down.shape[-1]
    tn = _pick_tile(d, 2048)
    return pl.pallas_call(
        _expert_down_kernel,
        grid=(e, d // tn, bsz),
        in_specs=[pl.BlockSpec((1, 1, capp, ff), lambda ei, j, b: (b, ei, 0, 0)),
                  pl.BlockSpec((None, 1, ff, tn), lambda ei, j, b: (layer, ei, 0, j))],
        out_specs=pl.BlockSpec((1, 1, capp, tn), lambda ei, j, b: (b, ei, 0, j)),
        out_shape=_sds((bsz, e, capp, d), BF16),
        scratch_shapes=[pltpu.VMEM((ff, tn), BF16)],
        compiler_params=_cparams("parallel", "parallel", "arbitrary"),
        name="expert_down",
    )(hid, w_down)


COMBINE_TN = 512


def _combine_kernel(*refs, nq):
    ye_ref, slot_t_ref, gv_t_ref = refs[:3]
    o_ref = refs[-1]
    q = pl.program_id(1)
    ei = pl.program_id(2)
    n, e = slot_t_ref.shape[1:]
    capp, d = ye_ref.shape[2:]

    @pl.when(ei == 0)
    def _():
        o_ref[...] = jnp.zeros(o_ref.shape, F32)

    @pl.when(q < nq)
    def _():
        pick = (lax.broadcasted_iota(jnp.int32, (e, LANE), 0) == ei).astype(F32)
        slot_b = _dot(slot_t_ref[0], pick, precision=HI)
        gv_b = _dot(gv_t_ref[0], pick, precision=HI)
        lane = lax.broadcasted_iota(jnp.int32, (1, LANE), 1).astype(F32)
        pt = jnp.concatenate([jnp.where(slot_b == lane + float(c * LANE), 1.0, 0.0).astype(BF16)
                              for c in range(capp // LANE)], axis=1)
        tn = COMBINE_TN
        gv = jnp.concatenate([gv_b] * (tn // LANE), axis=1)
        for j in range(d // tn):
            o_ref[0, :, j * tn:(j + 1) * tn] += _dot(pt, ye_ref[0, 0, :, j * tn:(j + 1) * tn]) * gv


def _combine(ye, slot_t, gv_t, tq, blk0, t, prev=None):
    bsz, e, capp, d = ye.shape
    ntok = slot_t.shape[1]
    nq = ntok // tq
    n_steps = nq if prev is not None else pl.cdiv(t, tq) - blk0
    qc = lambda q: jnp.minimum(q, nq - 1)
    in_specs = [pl.BlockSpec((1, 1, capp, d), lambda b, q, ei: (b, jnp.where(q < nq, ei, e - 1), 0, 0)),
                pl.BlockSpec((1, tq, e), lambda b, q, ei: (b, qc(q), 0)),
                pl.BlockSpec((1, tq, e), lambda b, q, ei: (b, qc(q), 0))]
    args = [ye, slot_t, gv_t]
    aliases = {}
    if prev is not None:
        in_specs.append(pl.BlockSpec(memory_space=pl.ANY))
        args.append(prev)
        aliases = {3: 0}
    return pl.pallas_call(
        functools.partial(_combine_kernel, nq=nq),
        grid=(bsz, n_steps, e),
        in_specs=in_specs,
        out_specs=pl.BlockSpec((1, tq, d), lambda b, q, ei: (b, blk0 + q, 0)),
        out_shape=_sds((bsz, t, d), F32),
        input_output_aliases=aliases,
        compiler_params=_cparams("parallel", "parallel", "arbitrary", vmem=VMEM_LIMIT_BIG),
        name="combine",
    )(*args)


def _round_up(x, m):
    return (x + m - 1) // m * m


def _moe_part(aff_t, xm, blk, ntok, sps, tq, t_out, we, layer, prev=None, gather_src=None):
    e = aff_t.shape[1]
    cap = EC_CAPACITY * ntok // e
    capp = _round_up(cap, LANE)
    slot, slot_t, gv_t, idx = _topk(aff_t, blk, ntok, cap, capp)
    if gather_src is not None:
        assert blk == 0 and cap == capp
        hid = _expert_up_gather(idx, *gather_src, cap, we["gate"], we["up"], layer)
    else:
        hid = _expert_up(slot, xm, blk, ntok, capp, sps, we["gate"], we["up"], layer)
    ye = _expert_down(hid, we["down"], layer)
    return _combine(ye, slot_t, gv_t, tq, blk * ntok // tq, t_out, prev)


def kernel(x, c, ctx, c_ctx, w_mod, b_mod, w_in, b_in, conv_a_w, conv_a_b, lru_wa, lru_ba, lru_wx, lru_bx, lru_lam, gla_wa2, gla_ba, gla_norm_g, conv_c_w, conv_c_b, mlstm_norm_g, w_branch, w_out, ln1_g, ln1_b, w_router, w_e_gate, w_e_up, w_e_down, ln2_g, ln2_b):
    bsz, n_lat, d = x.shape
    nc = ctx.shape[1]
    t = n_lat + nc
    depth = w_mod.shape[0]
    d_rnn = conv_a_w.shape[-1]
    kg, vg = gla_ba.shape[-1], gla_norm_g.shape[-1]
    rank = gla_wa2.shape[2]
    wm = mlstm_norm_g.shape[-1]
    n_gate = 4 * MLSTM_HEADS
    n_exp = w_router.shape[-1]
    alpha = (2 * depth) ** 0.25
    assert 2 * rank + n_gate <= LANE and t % ROW_TILE == 0 and n_lat % ROW_TILE == 0 and bsz < SUBLANE

    sizes = (d_rnn, d_rnn, kg, kg, vg, vg, 2 * rank, wm, wm, wm, wm, n_gate)
    offs = [0]
    for sz in sizes:
        offs.append(offs[-1] + sz)
    n_feat = offs[-1]
    take_main = lambda a: jnp.concatenate([a[..., offs[0]:offs[6]], a[..., offs[7]:offs[11]]], axis=-1)
    take_small = lambda a: jnp.concatenate([a[..., offs[6]:offs[7]], a[..., offs[11]:offs[12]]], axis=-1)
    n_main = (offs[6] - offs[0]) + (offs[11] - offs[7])
    pad_s = LANE - (2 * rank + n_gate)
    mo = {"a_x": 0, "a_g": d_rnn, "b_q": 2 * d_rnn, "b_k": 2 * d_rnn + kg, "b_v": 2 * d_rnn + 2 * kg,
          "b_r": 2 * d_rnn + 2 * kg + vg}
    mo["c_q"] = mo["b_r"] + vg
    mo["c_k"], mo["c_v"], mo["c_o"] = mo["c_q"] + wm, mo["c_q"] + 2 * wm, mo["c_q"] + 3 * wm

    cvec = jnp.concatenate([c, c_ctx[None], jnp.zeros((SUBLANE - bsz - 1, d), F32)], axis=0)
    mod = _mod_all(cvec, w_mod, b_mod)
    xs = jnp.concatenate([x, ctx], axis=1)

    tc = LANE
    while tc < 512 and all(v % (2 * tc) == 0 for v in (offs[6], 4 * wm, d)):
        tc *= 2
    w_in_t = jnp.swapaxes(w_in, 1, 2)
    w_main = _wprep(w_in_t, tc, 0, n_main // tc, ((0, offs[6] // tc, 0), (offs[6] // tc, n_main // tc, offs[7] - offs[6])),
                    "wprep_main")
    w_gate = _wprep(w_in_t, tc, n_main // tc, 3 * d // tc, ((0, 3 * d // tc, n_feat - n_main),), "wprep_gate")
    w_small = jnp.pad(take_small(w_in), ((0, 0), (0, 0), (0, pad_s)))
    w_br = w_branch.astype(BF16)
    w_rt = jnp.swapaxes(w_router, 1, 2).astype(BF16)
    we = {"gate": w_e_gate, "up": w_e_up, "down": w_e_down}

    mod6s = [mod[l].reshape(SUBLANE, 6, d) for l in range(depth)]
    h = _modulate(xs, mod6s[0], 0, n_lat)
    for l in range(depth):
        last = l == depth - 1
        mod6 = mod6s[l]
        b_main = take_main(b_in[l]).reshape(1, n_main)
        b_small = jnp.pad(take_small(b_in[l]), (0, pad_s)).reshape(1, LANE)
        bg = b_in[l][n_feat:].reshape(3, 1, d)
        lru_p = {"conv_w": conv_a_w[l], "conv_b": conv_a_b[l].reshape(1, d_rnn),
                 "wa": lru_wa[l].astype(BF16), "ba": lru_ba[l].reshape(2, 1, d_rnn),
                 "wx": lru_wx[l].astype(BF16), "bx": lru_bx[l].reshape(2, 1, d_rnn),
                 "lam": lru_lam[l].reshape(2, 1, d_rnn)}
        wlr = jnp.zeros((2, LANE, kg), F32)
        for dd in range(2):
            wlr = wlr.at[dd, dd * rank:(dd + 1) * rank].set(gla_wa2[l, dd])
        gla_p = {"wlr": wlr, "ba": gla_ba[l].reshape(2, 1, kg), "norm_g": gla_norm_g[l].reshape(1, vg)}
        ml_p = {"conv_w": conv_c_w[l], "conv_b": conv_c_b[l].reshape(1, 2 * wm), "norm_g": mlstm_norm_g[l].reshape(1, wm)}

        h = h.reshape(bsz * t, d)
        feat = _matmul(h, w_main, l, b_main, F32, 1024, 1024, "feat").reshape(bsz, t, n_main)
        feat_s = _matmul(h, w_small, l, b_small, F32, 1024, LANE, "feat_small").reshape(bsz, t, LANE)
        y0 = _rglru(feat, lru_p, d_rnn, n_lat)
        y1 = _gla(feat, feat_s, gla_p, (mo["b_q"], mo["b_k"], mo["b_v"], mo["b_r"]), n_lat)
        y2 = _mlstm(feat, feat_s, ml_p, (mo["c_q"], mo["c_k"], mo["c_v"], mo["c_o"]), 2 * rank, n_lat)
        ys = [y.reshape(bsz * t, -1) for y in (y0, y1, y2)]
        merged = _merge(h, w_gate, l, bg, ys, w_br)
        y = _matmul(merged, w_out, l, jnp.zeros((1, d), F32), F32, 512, 512, "out_proj").reshape(bsz, t, d)
        xs, xm = _res_ln(xs, y, mod6, 2, ln1_g[l], ln1_b[l], alpha, n_lat, t, mod6, 3)

        aff_t = _router(xm, w_rt[l])
        t_out = n_lat if last else t
        f = _moe_part(aff_t, xm, 0, n_lat, 1, _pick_tile(n_lat, 1024), t_out, we, l, gather_src=(xs, mod6, 3))
        if not last:
            f = _moe_part(aff_t, xm, n_lat // nc, nc, bsz, nc, t_out, we, l, prev=f)
        if last:
            xs = _res_ln(xs, f, mod6, 5, ln2_g[l], ln2_b[l], alpha, n_lat, t_out)
        else:
            xs, h = _res_ln(xs, f, mod6, 5, ln2_g[l], ln2_b[l], alpha, n_lat, t_out, mod6s[l + 1], 0)
    return xs
```

```python
import functools

import jax
import jax.numpy as jnp
from jax import lax
from jax.experimental import pallas as pl
from jax.experimental.pallas import tpu as pltpu

F32 = jnp.float32
BF16 = jnp.bfloat16
HI = lax.Precision.HIGHEST

GRID_W = 64
CHUNK = 64
LRU_C = 8.0
GLA_HEADS = 4
GLA_TAU = 16.0
MLSTM_HEADS = 4
EC_CAPACITY = 2
LN_EPS = 1e-5

LANE = 128
SUBLANE = 8
ROW_TILE = 256
VMEM_LIMIT = 48 << 20
VMEM_LIMIT_BIG = 56 << 20

NT = (((1,), (1,)), ((), ()))
TN = (((0,), (0,)), ((), ()))


def _cparams(*sem, vmem=VMEM_LIMIT):
    return pltpu.CompilerParams(dimension_semantics=sem, vmem_limit_bytes=vmem)


def _sds(shape, dtype):
    return jax.ShapeDtypeStruct(shape, dtype)


def _dot(a, b, dims=None, precision=None):
    if dims is None:
        return jnp.dot(a, b, preferred_element_type=F32, precision=precision)
    return lax.dot_general(a, b, dims, preferred_element_type=F32, precision=precision)


def _mod_kernel(c_ref, w_ref, b_ref, o_ref):
    a = jax.nn.silu(c_ref[...]).astype(BF16)
    o_ref[0] = _dot(a, w_ref[0].astype(BF16)) + b_ref[0]


def _mod_all(cvec, w_mod, b_mod):
    depth, d, n6 = w_mod.shape
    tn = 512
    return pl.pallas_call(
        _mod_kernel,
        grid=(depth, n6 // tn),
        in_specs=[pl.BlockSpec((SUBLANE, d), lambda l, j: (0, 0)),
                  pl.BlockSpec((1, d, tn), lambda l, j: (l, 0, j)),
                  pl.BlockSpec((1, 1, tn), lambda l, j: (l, 0, j))],
        out_specs=pl.BlockSpec((1, SUBLANE, tn), lambda l, j: (l, 0, j)),
        out_shape=_sds((depth, SUBLANE, n6), F32),
        compiler_params=_cparams("parallel", "parallel"),
        name="mod",
    )(cvec, w_mod, b_mod.reshape(depth, 1, n6))


def _modulate_kernel(x_ref, m_ref, o_ref, *, off):
    sh = m_ref[0, off:off + 1, :]
    sc = m_ref[0, off + 1:off + 2, :]
    o_ref[0] = (x_ref[0] * (1.0 + sc) + sh).astype(o_ref.dtype)


def _mod_row_map(n_lat_tiles, batch):
    return lambda b, i: (jnp.where(i < n_lat_tiles, b, batch), 0, 0)


def _modulate(xs, mod6, off, n_lat):
    batch, t, d = xs.shape
    tr = ROW_TILE
    return pl.pallas_call(
        functools.partial(_modulate_kernel, off=off),
        grid=(batch, t // tr),
        in_specs=[pl.BlockSpec((1, tr, d), lambda b, i: (b, i, 0)),
                  pl.BlockSpec((1, 6, d), _mod_row_map(n_lat // tr, batch))],
        out_specs=pl.BlockSpec((1, tr, d), lambda b, i: (b, i, 0)),
        out_shape=_sds((batch, t, d), BF16),
        compiler_params=_cparams("parallel", "parallel"),
        name="modulate",
    )(xs, mod6)


def _mm_kernel(a_ref, w_ref, b_ref, o_ref):
    o_ref[...] = (_dot(a_ref[...], w_ref[...]) + b_ref[...]).astype(o_ref.dtype)


def _pick_tile(n, pref):
    while pref > LANE and n % pref:
        pref //= 2
    assert n % pref == 0
    return pref


def _mm_castw_kernel(a_ref, w_ref, b_ref, o_ref, w_s):
    @pl.when(pl.program_id(1) == 0)
    def _():
        w_s[...] = w_ref[...].astype(BF16)

    o_ref[...] = (_dot(a_ref[...], w_s[...]) + b_ref[...]).astype(o_ref.dtype)


def _matmul(a, w, layer, bias, out_dtype, tm, tn, name):
    m, k = a.shape
    n = w.shape[2]
    tm, tn = _pick_tile(m, tm), _pick_tile(n, tn)
    cast = w.dtype == F32
    return pl.pallas_call(
        _mm_castw_kernel if cast else _mm_kernel,
        grid=(n // tn, m // tm),
        in_specs=[pl.BlockSpec((tm, k), lambda j, i: (i, 0)),
                  pl.BlockSpec((None, k, tn), lambda j, i: (layer, 0, j)),
                  pl.BlockSpec((1, tn), lambda j, i: (0, j))],
        out_specs=pl.BlockSpec((tm, tn), lambda j, i: (i, j)),
        out_shape=_sds((m, n), out_dtype),
        scratch_shapes=[pltpu.VMEM((k, tn), BF16)] if cast else [],
        compiler_params=_cparams("parallel", "arbitrary" if cast else "parallel"),
        name=name,
    )(a, w, bias)


def _wprep_kernel(a_ref, b_ref, o_ref, *, regions):
    j = pl.program_id(2)
    tc = o_ref.shape[-1]
    for lo, hi, shift in regions:
        @pl.when(jnp.logical_and(j >= lo, j < hi))
        def _():
            if shift == 0:
                rows = a_ref[...]
            else:
                rows = jnp.concatenate([a_ref[...], b_ref[:LANE]], axis=0)[shift:shift + tc]
            o_ref[...] = rows.T.astype(BF16)


def _wprep(w_t, tc, base_blk, n_blk, regions, name):
    depth, _, k = w_t.shape
    assert all(0 <= s < LANE and s % SUBLANE == 0 for _, _, s in regions)
    tr = _pick_tile(k, 1024)
    return pl.pallas_call(
        functools.partial(_wprep_kernel, regions=regions),
        grid=(depth, k // tr, n_blk),
        in_specs=[pl.BlockSpec((None, tc, tr), lambda l, i, j: (l, base_blk + j, i)),
                  pl.BlockSpec((None, tc, tr), lambda l, i, j: (l, base_blk + j + 1, i))],
        out_specs=pl.BlockSpec((None, tr, tc), lambda l, i, j: (l, i, j)),
        out_shape=_sds((depth, k, n_blk * tc), BF16),
        compiler_params=_cparams("parallel", "parallel", "parallel"),
        name=name,
    )(w_t, w_t)


def _res_ln_kernel(*refs, off, alpha, next_off):
    if next_off is None:
        x_ref, y_ref, m_ref, g_ref, b_ref, o_ref = refs
    else:
        x_ref, y_ref, m_ref, g_ref, b_ref, mn_ref, o_ref, on_ref = refs
    gate = m_ref[0, off:off + 1, :]
    z = alpha * x_ref[0] + gate * y_ref[0]
    mu = jnp.mean(z, axis=-1, keepdims=True)
    zc = z - mu
    var = jnp.mean(zc * zc, axis=-1, keepdims=True)
    out = zc * lax.rsqrt(var + LN_EPS) * g_ref[...] + b_ref[...]
    o_ref[0] = out
    if next_off is not None:
        sh = mn_ref[0, next_off:next_off + 1, :]
        sc = mn_ref[0, next_off + 1:next_off + 2, :]
        on_ref[0] = (out * (1.0 + sc) + sh).astype(on_ref.dtype)


def _res_ln(xs, y, mod6, off, ln_g, ln_b, alpha, n_lat, t_out, next_mod6=None, next_off=None):
    batch, t, d = xs.shape
    tr = ROW_TILE
    row = pl.BlockSpec((1, tr, d), lambda b, i: (b, i, 0))
    modspec = pl.BlockSpec((1, 6, d), _mod_row_map(n_lat // tr, batch))
    vec = pl.BlockSpec((1, d), lambda b, i: (0, 0))
    in_specs = [row, row, modspec, vec, vec]
    args = [xs, y, mod6, ln_g.reshape(1, d), ln_b.reshape(1, d)]
    out_specs, out_shape = row, _sds((batch, t_out, d), F32)
    if next_off is not None:
        in_specs.append(modspec)
        args.append(next_mod6)
        out_specs, out_shape = [row, row], [out_shape, _sds((batch, t_out, d), BF16)]
    return pl.pallas_call(
        functools.partial(_res_ln_kernel, off=off, alpha=alpha, next_off=next_off),
        grid=(batch, t_out // tr),
        in_specs=in_specs,
        out_specs=out_specs,
        out_shape=out_shape,
        compiler_params=_cparams("parallel", "parallel"),
        name="res_ln",
    )(*args)


def _gelu_tanh(x):
    return jax.nn.gelu(x, approximate=True)


def _lru_gates(u2, wa_ref, ba_ref, wx_ref, bx_ref, lam_ref):
    ub = u2.astype(BF16)
    r = jax.nn.sigmoid(_dot(ub, wa_ref[0]) + ba_ref[...])
    i = jax.nn.sigmoid(_dot(ub, wx_ref[0]) + bx_ref[...])
    log_a = (-LRU_C * jax.nn.softplus(-lam_ref[...])) * r
    a = jnp.exp(log_a)
    bt = jnp.sqrt(1.0 - jnp.exp(2.0 * log_a)) * (i * u2)
    return a, bt


def _lru_lat_kernel(*refs, rev, n_cg):
    if rev:
        (x_ref, pv_ref, nx_ref, cw_ref, cb_ref, wa_ref, ba_ref, wx_ref, bx_ref, lam_ref, e0_ref,
         ag_ref, hf_ref, _alias, o_ref, a_s, b_s, carry) = refs
    else:
        (x_ref, pv_ref, nx_ref, cw_ref, cb_ref, wa_ref, ba_ref, wx_ref, bx_ref, lam_ref, e0_ref,
         _alias, o_ref, a_s, b_s, carry) = refs
    s = pl.program_id(1)
    cg = (n_cg - 1 - s) if rev else s
    bsz, rows, ncol, cb = x_ref.shape

    @pl.when(s == 0)
    def _():
        carry[...] = e0_ref[...]

    x = x_ref[...]
    col = lax.broadcasted_iota(jnp.int32, (1, ncol, 1), 1)
    not_first = (cg > 0).astype(F32)
    not_last = (cg < n_cg - 1).astype(F32)
    top = jnp.where(col == 0, pltpu.roll(pv_ref[:, SUBLANE - 1], 1, 1) * not_first, pltpu.roll(x[:, rows - 1], 1, 1))
    bot1 = jnp.where(col == ncol - 1, pltpu.roll(nx_ref[:, 0], ncol - 1, 1) * not_last, pltpu.roll(x[:, 0], ncol - 1, 1))
    bot2 = jnp.where(col == ncol - 1, pltpu.roll(nx_ref[:, 1], ncol - 1, 1) * not_last, pltpu.roll(x[:, 1], ncol - 1, 1))
    xe = jnp.concatenate([top[:, None], x, bot1[:, None], bot2[:, None]], axis=1)
    u = cb_ref[...].reshape(1, 1, 1, cb)
    for k in range(4):
        u = u + cw_ref[k:k + 1, :].reshape(1, 1, 1, cb) * xe[:, k:k + rows]
    a, bt = _lru_gates(u.reshape(bsz * rows * ncol, cb), wa_ref, ba_ref, wx_ref, bx_ref, lam_ref)
    a_s[...] = a.reshape(bsz, rows, ncol, cb)
    b_s[...] = bt.reshape(bsz, rows, ncol, cb)

    def body(t, hp):
        h, p = hp
        r = (rows - 1 - t) if rev else t
        a_t = a_s[:, r]
        h = a_t * h + b_s[:, r]
        p = p * a_t
        b_s[:, r] = h
        a_s[:, r] = p
        return h, p

    h_end, p_end = lax.fori_loop(0, rows, body, (jnp.zeros((bsz, ncol, cb), F32), jnp.ones((bsz, ncol, cb), F32)), unroll=8)
    av, bv = p_end, h_end
    sh = 1
    while sh < ncol:
        if rev:
            valid = col < ncol - sh
            amt = ncol - sh
        else:
            valid = col >= sh
            amt = sh
        b_sh = jnp.where(valid, pltpu.roll(bv, amt, 1), 0.0)
        a_sh = jnp.where(valid, pltpu.roll(av, amt, 1), 1.0)
        bv = bv + av * b_sh
        av = av * a_sh
        sh *= 2
    e_prev = carry[...]
    e = bv + av * e_prev
    if rev:
        c_in = jnp.where(col == ncol - 1, e_prev, pltpu.roll(e, ncol - 1, 1))
        carry[...] = jnp.broadcast_to(e[:, 0:1], e.shape)
    else:
        c_in = jnp.where(col == 0, e_prev, pltpu.roll(e, 1, 1))
        carry[...] = jnp.broadcast_to(e[:, ncol - 1:ncol], e.shape)
    h = b_s[...] + a_s[...] * c_in[:, None]
    if rev:
        o_ref[...] = (hf_ref[...] + h) * _gelu_tanh(ag_ref[...])
    else:
        o_ref[...] = h


def _lru_ctx_kernel(*refs, rev):
    if rev:
        (x_ref, cw_ref, cb_ref, wa_ref, ba_ref, wx_ref, bx_ref, lam_ref, ag_ref, hf_ref, _base, o_ref, e_ref) = refs
    else:
        (x_ref, cw_ref, cb_ref, wa_ref, ba_ref, wx_ref, bx_ref, lam_ref, _base, o_ref, e_ref) = refs
    bsz, nc, cb = x_ref.shape
    x = x_ref[...]
    t = lax.broadcasted_iota(jnp.int32, (1, nc, 1), 1)
    xm1 = jnp.where(t >= 1, pltpu.roll(x, 1, 1), 0.0)
    xp1 = jnp.where(t < nc - 1, pltpu.roll(x, nc - 1, 1), 0.0)
    xp2 = jnp.where(t < nc - 2, pltpu.roll(x, nc - 2, 1), 0.0)
    w = [cw_ref[k:k + 1, :].reshape(1, 1, cb) for k in range(4)]
    u = w[0] * xm1 + w[1] * x + w[2] * xp1 + w[3] * xp2 + cb_ref[...].reshape(1, 1, cb)
    a, bt = _lru_gates(u.reshape(bsz * nc, cb), wa_ref, ba_ref, wx_ref, bx_ref, lam_ref)
    av = a.reshape(bsz, nc, cb)
    bv = bt.reshape(bsz, nc, cb)
    sh = 1
    while sh < nc:
        if rev:
            valid = t < nc - sh
            amt = nc - sh
        else:
            valid = t >= sh
            amt = sh
        b_sh = jnp.where(valid, pltpu.roll(bv, amt, 1), 0.0)
        a_sh = jnp.where(valid, pltpu.roll(av, amt, 1), 1.0)
        bv = bv + av * b_sh
        av = av * a_sh
        sh *= 2
    if rev:
        o_ref[...] = (hf_ref[...] + bv) * _gelu_tanh(ag_ref[...])
        e_ref[...] = jnp.broadcast_to(bv[:, 0:1], (bsz, SUBLANE, cb))
    else:
        o_ref[...] = bv
        e_ref[...] = jnp.broadcast_to(bv[:, nc - 1:nc], (bsz, SUBLANE, cb))


def _rglru(feat, p, d_rnn, n_lat):
    bsz, t, nf = feat.shape
    nc = t - n_lat
    nb, bs, _ = p["wa"][0].shape
    cb = bs
    assert d_rnn == nb * bs and cb % LANE == 0 and n_lat % nc == 0 and nc % GRID_W == 0
    rows = n_lat // GRID_W
    assert rows % SUBLANE == 0 and GRID_W % SUBLANE == 0
    n_cg = GRID_W // SUBLANE
    ag_off = d_rnn // cb
    feat4 = feat.reshape(bsz, t // GRID_W, GRID_W, nf)
    ctx_blk = n_lat // nc

    def wspecs(im):
        return [pl.BlockSpec((4, cb), im(lambda j: (0, j))),
                pl.BlockSpec((1, cb), im(lambda j: (0, j))),
                pl.BlockSpec((1, bs, bs), im(lambda j: (j, 0, 0))),
                pl.BlockSpec((1, cb), im(lambda j: (0, j))),
                pl.BlockSpec((1, bs, bs), im(lambda j: (j, 0, 0))),
                pl.BlockSpec((1, cb), im(lambda j: (0, j))),
                pl.BlockSpec((1, cb), im(lambda j: (0, j)))]

    def wargs(d):
        return [p["conv_w"], p["conv_b"], p["wa"][d], p["ba"][d], p["wx"][d], p["bx"][d], p["lam"][d]]

    im1 = lambda f: (lambda j: f(j))
    im2 = lambda f: (lambda j, s: f(j))
    hf = None
    out = None
    for rev in (False, True):
        in_specs = [pl.BlockSpec((bsz, nc, cb), lambda j: (0, ctx_blk, j))] + wspecs(im1)
        args = [feat] + wargs(int(rev))
        if rev:
            in_specs += [pl.BlockSpec((bsz, nc, cb), lambda j: (0, ctx_blk, ag_off + j)),
                         pl.BlockSpec((bsz, nc, cb), lambda j: (0, ctx_blk, j))]
            args += [feat, hf]
        in_specs += [pl.BlockSpec(memory_space=pl.ANY)]
        args += [jnp.zeros((bsz, t, d_rnn), F32)]
        part, e0 = pl.pallas_call(
            functools.partial(_lru_ctx_kernel, rev=rev),
            grid=(nb,),
            in_specs=in_specs,
            out_specs=[pl.BlockSpec((bsz, nc, cb), lambda j: (0, ctx_blk, j)),
                       pl.BlockSpec((bsz, SUBLANE, cb), lambda j: (0, 0, j))],
            out_shape=[_sds((bsz, t, d_rnn), F32), _sds((bsz, SUBLANE, d_rnn), F32)],
            input_output_aliases={len(args) - 1: 0},
            compiler_params=_cparams("parallel"),
            name="lru_ctx_bwd" if rev else "lru_ctx_fwd",
        )(*args)
        cgm = (lambda s: n_cg - 1 - s) if rev else (lambda s: s)
        in_specs = [pl.BlockSpec((bsz, rows, SUBLANE, cb), lambda j, s: (0, 0, cgm(s), j)),
                    pl.BlockSpec((bsz, SUBLANE, SUBLANE, cb), lambda j, s: (0, rows // SUBLANE - 1, jnp.maximum(cgm(s) - 1, 0), j)),
                    pl.BlockSpec((bsz, SUBLANE, SUBLANE, cb), lambda j, s: (0, 0, jnp.minimum(cgm(s) + 1, n_cg - 1), j))]
        in_specs += wspecs(im2) + [pl.BlockSpec((bsz, SUBLANE, cb), lambda j, s: (0, 0, j))]
        args = [feat4, feat4, feat4] + wargs(int(rev)) + [e0]
        if rev:
            in_specs += [pl.BlockSpec((bsz, rows, SUBLANE, cb), lambda j, s: (0, 0, cgm(s), ag_off + j)),
                         pl.BlockSpec((bsz, rows, SUBLANE, cb), lambda j, s: (0, 0, cgm(s), j))]
            args += [feat4, hf.reshape(bsz, t // GRID_W, GRID_W, d_rnn)]
        in_specs += [pl.BlockSpec(memory_space=pl.ANY)]
        args += [part.reshape(bsz, t // GRID_W, GRID_W, d_rnn)]
        res = pl.pallas_call(
            functools.partial(_lru_lat_kernel, rev=rev, n_cg=n_cg),
            grid=(nb, n_cg),
            in_specs=in_specs,
            out_specs=pl.BlockSpec((bsz, rows, SUBLANE, cb), lambda j, s: (0, 0, cgm(s), j)),
            out_shape=_sds((bsz, t // GRID_W, GRID_W, d_rnn), F32),
            scratch_shapes=[pltpu.VMEM((bsz, rows, SUBLANE, cb), F32),
                            pltpu.VMEM((bsz, rows, SUBLANE, cb), F32),
                            pltpu.VMEM((bsz, SUBLANE, cb), F32)],
            input_output_aliases={len(args) - 1: 0},
            compiler_params=_cparams("parallel", "arbitrary"),
            name="lru_lat_bwd" if rev else "lru_lat_fwd",
        )(*args)
        res = res.reshape(bsz, t, d_rnn)
        if rev:
            out = res
        else:
            hf = res
    return out


def _chunk_of(s, n_ch, n_lat_ch, rev):
    if rev:
        return n_ch - 1 - s
    return jnp.where(s < n_ch - n_lat_ch, s + n_lat_ch, s - (n_ch - n_lat_ch))


MIXER_SPS = 4


def _mixer_sps(bsz):
    sps = MIXER_SPS
    while bsz % sps:
        sps //= 2
    return sps


def _head_norm(o, g):
    mu = jnp.mean(o, axis=-1, keepdims=True)
    oc = o - mu
    var = jnp.mean(oc * oc, axis=-1, keepdims=True)
    return oc * lax.rsqrt(var + LN_EPS) * g


def _gla_kernel(*refs, rev, heads):
    if rev:
        (q_ref, k_ref, v_ref, lr_ref, wlr_ref, ba_ref, tri_ref, rsel_ref, r_ref, of_ref, ng_ref, o_ref, st) = refs
    else:
        (q_ref, k_ref, v_ref, lr_ref, wlr_ref, ba_ref, tri_ref, rsel_ref, o_ref, st) = refs
    s = pl.program_id(1)

    @pl.when(s == 0)
    def _():
        st[...] = jnp.zeros(st.shape, F32)

    dk = q_ref.shape[-1] // heads
    dv = v_ref.shape[-1] // heads
    tri = tri_ref[...]
    for i in range(q_ref.shape[0]):
        g = jax.nn.log_sigmoid(_dot(lr_ref[i], wlr_ref[...], precision=HI) + ba_ref[...]) / GLA_TAU
        bcum = _dot(tri, g, precision=HI)
        bm = _dot(rsel_ref[...], g, precision=HI)
        q = q_ref[i] * dk ** -0.5
        k = k_ref[i]
        v = v_ref[i]
        for h in range(heads):
            sk = slice(h * dk, (h + 1) * dk)
            sv = slice(h * dv, (h + 1) * dv)
            qh, kh, bh = q[:, sk], k[:, sk], bcum[:, sk]
            vh = v[:, sv].astype(BF16)
            bmid, blast = bm[0:1, sk], bm[1:2, sk]
            att = _dot((qh * jnp.exp(bh - bmid)).astype(BF16), (kh * jnp.exp(bmid - bh)).astype(BF16), NT) * tri
            sth = st[i, h]
            o = _dot(att.astype(BF16), vh) + _dot((qh * jnp.exp(bh)).astype(BF16), sth.astype(BF16), NT)
            st[i, h] = jnp.exp(blast) * sth + _dot(vh, (kh * jnp.exp(blast - bh)).astype(BF16), TN)
            if rev:
                o = _head_norm(of_ref[i, :, sv] + o, ng_ref[:, sv]) * jax.nn.silu(r_ref[i, :, sv])
            o_ref[i, :, sv] = o.astype(o_ref.dtype)


def _scan_consts(rev):
    i = jnp.arange(CHUNK)
    tri = (i[None, :] >= i[:, None]) if rev else (i[None, :] <= i[:, None])
    tri = tri.astype(F32)
    mid = CHUNK // 2 if rev else CHUNK // 2 - 1
    last = 0 if rev else CHUNK - 1
    rsel = jnp.zeros((SUBLANE, CHUNK), F32).at[0].set(tri[mid]).at[1].set(tri[last])
    return tri, rsel


def _gla(feat, feat_s, p, offs, n_lat):
    bsz, t, _ = feat.shape
    kg, vg = p["ba"][0].shape[-1], p["norm_g"].shape[-1]
    oq, ok, ov, orr = offs
    assert oq % kg == 0 and ok % kg == 0 and ov % vg == 0 and orr % vg == 0
    n_ch, n_lat_ch = t // CHUNK, n_lat // CHUNK
    sps = _mixer_sps(bsz)
    of = None
    for rev in (False, True):
        d = int(rev)
        tri, rsel = _scan_consts(rev)
        cm = lambda b, s: (b, _chunk_of(s, n_ch, n_lat_ch, rev))
        const = lambda b, s: (0, 0)
        in_specs = [pl.BlockSpec((sps, CHUNK, kg), lambda b, s: cm(b, s) + (oq // kg,)),
                    pl.BlockSpec((sps, CHUNK, kg), lambda b, s: cm(b, s) + (ok // kg,)),
                    pl.BlockSpec((sps, CHUNK, vg), lambda b, s: cm(b, s) + (ov // vg,)),
                    pl.BlockSpec((sps, CHUNK, LANE), lambda b, s: cm(b, s) + (0,)),
                    pl.BlockSpec((LANE, kg), const),
                    pl.BlockSpec((1, kg), const),
                    pl.BlockSpec((CHUNK, CHUNK), const),
                    pl.BlockSpec((SUBLANE, CHUNK), const)]
        args = [feat, feat, feat, feat_s, p["wlr"][d], p["ba"][d], tri, rsel]
        if rev:
            in_specs += [pl.BlockSpec((sps, CHUNK, vg), lambda b, s: cm(b, s) + (orr // vg,)),
                         pl.BlockSpec((sps, CHUNK, vg), lambda b, s: cm(b, s) + (0,)),
                         pl.BlockSpec((1, vg), const)]
            args += [feat, of, p["norm_g"]]
        res = pl.pallas_call(
            functools.partial(_gla_kernel, rev=rev, heads=GLA_HEADS),
            grid=(bsz // sps, n_ch),
            in_specs=in_specs,
            out_specs=pl.BlockSpec((sps, CHUNK, vg), lambda b, s: cm(b, s) + (0,)),
            out_shape=_sds((bsz, t, vg), BF16 if rev else F32),
            scratch_shapes=[pltpu.VMEM((sps, GLA_HEADS, vg // GLA_HEADS, kg // GLA_HEADS), F32)],
            compiler_params=_cparams("parallel", "arbitrary"),
            name="gla_bwd" if rev else "gla_fwd",
        )(*args)
        of = res
    return of


def _conv_rows(x, prev_row, next_rows, w_ref, b_ref, lo, hi):
    n = x.shape[0]
    t = lax.broadcasted_iota(jnp.int32, (n, 1), 0)
    xm1 = jnp.where(t == 0, prev_row, pltpu.roll(x, 1, 0))
    xp1 = jnp.where(t == n - 1, next_rows[0:1], pltpu.roll(x, n - 1, 0))
    xp2 = jnp.where(t == n - 2, next_rows[0:1], jnp.where(t == n - 1, next_rows[1:2], pltpu.roll(x, n - 2, 0)))
    w = w_ref[:, lo:hi]
    return w[0:1] * xm1 + w[1:2] * x + w[2:3] * xp1 + w[3:4] * xp2 + b_ref[:, lo:hi]


def _mlstm_kernel(*refs, rev, heads, n_ch, n_lat_ch, g_off):
    if rev:
        (q_ref, qp_ref, qn_ref, k_ref, kp_ref, kn_ref, v_ref, cg_ref, cw_ref, cbias_ref, tri_ref, trit_ref,
         og_ref, hf_ref, ng_ref, o_ref, c_s, n_s, m_s) = refs
    else:
        (q_ref, qp_ref, qn_ref, k_ref, kp_ref, kn_ref, v_ref, cg_ref, cw_ref, cbias_ref, tri_ref, trit_ref,
         o_ref, c_s, n_s, m_s) = refs
    s = pl.program_id(1)

    @pl.when(s == 0)
    def _():
        c_s[...] = jnp.zeros(c_s.shape, F32)
        n_s[...] = jnp.zeros(n_s.shape, F32)
        m_s[...] = jnp.zeros(m_s.shape, F32)

    ch = _chunk_of(s, n_ch, n_lat_ch, rev)
    not_first = jnp.logical_and(ch != 0, ch != n_lat_ch).astype(F32)
    not_last = jnp.logical_and(ch != n_lat_ch - 1, ch != n_ch - 1).astype(F32)
    wm = q_ref.shape[-1]
    dh = wm // heads
    tri = tri_ref[...]
    ci0 = g_off + int(rev) * 2 * heads
    cf0 = ci0 + heads
    lane = lax.broadcasted_iota(jnp.int32, (SUBLANE, LANE), 1)
    row = lax.broadcasted_iota(jnp.int32, (SUBLANE, LANE), 0)
    sel_i = (lane == row + ci0).astype(F32)
    sel_f = (lane == row + cf0).astype(F32)
    last = 0 if rev else CHUNK - 1
    for i in range(q_ref.shape[0]):
        qc = jax.nn.silu(_conv_rows(q_ref[i], qp_ref[i, SUBLANE - 1:SUBLANE] * not_first, qn_ref[i, 0:2] * not_last,
                                    cw_ref, cbias_ref, 0, wm))
        kc = jax.nn.silu(_conv_rows(k_ref[i], kp_ref[i, SUBLANE - 1:SUBLANE] * not_first, kn_ref[i, 0:2] * not_last,
                                    cw_ref, cbias_ref, wm, 2 * wm)) * dh ** -0.5
        v = v_ref[i]
        gts = cg_ref[i]
        gls = jax.nn.log_sigmoid(gts)
        bcol_all = _dot(tri, gls, precision=HI)
        ig_rows = _dot(sel_i, gts, NT, precision=HI)
        b_rows = _dot(_dot(sel_f, gls, NT, precision=HI), trit_ref[...], precision=HI)
        for h in range(heads):
            sl = slice(h * dh, (h + 1) * dh)
            qh = qc[:, sl]
            qb, kb, vh = qh.astype(BF16), kc[:, sl].astype(BF16), v[:, sl]
            bc = bcol_all[:, cf0 + h:cf0 + h + 1]
            igc = gts[:, ci0 + h:ci0 + h + 1]
            m = m_s[i, h:h + 1, 0:1]
            dmat = jnp.where(tri > 0.0, bc - b_rows[h:h + 1, :] + ig_rows[h:h + 1, :], -jnp.inf)
            inter = bc + m
            m_row = jnp.maximum(jnp.max(dmat, axis=-1, keepdims=True), inter)
            pmat = _dot(qb, kb, NT) * jnp.exp(dmat - m_row)
            s_inter = jnp.exp(inter - m_row)
            cm = c_s[i, h]
            nv = n_s[i, h:h + 1, :]
            num = _dot(pmat.astype(BF16), vh.astype(BF16)) + s_inter * _dot(qb, cm.astype(BF16), NT)
            den = jnp.sum(pmat, axis=-1, keepdims=True) + s_inter * jnp.sum(qh * nv, axis=-1, keepdims=True)
            hout = num / jnp.maximum(jnp.abs(den), jnp.exp(-m_row))
            b_last = bc[last:last + 1]
            wl = b_last - bc + igc
            m_new = jnp.maximum(b_last + m, jnp.max(wl, axis=0, keepdims=True))
            sw = jnp.exp(wl - m_new)
            decay = jnp.exp(b_last + m - m_new)
            c_s[i, h] = decay * cm + _dot((sw * vh).astype(BF16), kb, TN)
            n_s[i, h:h + 1, :] = decay * nv + jnp.sum(sw * kc[:, sl], axis=0, keepdims=True)
            m_s[i, h:h + 1, :] = jnp.broadcast_to(m_new, (1, LANE))
            if rev:
                hout = _head_norm(hf_ref[i, :, sl] + hout, ng_ref[:, sl]) * jax.nn.sigmoid(og_ref[i, :, sl])
            o_ref[i, :, sl] = hout.astype(o_ref.dtype)


def _mlstm(feat, feat_s, p, offs, g_off, n_lat):
    bsz, t, _ = feat.shape
    wm = p["norm_g"].shape[-1]
    oq, ok, ov, oo = offs
    assert all(o % wm == 0 for o in offs)
    n_ch, n_lat_ch = t // CHUNK, n_lat // CHUNK
    n_r8 = t // SUBLANE
    per = CHUNK // SUBLANE
    sps = _mixer_sps(bsz)
    hf = None
    for rev in (False, True):
        tri, _ = _scan_consts(rev)
        chm = lambda s: _chunk_of(s, n_ch, n_lat_ch, rev)
        const = lambda b, s: (0, 0)

        def cur(off):
            return pl.BlockSpec((sps, CHUNK, wm), lambda b, s: (b, chm(s), off // wm))

        def prv(off):
            return pl.BlockSpec((sps, SUBLANE, wm), lambda b, s: (b, jnp.maximum(chm(s) * per - 1, 0), off // wm))

        def nxt(off):
            return pl.BlockSpec((sps, SUBLANE, wm), lambda b, s: (b, jnp.minimum((chm(s) + 1) * per, n_r8 - 1), off // wm))

        in_specs = [cur(oq), prv(oq), nxt(oq), cur(ok), prv(ok), nxt(ok), cur(ov),
                    pl.BlockSpec((sps, CHUNK, LANE), lambda b, s: (b, chm(s), 0)),
                    pl.BlockSpec((4, 2 * wm), const),
                    pl.BlockSpec((1, 2 * wm), const),
                    pl.BlockSpec((CHUNK, CHUNK), const),
                    pl.BlockSpec((CHUNK, CHUNK), const)]
        args = [feat] * 7 + [feat_s, p["conv_w"], p["conv_b"], tri, tri.T]
        if rev:
            in_specs += [cur(oo), pl.BlockSpec((sps, CHUNK, wm), lambda b, s: (b, chm(s), 0)), pl.BlockSpec((1, wm), const)]
            args += [feat, hf, p["norm_g"]]
        dh = wm // MLSTM_HEADS
        hf = pl.pallas_call(
            functools.partial(_mlstm_kernel, rev=rev, heads=MLSTM_HEADS, n_ch=n_ch, n_lat_ch=n_lat_ch, g_off=g_off),
            grid=(bsz // sps, n_ch),
            in_specs=in_specs,
            out_specs=pl.BlockSpec((sps, CHUNK, wm), lambda b, s: (b, chm(s), 0)),
            out_shape=_sds((bsz, t, wm), BF16 if rev else F32),
            scratch_shapes=[pltpu.VMEM((sps, MLSTM_HEADS, dh, dh), F32),
                            pltpu.VMEM((sps, SUBLANE, dh), F32),
                            pltpu.VMEM((sps, SUBLANE, LANE), F32)],
            compiler_params=_cparams("parallel", "arbitrary"),
            name="mlstm_bwd" if rev else "mlstm_fwd",
        )(*args)
    return hf


def _merge_kernel(h_ref, wg0_ref, wg1_ref, wg2_ref, bg_ref, y0_ref, y1_ref, y2_ref, wb_ref, o_ref):
    h = h_ref[...]
    acc = None
    for n, (wg_ref, y_ref) in enumerate(((wg0_ref, y0_ref), (wg1_ref, y1_ref), (wg2_ref, y2_ref))):
        g = jax.nn.sigmoid(_dot(h, wg_ref[...]) + bg_ref[n])
        term = g * _dot(y_ref[...].astype(BF16), wb_ref[n])
        acc = term if acc is None else acc + term
    o_ref[...] = acc.astype(o_ref.dtype)


def _merge(h, w_gate, layer, bg, ys, wb):
    m, d = h.shape
    r = ys[0].shape[1]
    tm, tn = _pick_tile(m, 512), _pick_tile(d, 512)
    nj = d // tn
    wspecs = [pl.BlockSpec((None, d, tn), lambda j, i, n=n: (layer, 0, n * nj + j)) for n in range(3)]
    yspec = pl.BlockSpec((tm, r), lambda j, i: (i, 0))
    return pl.pallas_call(
        _merge_kernel,
        grid=(nj, m // tm),
        in_specs=[pl.BlockSpec((tm, d), lambda j, i: (i, 0))] + wspecs +
                 [pl.BlockSpec((3, 1, tn), lambda j, i: (0, 0, j)),
                  yspec, yspec, yspec,
                  pl.BlockSpec((None, 3, r, tn), lambda j, i: (layer, 0, 0, j))],
        out_specs=pl.BlockSpec((tm, tn), lambda j, i: (i, j)),
        out_shape=_sds((m, d), BF16),
        compiler_params=_cparams("parallel", "parallel", vmem=VMEM_LIMIT_BIG),
        name="merge",
    )(h, w_gate, w_gate, w_gate, bg, *ys, wb)


def _router_kernel(x_ref, w_ref, o_ref):
    logits = _dot(w_ref[...], x_ref[0], NT)
    ex = jnp.exp(logits - jnp.max(logits, axis=0, keepdims=True))
    o_ref[0] = ex / jnp.sum(ex, axis=0, keepdims=True)


def _router(xm, w_rt):
    bsz, t, d = xm.shape
    e = w_rt.shape[0]
    tr = ROW_TILE
    return pl.pallas_call(
        _router_kernel,
        grid=(bsz, t // tr),
        in_specs=[pl.BlockSpec((1, tr, d), lambda b, i: (b, i, 0)),
                  pl.BlockSpec((e, d), lambda b, i: (0, 0))],
        out_specs=pl.BlockSpec((1, e, tr), lambda b, i: (b, 0, i)),
        out_shape=_sds((bsz, e, t), F32),
        compiler_params=_cparams("parallel", "parallel"),
        name="router",
    )(xm, w_rt)


def _prefix_excl(src_ref, dst_ref, upper):
    e, n = src_ref.shape
    off = jnp.zeros((e, 1), F32)
    for kb in range(n // LANE):
        blk = src_ref[:, kb * LANE:(kb + 1) * LANE]
        inc = _dot(blk.astype(BF16), upper)
        dst_ref[:, kb * LANE:(kb + 1) * LANE] = inc - blk + off
        off = off + inc[:, LANE - 1:LANE]


TOKEN_SPLIT_BITS = 6
TOKEN_SPLIT = 1 << TOKEN_SPLIT_BITS


def _topk_kernel(a_ref, slot_ref, slot_t_ref, gv_t_ref, idx_ref, m_s, r_s, *, cap):
    aff = a_ref[0]
    e, n = aff.shape
    bits = pltpu.bitcast(aff, jnp.int32)
    thr = jnp.zeros((e, 1), jnp.int32)
    for bit in range(30, -1, -1):
        cand = thr | (1 << bit)
        cnt = jnp.sum((bits >= cand).astype(jnp.int32), axis=1, keepdims=True)
        thr = jnp.where(cnt >= cap, cand, thr)
    gt = (bits > thr).astype(F32)
    eq = (bits == thr).astype(F32)
    need = cap - jnp.sum(gt, axis=1, keepdims=True)
    ii = lax.broadcasted_iota(jnp.int32, (LANE, LANE), 0)
    jj = lax.broadcasted_iota(jnp.int32, (LANE, LANE), 1)
    upper = (ii <= jj).astype(BF16)
    m_s[...] = eq
    _prefix_excl(m_s, r_s, upper)
    sel = gt + eq * (r_s[...] < need).astype(F32)
    m_s[...] = sel
    _prefix_excl(m_s, r_s, upper)
    slot = jnp.where(sel > 0.0, r_s[...], -1.0)
    slot_ref[0] = slot
    eye = (lax.broadcasted_iota(jnp.int32, (e, e), 0) == lax.broadcasted_iota(jnp.int32, (e, e), 1)).astype(F32)
    slot_t_ref[0] = _dot(slot, eye, TN, precision=HI)
    gv_t_ref[0] = _dot(aff, eye, TN, precision=HI)
    capp = idx_ref.shape[-1]
    tok = lax.broadcasted_iota(jnp.int32, (SUBLANE, n), 1)
    part = lax.broadcasted_iota(jnp.int32, (SUBLANE, n), 0)
    tvals = jnp.where(part == 0, tok >> TOKEN_SPLIT_BITS, tok & (TOKEN_SPLIT - 1)).astype(F32).astype(BF16)
    sidx = lax.broadcasted_iota(jnp.int32, (capp, n), 0).astype(F32)
    for ei in range(e):
        onehot = jnp.where(slot[ei:ei + 1, :] == sidx, 1.0, 0.0).astype(BF16)
        parts = _dot(tvals, onehot, NT)
        idx_ref[0, ei:ei + 1, :] = (parts[0:1] * TOKEN_SPLIT + parts[1:2]).astype(jnp.int32)


def _topk(aff_t, blk, ntok, cap, capp):
    bsz, e, _ = aff_t.shape
    assert ntok <= TOKEN_SPLIT * 256
    return pl.pallas_call(
        functools.partial(_topk_kernel, cap=cap),
        grid=(bsz,),
        in_specs=[pl.BlockSpec((1, e, ntok), lambda b: (b, 0, blk))],
        out_specs=[pl.BlockSpec((1, e, ntok), lambda b: (b, 0, 0)),
                   pl.BlockSpec((1, ntok, e), lambda b: (b, 0, 0)),
                   pl.BlockSpec((1, ntok, e), lambda b: (b, 0, 0)),
                   pl.BlockSpec((1, e, capp), lambda b: (b, 0, 0))],
        out_shape=[_sds((bsz, e, ntok), F32), _sds((bsz, ntok, e), F32), _sds((bsz, ntok, e), F32),
                   _sds((bsz, e, capp), jnp.int32)],
        scratch_shapes=[pltpu.VMEM((e, ntok), F32), pltpu.VMEM((e, ntok), F32)],
        compiler_params=_cparams("parallel"),
        name="topk",
    )(aff_t)


def _expert_up_kernel(slot_ref, x_ref, wg_ref, wu_ref, o_ref, p_s, g_s, u_s):
    k = pl.program_id(2)
    sps, capp, n = p_s.shape

    @pl.when(k == 0)
    def _():
        sidx = lax.broadcasted_iota(jnp.int32, (capp, n), 0).astype(F32)
        for i in range(sps):
            srow = slot_ref[i, pl.ds(pl.program_id(1), 1), :]
            p_s[i] = jnp.where(srow == sidx, 1.0, 0.0).astype(BF16)
        g_s[...] = jnp.zeros(g_s.shape, F32)
        u_s[...] = jnp.zeros(u_s.shape, F32)

    xg = jnp.concatenate([_dot(p_s[i], x_ref[i]).astype(BF16) for i in range(sps)], axis=0)
    g_s[...] += _dot(xg, wg_ref[0].astype(BF16))
    u_s[...] += _dot(xg, wu_ref[0].astype(BF16))

    @pl.when(k == pl.num_programs(2) - 1)
    def _():
        hid = (jax.nn.silu(g_s[...]) * u_s[...]).astype(o_ref.dtype)
        for i in range(sps):
            o_ref[i, 0] = hid[i * capp:(i + 1) * capp]


def _expert_up(slot, xm, blk, ntok, capp, sps, w_gate, w_up, layer):
    bsz, e = slot.shape[:2]
    d, ff = w_gate.shape[2:]
    tk = 512
    return pl.pallas_call(
        _expert_up_kernel,
        grid=(bsz // sps, e, d // tk),
        in_specs=[pl.BlockSpec((sps, e, ntok), lambda b, ei, k: (b, 0, 0)),
                  pl.BlockSpec((sps, ntok, tk), lambda b, ei, k: (b, blk, k)),
                  pl.BlockSpec((None, 1, tk, ff), lambda b, ei, k: (layer, ei, k, 0)),
                  pl.BlockSpec((None, 1, tk, ff), lambda b, ei, k: (layer, ei, k, 0))],
        out_specs=pl.BlockSpec((sps, 1, capp, ff), lambda b, ei, k: (b, ei, 0, 0)),
        out_shape=_sds((bsz, e, capp, ff), BF16),
        scratch_shapes=[pltpu.VMEM((sps, capp, ntok), BF16), pltpu.VMEM((sps * capp, ff), F32),
                        pltpu.VMEM((sps * capp, ff), F32)],
        compiler_params=_cparams("parallel", "parallel", "arbitrary"),
        name="expert_up",
    )(slot, xm, w_gate, w_up)


def _expert_up_gather_kernel(idx_ref, xs_hbm, m_ref, wg_ref, wu_ref, o_ref, x32_s, xb_s, g_s, u_s, sem, *, off, n_exp):
    bg, ei, k = pl.program_id(0), pl.program_id(1), pl.program_id(2)
    sps, cap, _ = x32_s.shape
    tk = wg_ref.shape[1]

    def row_copy(i, s, row):
        return pltpu.make_async_copy(xs_hbm.at[bg * sps + i, pl.ds(row, 1), :], x32_s.at[i, pl.ds(s, 1), :], sem.at[i])

    @pl.when(k == 0)
    def _():
        for i in range(sps):
            def issue(s, carry, i=i):
                row_copy(i, s, idx_ref[(bg * sps + i) * n_exp + ei, s]).start()
                return carry

            lax.fori_loop(0, cap, issue, 0, unroll=8)
        for i in range(sps):
            def wait(s, carry, i=i):
                row_copy(i, s, 0).wait()
                return carry

            lax.fori_loop(0, cap, wait, 0, unroll=8)
            sh = m_ref[i, off:off + 1, :]
            sc = m_ref[i, off + 1:off + 2, :]
            xb_s[i * cap:(i + 1) * cap, :] = (x32_s[i] * (1.0 + sc) + sh).astype(BF16)
        g_s[...] = jnp.zeros(g_s.shape, F32)
        u_s[...] = jnp.zeros(u_s.shape, F32)

    xk = xb_s[:, pl.ds(pl.multiple_of(k * tk, tk), tk)]
    g_s[...] += _dot(xk, wg_ref[0].astype(BF16))
    u_s[...] += _dot(xk, wu_ref[0].astype(BF16))

    @pl.when(k == pl.num_programs(2) - 1)
    def _():
        hid = (jax.nn.silu(g_s[...]) * u_s[...]).astype(o_ref.dtype)
        for i in range(sps):
            o_ref[i, 0] = hid[i * cap:(i + 1) * cap]


GATHER_SPS = 2


def _expert_up_gather(idx, xs, mod6, off, cap, w_gate, w_up, layer):
    bsz, e, _ = idx.shape
    d, ff = w_gate.shape[2:]
    tk = 512
    sps = GATHER_SPS if bsz % GATHER_SPS == 0 else 1
    grid_spec = pltpu.PrefetchScalarGridSpec(
        num_scalar_prefetch=1,
        grid=(bsz // sps, e, d // tk),
        in_specs=[pl.BlockSpec(memory_space=pl.ANY),
                  pl.BlockSpec((sps, 6, d), lambda b, ei, k, idx_ref: (b, 0, 0)),
                  pl.BlockSpec((None, 1, tk, ff), lambda b, ei, k, idx_ref: (layer, ei, k, 0)),
                  pl.BlockSpec((None, 1, tk, ff), lambda b, ei, k, idx_ref: (layer, ei, k, 0))],
        out_specs=pl.BlockSpec((sps, 1, cap, ff), lambda b, ei, k, idx_ref: (b, ei, 0, 0)),
        scratch_shapes=[pltpu.VMEM((sps, cap, d), F32), pltpu.VMEM((sps * cap, d), BF16),
                        pltpu.VMEM((sps * cap, ff), F32), pltpu.VMEM((sps * cap, ff), F32),
                        pltpu.SemaphoreType.DMA((sps,))])
    return pl.pallas_call(
        functools.partial(_expert_up_gather_kernel, off=off, n_exp=e),
        grid_spec=grid_spec,
        out_shape=_sds((bsz, e, cap, ff), BF16),
        compiler_params=_cparams("parallel", "parallel", "arbitrary"),
        name="expert_up_gather",
    )(idx.reshape(bsz * e, idx.shape[-1]), xs, mod6, w_gate, w_up)


def _expert_down_kernel(hid_ref, wd_ref, o_ref, w_s):
    @pl.when(pl.program_id(2) == 0)
    def _():
        w_s[...] = wd_ref[0].astype(BF16)

    o_ref[0, 0] = _dot(hid_ref[0, 0], w_s[...]).astype(o_ref.dtype)


def _expert_down(hid, w_down, layer):
    bsz, e, capp, ff = hid.shape
    d = w_down.shape[-1]
    tn = _pick_tile(d, 2048)
    return pl.pallas_call(
        _expert_down_kernel,
        grid=(e, d // tn, bsz),
        in_specs=[pl.BlockSpec((1, 1, capp, ff), lambda ei, j, b: (b, ei, 0, 0)),
                  pl.BlockSpec((None, 1, ff, tn), lambda ei, j, b: (layer, ei, 0, j))],
        out_specs=pl.BlockSpec((1, 1, capp, tn), lambda ei, j, b: (b, ei, 0, j)),
        out_shape=_sds((bsz, e, capp, d), BF16),
        scratch_shapes=[pltpu.VMEM((ff, tn), BF16)],
        compiler_params=_cparams("parallel", "parallel", "arbitrary"),
        name="expert_down",
    )(hid, w_down)


COMBINE_TN = 512


def _combine_kernel(*refs, nq):
    ye_ref, slot_t_ref, gv_t_ref = refs[:3]
    o_ref = refs[-1]
    q = pl.program_id(1)
    ei = pl.program_id(2)
    n, e = slot_t_ref.shape[1:]
    capp, d = ye_ref.shape[2:]

    @pl.when(ei == 0)
    def _():
        o_ref[...] = jnp.zeros(o_ref.shape, F32)

    @pl.when(q < nq)
    def _():
        pick = (lax.broadcasted_iota(jnp.int32, (e, LANE), 0) == ei).astype(F32)
        slot_b = _dot(slot_t_ref[0], pick, precision=HI)
        gv_b = _dot(gv_t_ref[0], pick, precision=HI)
        lane = lax.broadcasted_iota(jnp.int32, (1, LANE), 1).astype(F32)
        pt = jnp.concatenate([jnp.where(slot_b == lane + float(c * LANE), 1.0, 0.0).astype(BF16)
                              for c in range(capp // LANE)], axis=1)
        tn = COMBINE_TN
        gv = jnp.concatenate([gv_b] * (tn // LANE), axis=1)
        for j in range(d // tn):
            o_ref[0, :, j * tn:(j + 1) * tn] += _dot(pt, ye_ref[0, 0, :, j * tn:(j + 1) * tn]) * gv


def _combine(ye, slot_t, gv_t, tq, blk0, t, prev=None):
    bsz, e, capp, d = ye.shape
    ntok = slot_t.shape[1]
    nq = ntok // tq
    n_steps = nq if prev is not None else pl.cdiv(t, tq) - blk0
    qc = lambda q: jnp.minimum(q, nq - 1)
    in_specs = [pl.BlockSpec((1, 1, capp, d), lambda b, q, ei: (b, jnp.where(q < nq, ei, e - 1), 0, 0)),
                pl.BlockSpec((1, tq, e), lambda b, q, ei: (b, qc(q), 0)),
                pl.BlockSpec((1, tq, e), lambda b, q, ei: (b, qc(q), 0))]
    args = [ye, slot_t, gv_t]
    aliases = {}
    if prev is not None:
        in_specs.append(pl.BlockSpec(memory_space=pl.ANY))
        args.append(prev)
        aliases = {3: 0}
    return pl.pallas_call(
        functools.partial(_combine_kernel, nq=nq),
        grid=(bsz, n_steps, e),
        in_specs=in_specs,
        out_specs=pl.BlockSpec((1, tq, d), lambda b, q, ei: (b, blk0 + q, 0)),
        out_shape=_sds((bsz, t, d), F32),
        input_output_aliases=aliases,
        compiler_params=_cparams("parallel", "parallel", "arbitrary", vmem=VMEM_LIMIT_BIG),
        name="combine",
    )(*args)


def _round_up(x, m):
    return (x + m - 1) // m * m


def _moe_part(aff_t, xm, blk, ntok, sps, tq, t_out, we, layer, prev=None, gather_src=None):
    e = aff_t.shape[1]
    cap = EC_CAPACITY * ntok // e
    capp = _round_up(cap, LANE)
    slot, slot_t, gv_t, idx = _topk(aff_t, blk, ntok, cap, capp)
    if gather_src is not None:
        assert blk == 0 and cap == capp
        hid = _expert_up_gather(idx, *gather_src, cap, we["gate"], we["up"], layer)
    else:
        hid = _expert_up(slot, xm, blk, ntok, capp, sps, we["gate"], we["up"], layer)
    ye = _expert_down(hid, we["down"], layer)
    return _combine(ye, slot_t, gv_t, tq, blk * ntok // tq, t_out, prev)


def kernel(x, c, ctx, c_ctx, w_mod, b_mod, w_in, b_in, conv_a_w, conv_a_b, lru_wa, lru_ba, lru_wx, lru_bx, lru_lam, gla_wa2, gla_ba, gla_norm_g, conv_c_w, conv_c_b, mlstm_norm_g, w_branch, w_out, ln1_g, ln1_b, w_router, w_e_gate, w_e_up, w_e_down, ln2_g, ln2_b):
    bsz, n_lat, d = x.shape
    nc = ctx.shape[1]
    t = n_lat + nc
    depth = w_mod.shape[0]
    d_rnn = conv_a_w.shape[-1]
    kg, vg = gla_ba.shape[-1], gla_norm_g.shape[-1]
    rank = gla_wa2.shape[2]
    wm = mlstm_norm_g.shape[-1]
    n_gate = 4 * MLSTM_HEADS
    n_exp = w_router.shape[-1]
    alpha = (2 * depth) ** 0.25
    assert 2 * rank + n_gate <= LANE and t % ROW_TILE == 0 and n_lat % ROW_TILE == 0 and bsz < SUBLANE

    sizes = (d_rnn, d_rnn, kg, kg, vg, vg, 2 * rank, wm, wm, wm, wm, n_gate)
    offs = [0]
    for sz in sizes:
        offs.append(offs[-1] + sz)
    n_feat = offs[-1]
    take_main = lambda a: jnp.concatenate([a[..., offs[0]:offs[6]], a[..., offs[7]:offs[11]]], axis=-1)
    take_small = lambda a: jnp.concatenate([a[..., offs[6]:offs[7]], a[..., offs[11]:offs[12]]], axis=-1)
    n_main = (offs[6] - offs[0]) + (offs[11] - offs[7])
    pad_s = LANE - (2 * rank + n_gate)
    mo = {"a_x": 0, "a_g": d_rnn, "b_q": 2 * d_rnn, "b_k": 2 * d_rnn + kg, "b_v": 2 * d_rnn + 2 * kg,
          "b_r": 2 * d_rnn + 2 * kg + vg}
    mo["c_q"] = mo["b_r"] + vg
    mo["c_k"], mo["c_v"], mo["c_o"] = mo["c_q"] + wm, mo["c_q"] + 2 * wm, mo["c_q"] + 3 * wm

    cvec = jnp.concatenate([c, c_ctx[None], jnp.zeros((SUBLANE - bsz - 1, d), F32)], axis=0)
    mod = _mod_all(cvec, w_mod, b_mod)
    xs = jnp.concatenate([x, ctx], axis=1)

    tc = LANE
    while tc < 512 and all(v % (2 * tc) == 0 for v in (offs[6], 4 * wm, d)):
        tc *= 2
    w_in_t = jnp.swapaxes(w_in, 1, 2)
    w_main = _wprep(w_in_t, tc, 0, n_main // tc, ((0, offs[6] // tc, 0), (offs[6] // tc, n_main // tc, offs[7] - offs[6])),
                    "wprep_main")
    w_gate = _wprep(w_in_t, tc, n_main // tc, 3 * d // tc, ((0, 3 * d // tc, n_feat - n_main),), "wprep_gate")
    w_small = jnp.pad(take_small(w_in), ((0, 0), (0, 0), (0, pad_s)))
    w_br = w_branch.astype(BF16)
    w_rt = jnp.swapaxes(w_router, 1, 2).astype(BF16)
    we = {"gate": w_e_gate, "up": w_e_up, "down": w_e_down}

    mod6s = [mod[l].reshape(SUBLANE, 6, d) for l in range(depth)]
    h = _modulate(xs, mod6s[0], 0, n_lat)
    for l in range(depth):
        last = l == depth - 1
        mod6 = mod6s[l]
        b_main = take_main(b_in[l]).reshape(1, n_main)
        b_small = jnp.pad(take_small(b_in[l]), (0, pad_s)).reshape(1, LANE)
        bg = b_in[l][n_feat:].reshape(3, 1, d)
        lru_p = {"conv_w": conv_a_w[l], "conv_b": conv_a_b[l].reshape(1, d_rnn),
                 "wa": lru_wa[l].astype(BF16), "ba": lru_ba[l].reshape(2, 1, d_rnn),
                 "wx": lru_wx[l].astype(BF16), "bx": lru_bx[l].reshape(2, 1, d_rnn),
                 "lam": lru_lam[l].reshape(2, 1, d_rnn)}
        wlr = jnp.zeros((2, LANE, kg), F32)
        for dd in range(2):
            wlr = wlr.at[dd, dd * rank:(dd + 1) * rank].set(gla_wa2[l, dd])
        gla_p = {"wlr": wlr, "ba": gla_ba[l].reshape(2, 1, kg), "norm_g": gla_norm_g[l].reshape(1, vg)}
        ml_p = {"conv_w": conv_c_w[l], "conv_b": conv_c_b[l].reshape(1, 2 * wm), "norm_g": mlstm_norm_g[l].reshape(1, wm)}

        h = h.reshape(bsz * t, d)
        feat = _matmul(h, w_main, l, b_main, F32, 1024, 1024, "feat").reshape(bsz, t, n_main)
        feat_s = _matmul(h, w_small, l, b_small, F32, 1024, LANE, "feat_small").reshape(bsz, t, LANE)
        y0 = _rglru(feat, lru_p, d_rnn, n_lat)
        y1 = _gla(feat, feat_s, gla_p, (mo["b_q"], mo["b_k"], mo["b_v"], mo["b_r"]), n_lat)
        y2 = _mlstm(feat, feat_s, ml_p, (mo["c_q"], mo["c_k"], mo["c_v"], mo["c_o"]), 2 * rank, n_lat)
        ys = [y.reshape(bsz * t, -1) for y in (y0, y1, y2)]
        merged = _merge(h, w_gate, l, bg, ys, w_br)
        y = _matmul(merged, w_out, l, jnp.zeros((1, d), F32), F32, 1024, 512, "out_proj").reshape(bsz, t, d)
        xs, xm = _res_ln(xs, y, mod6, 2, ln1_g[l], ln1_b[l], alpha, n_lat, t, mod6, 3)

        aff_t = _router(xm, w_rt[l])
        t_out = n_lat if last else t
        f = _moe_part(aff_t, xm, 0, n_lat, 1, _pick_tile(n_lat, 1024), t_out, we, l, gather_src=(xs, mod6, 3))
        if not last:
            f = _moe_part(aff_t, xm, n_lat // nc, nc, bsz, nc, t_out, we, l, prev=f)
        if last:
            xs = _res_ln(xs, f, mod6, 5, ln2_g[l], ln2_b[l], alpha, n_lat, t_out)
        else:
            xs, h = _res_ln(xs, f, mod6, 5, ln2_g[l], ln2_b[l], alpha, n_lat, t_out, mod6s[l + 1], 0)
    return xs
```

```python
import functools

import jax
import jax.numpy as jnp
from jax import lax
from jax.experimental import pallas as pl
from jax.experimental.pallas import tpu as pltpu

F32 = jnp.float32
BF16 = jnp.bfloat16
HI = lax.Precision.HIGHEST

GRID_W = 64
CHUNK = 64
LRU_C = 8.0
GLA_HEADS = 4
GLA_TAU = 16.0
MLSTM_HEADS = 4
EC_CAPACITY = 2
LN_EPS = 1e-5

LANE = 128
SUBLANE = 8
ROW_TILE = 256
VMEM_LIMIT = 48 << 20
VMEM_LIMIT_BIG = 56 << 20

NT = (((1,), (1,)), ((), ()))
TN = (((0,), (0,)), ((), ()))


def _cparams(*sem, vmem=VMEM_LIMIT):
    return pltpu.CompilerParams(dimension_semantics=sem, vmem_limit_bytes=vmem)


def _sds(shape, dtype):
    return jax.ShapeDtypeStruct(shape, dtype)


def _dot(a, b, dims=None, precision=None):
    if dims is None:
        return jnp.dot(a, b, preferred_element_type=F32, precision=precision)
    return lax.dot_general(a, b, dims, preferred_element_type=F32, precision=precision)


def _mod_kernel(c_ref, w_ref, b_ref, o_ref):
    a = jax.nn.silu(c_ref[...]).astype(BF16)
    o_ref[0] = _dot(a, w_ref[0].astype(BF16)) + b_ref[0]


def _mod_all(cvec, w_mod, b_mod):
    depth, d, n6 = w_mod.shape
    tn = 512
    return pl.pallas_call(
        _mod_kernel,
        grid=(depth, n6 // tn),
        in_specs=[pl.BlockSpec((SUBLANE, d), lambda l, j: (0, 0)),
                  pl.BlockSpec((1, d, tn), lambda l, j: (l, 0, j)),
                  pl.BlockSpec((1, 1, tn), lambda l, j: (l, 0, j))],
        out_specs=pl.BlockSpec((1, SUBLANE, tn), lambda l, j: (l, 0, j)),
        out_shape=_sds((depth, SUBLANE, n6), F32),
        compiler_params=_cparams("parallel", "parallel"),
        name="mod",
    )(cvec, w_mod, b_mod.reshape(depth, 1, n6))


def _modulate_kernel(x_ref, c_ref, m_ref, o_ref, *, off, n_lat_tiles):
    sh = m_ref[0, off:off + 1, :]
    sc = m_ref[0, off + 1:off + 2, :]
    xin = jnp.where(pl.program_id(1) < n_lat_tiles, x_ref[0], c_ref[0])
    o_ref[0] = (xin * (1.0 + sc) + sh).astype(o_ref.dtype)


def _mod_row_map(n_lat_tiles, batch):
    return lambda b, i: (jnp.where(i < n_lat_tiles, b, batch), 0, 0)


def _two_source_specs(tr, d, n_lat_tiles):
    return [pl.BlockSpec((1, tr, d), lambda b, i: (b, jnp.minimum(i, n_lat_tiles - 1), 0)),
            pl.BlockSpec((1, tr, d), lambda b, i: (b, jnp.maximum(i - n_lat_tiles, 0), 0))]


def _modulate(x, ctx, mod6, off):
    batch, n_lat, d = x.shape
    t = n_lat + ctx.shape[1]
    tr = ROW_TILE
    return pl.pallas_call(
        functools.partial(_modulate_kernel, off=off, n_lat_tiles=n_lat // tr),
        grid=(batch, t // tr),
        in_specs=_two_source_specs(tr, d, n_lat // tr) + [pl.BlockSpec((1, 6, d), _mod_row_map(n_lat // tr, batch))],
        out_specs=pl.BlockSpec((1, tr, d), lambda b, i: (b, i, 0)),
        out_shape=_sds((batch, t, d), BF16),
        compiler_params=_cparams("parallel", "parallel"),
        name="modulate",
    )(x, ctx, mod6)


def _mm_kernel(a_ref, w_ref, b_ref, o_ref):
    o_ref[...] = (_dot(a_ref[...], w_ref[...]) + b_ref[...]).astype(o_ref.dtype)


def _pick_tile(n, pref):
    while pref > LANE and n % pref:
        pref //= 2
    assert n % pref == 0
    return pref


def _mm_castw_kernel(a_ref, w_ref, b_ref, o_ref, w_s):
    @pl.when(pl.program_id(1) == 0)
    def _():
        w_s[...] = w_ref[...].astype(BF16)

    o_ref[...] = (_dot(a_ref[...], w_s[...]) + b_ref[...]).astype(o_ref.dtype)


def _matmul(a, w, layer, bias, out_dtype, tm, tn, name):
    m, k = a.shape
    n = w.shape[2]
    tm, tn = _pick_tile(m, tm), _pick_tile(n, tn)
    cast = w.dtype == F32
    return pl.pallas_call(
        _mm_castw_kernel if cast else _mm_kernel,
        grid=(n // tn, m // tm),
        in_specs=[pl.BlockSpec((tm, k), lambda j, i: (i, 0)),
                  pl.BlockSpec((None, k, tn), lambda j, i: (layer, 0, j)),
                  pl.BlockSpec((1, tn), lambda j, i: (0, j))],
        out_specs=pl.BlockSpec((tm, tn), lambda j, i: (i, j)),
        out_shape=_sds((m, n), out_dtype),
        scratch_shapes=[pltpu.VMEM((k, tn), BF16)] if cast else [],
        compiler_params=_cparams("parallel", "arbitrary" if cast else "parallel"),
        name=name,
    )(a, w, bias)


def _wprep_kernel(a_ref, b_ref, o_ref, *, regions):
    j = pl.program_id(2)
    tc = o_ref.shape[-1]
    for lo, hi, shift in regions:
        @pl.when(jnp.logical_and(j >= lo, j < hi))
        def _():
            if shift == 0:
                rows = a_ref[...]
            else:
                rows = jnp.concatenate([a_ref[...], b_ref[:LANE]], axis=0)[shift:shift + tc]
            o_ref[...] = rows.T.astype(BF16)


def _wprep(w_t, tc, base_blk, n_blk, regions, name):
    depth, _, k = w_t.shape
    assert all(0 <= s < LANE and s % SUBLANE == 0 for _, _, s in regions)
    tr = _pick_tile(k, 1024)
    return pl.pallas_call(
        functools.partial(_wprep_kernel, regions=regions),
        grid=(depth, k // tr, n_blk),
        in_specs=[pl.BlockSpec((None, tc, tr), lambda l, i, j: (l, base_blk + j, i)),
                  pl.BlockSpec((None, tc, tr), lambda l, i, j: (l, base_blk + j + 1, i))],
        out_specs=pl.BlockSpec((None, tr, tc), lambda l, i, j: (l, i, j)),
        out_shape=_sds((depth, k, n_blk * tc), BF16),
        compiler_params=_cparams("parallel", "parallel", "parallel"),
        name=name,
    )(w_t, w_t)


def _res_ln_kernel(*refs, off, alpha, next_off, n_lat_tiles):
    refs = list(refs)
    xin = refs.pop(0)[0]
    if n_lat_tiles is not None:
        xin = jnp.where(pl.program_id(1) < n_lat_tiles, xin, refs.pop(0)[0])
    if next_off is None:
        y_ref, m_ref, g_ref, b_ref, o_ref = refs
    else:
        y_ref, m_ref, g_ref, b_ref, mn_ref, o_ref, on_ref = refs
    gate = m_ref[0, off:off + 1, :]
    z = alpha * xin + gate * y_ref[0]
    mu = jnp.mean(z, axis=-1, keepdims=True)
    zc = z - mu
    var = jnp.mean(zc * zc, axis=-1, keepdims=True)
    out = zc * lax.rsqrt(var + LN_EPS) * g_ref[...] + b_ref[...]
    o_ref[0] = out
    if next_off is not None:
        sh = mn_ref[0, next_off:next_off + 1, :]
        sc = mn_ref[0, next_off + 1:next_off + 2, :]
        on_ref[0] = (out * (1.0 + sc) + sh).astype(on_ref.dtype)


def _res_ln(xs, y, mod6, off, ln_g, ln_b, alpha, n_lat, t_out, next_mod6=None, next_off=None):
    two = isinstance(xs, tuple)
    batch, _, d = (xs[0] if two else xs).shape
    tr = ROW_TILE
    row = pl.BlockSpec((1, tr, d), lambda b, i: (b, i, 0))
    modspec = pl.BlockSpec((1, 6, d), _mod_row_map(n_lat // tr, batch))
    vec = pl.BlockSpec((1, d), lambda b, i: (0, 0))
    in_specs = (_two_source_specs(tr, d, n_lat // tr) if two else [row]) + [row, modspec, vec, vec]
    args = (list(xs) if two else [xs]) + [y, mod6, ln_g.reshape(1, d), ln_b.reshape(1, d)]
    out_specs, out_shape = row, _sds((batch, t_out, d), F32)
    if next_off is not None:
        in_specs.append(modspec)
        args.append(next_mod6)
        out_specs, out_shape = [row, row], [out_shape, _sds((batch, t_out, d), BF16)]
    return pl.pallas_call(
        functools.partial(_res_ln_kernel, off=off, alpha=alpha, next_off=next_off,
                          n_lat_tiles=n_lat // tr if two else None),
        grid=(batch, t_out // tr),
        in_specs=in_specs,
        out_specs=out_specs,
        out_shape=out_shape,
        compiler_params=_cparams("parallel", "parallel"),
        name="res_ln",
    )(*args)


def _gelu_tanh(x):
    return jax.nn.gelu(x, approximate=True)


def _lru_gates(u2, wa_ref, ba_ref, wx_ref, bx_ref, lam_ref):
    ub = u2.astype(BF16)
    r = jax.nn.sigmoid(_dot(ub, wa_ref[0]) + ba_ref[...])
    i = jax.nn.sigmoid(_dot(ub, wx_ref[0]) + bx_ref[...])
    log_a = (-LRU_C * jax.nn.softplus(-lam_ref[...])) * r
    a = jnp.exp(log_a)
    bt = jnp.sqrt(1.0 - jnp.exp(2.0 * log_a)) * (i * u2)
    return a, bt


def _lru_lat_kernel(*refs, rev, n_cg):
    if rev:
        (x_ref, pv_ref, nx_ref, cw_ref, cb_ref, wa_ref, ba_ref, wx_ref, bx_ref, lam_ref, e0_ref,
         ag_ref, hf_ref, _alias, o_ref, a_s, b_s, carry) = refs
    else:
        (x_ref, pv_ref, nx_ref, cw_ref, cb_ref, wa_ref, ba_ref, wx_ref, bx_ref, lam_ref, e0_ref,
         _alias, o_ref, a_s, b_s, carry) = refs
    s = pl.program_id(1)
    cg = (n_cg - 1 - s) if rev else s
    bsz, rows, ncol, cb = x_ref.shape

    @pl.when(s == 0)
    def _():
        carry[...] = e0_ref[...]

    x = x_ref[...]
    col = lax.broadcasted_iota(jnp.int32, (1, ncol, 1), 1)
    not_first = (cg > 0).astype(F32)
    not_last = (cg < n_cg - 1).astype(F32)
    top = jnp.where(col == 0, pltpu.roll(pv_ref[:, SUBLANE - 1], 1, 1) * not_first, pltpu.roll(x[:, rows - 1], 1, 1))
    bot1 = jnp.where(col == ncol - 1, pltpu.roll(nx_ref[:, 0], ncol - 1, 1) * not_last, pltpu.roll(x[:, 0], ncol - 1, 1))
    bot2 = jnp.where(col == ncol - 1, pltpu.roll(nx_ref[:, 1], ncol - 1, 1) * not_last, pltpu.roll(x[:, 1], ncol - 1, 1))
    xe = jnp.concatenate([top[:, None], x, bot1[:, None], bot2[:, None]], axis=1)
    u = cb_ref[...].reshape(1, 1, 1, cb)
    for k in range(4):
        u = u + cw_ref[k:k + 1, :].reshape(1, 1, 1, cb) * xe[:, k:k + rows]
    a, bt = _lru_gates(u.reshape(bsz * rows * ncol, cb), wa_ref, ba_ref, wx_ref, bx_ref, lam_ref)
    a_s[...] = a.reshape(bsz, rows, ncol, cb)
    b_s[...] = bt.reshape(bsz, rows, ncol, cb)

    def body(t, hp):
        h, p = hp
        r = (rows - 1 - t) if rev else t
        a_t = a_s[:, r]
        h = a_t * h + b_s[:, r]
        p = p * a_t
        b_s[:, r] = h
        a_s[:, r] = p
        return h, p

    h_end, p_end = lax.fori_loop(0, rows, body, (jnp.zeros((bsz, ncol, cb), F32), jnp.ones((bsz, ncol, cb), F32)), unroll=8)
    av, bv = p_end, h_end
    sh = 1
    while sh < ncol:
        if rev:
            valid = col < ncol - sh
            amt = ncol - sh
        else:
            valid = col >= sh
            amt = sh
        b_sh = jnp.where(valid, pltpu.roll(bv, amt, 1), 0.0)
        a_sh = jnp.where(valid, pltpu.roll(av, amt, 1), 1.0)
        bv = bv + av * b_sh
        av = av * a_sh
        sh *= 2
    e_prev = carry[...]
    e = bv + av * e_prev
    if rev:
        c_in = jnp.where(col == ncol - 1, e_prev, pltpu.roll(e, ncol - 1, 1))
        carry[...] = jnp.broadcast_to(e[:, 0:1], e.shape)
    else:
        c_in = jnp.where(col == 0, e_prev, pltpu.roll(e, 1, 1))
        carry[...] = jnp.broadcast_to(e[:, ncol - 1:ncol], e.shape)
    h = b_s[...] + a_s[...] * c_in[:, None]
    if rev:
        o_ref[...] = (hf_ref[...] + h) * _gelu_tanh(ag_ref[...])
    else:
        o_ref[...] = h


def _lru_ctx_kernel(*refs, rev):
    if rev:
        (x_ref, cw_ref, cb_ref, wa_ref, ba_ref, wx_ref, bx_ref, lam_ref, ag_ref, hf_ref, _base, o_ref, e_ref) = refs
    else:
        (x_ref, cw_ref, cb_ref, wa_ref, ba_ref, wx_ref, bx_ref, lam_ref, _base, o_ref, e_ref) = refs
    bsz, nc, cb = x_ref.shape
    x = x_ref[...]
    t = lax.broadcasted_iota(jnp.int32, (1, nc, 1), 1)
    xm1 = jnp.where(t >= 1, pltpu.roll(x, 1, 1), 0.0)
    xp1 = jnp.where(t < nc - 1, pltpu.roll(x, nc - 1, 1), 0.0)
    xp2 = jnp.where(t < nc - 2, pltpu.roll(x, nc - 2, 1), 0.0)
    w = [cw_ref[k:k + 1, :].reshape(1, 1, cb) for k in range(4)]
    u = w[0] * xm1 + w[1] * x + w[2] * xp1 + w[3] * xp2 + cb_ref[...].reshape(1, 1, cb)
    a, bt = _lru_gates(u.reshape(bsz * nc, cb), wa_ref, ba_ref, wx_ref, bx_ref, lam_ref)
    av = a.reshape(bsz, nc, cb)
    bv = bt.reshape(bsz, nc, cb)
    sh = 1
    while sh < nc:
        if rev:
            valid = t < nc - sh
            amt = nc - sh
        else:
            valid = t >= sh
            amt = sh
        b_sh = jnp.where(valid, pltpu.roll(bv, amt, 1), 0.0)
        a_sh = jnp.where(valid, pltpu.roll(av, amt, 1), 1.0)
        bv = bv + av * b_sh
        av = av * a_sh
        sh *= 2
    if rev:
        o_ref[...] = (hf_ref[...] + bv) * _gelu_tanh(ag_ref[...])
        e_ref[...] = jnp.broadcast_to(bv[:, 0:1], (bsz, SUBLANE, cb))
    else:
        o_ref[...] = bv
        e_ref[...] = jnp.broadcast_to(bv[:, nc - 1:nc], (bsz, SUBLANE, cb))


def _rglru(feat, p, d_rnn, n_lat):
    bsz, t, nf = feat.shape
    nc = t - n_lat
    nb, bs, _ = p["wa"][0].shape
    cb = bs
    assert d_rnn == nb * bs and cb % LANE == 0 and n_lat % nc == 0 and nc % GRID_W == 0
    rows = n_lat // GRID_W
    assert rows % SUBLANE == 0 and GRID_W % SUBLANE == 0
    n_cg = GRID_W // SUBLANE
    ag_off = d_rnn // cb
    feat4 = feat.reshape(bsz, t // GRID_W, GRID_W, nf)
    ctx_blk = n_lat // nc

    def wspecs(im):
        return [pl.BlockSpec((4, cb), im(lambda j: (0, j))),
                pl.BlockSpec((1, cb), im(lambda j: (0, j))),
                pl.BlockSpec((1, bs, bs), im(lambda j: (j, 0, 0))),
                pl.BlockSpec((1, cb), im(lambda j: (0, j))),
                pl.BlockSpec((1, bs, bs), im(lambda j: (j, 0, 0))),
                pl.BlockSpec((1, cb), im(lambda j: (0, j))),
                pl.BlockSpec((1, cb), im(lambda j: (0, j)))]

    def wargs(d):
        return [p["conv_w"], p["conv_b"], p["wa"][d], p["ba"][d], p["wx"][d], p["bx"][d], p["lam"][d]]

    im1 = lambda f: (lambda j: f(j))
    im2 = lambda f: (lambda j, s: f(j))
    hf = None
    out = None
    for rev in (False, True):
        in_specs = [pl.BlockSpec((bsz, nc, cb), lambda j: (0, ctx_blk, j))] + wspecs(im1)
        args = [feat] + wargs(int(rev))
        if rev:
            in_specs += [pl.BlockSpec((bsz, nc, cb), lambda j: (0, ctx_blk, ag_off + j)),
                         pl.BlockSpec((bsz, nc, cb), lambda j: (0, ctx_blk, j))]
            args += [feat, hf]
        in_specs += [pl.BlockSpec(memory_space=pl.ANY)]
        args += [jnp.zeros((bsz, t, d_rnn), F32)]
        part, e0 = pl.pallas_call(
            functools.partial(_lru_ctx_kernel, rev=rev),
            grid=(nb,),
            in_specs=in_specs,
            out_specs=[pl.BlockSpec((bsz, nc, cb), lambda j: (0, ctx_blk, j)),
                       pl.BlockSpec((bsz, SUBLANE, cb), lambda j: (0, 0, j))],
            out_shape=[_sds((bsz, t, d_rnn), F32), _sds((bsz, SUBLANE, d_rnn), F32)],
            input_output_aliases={len(args) - 1: 0},
            compiler_params=_cparams("parallel"),
            name="lru_ctx_bwd" if rev else "lru_ctx_fwd",
        )(*args)
        cgm = (lambda s: n_cg - 1 - s) if rev else (lambda s: s)
        in_specs = [pl.BlockSpec((bsz, rows, SUBLANE, cb), lambda j, s: (0, 0, cgm(s), j)),
                    pl.BlockSpec((bsz, SUBLANE, SUBLANE, cb), lambda j, s: (0, rows // SUBLANE - 1, jnp.maximum(cgm(s) - 1, 0), j)),
                    pl.BlockSpec((bsz, SUBLANE, SUBLANE, cb), lambda j, s: (0, 0, jnp.minimum(cgm(s) + 1, n_cg - 1), j))]
        in_specs += wspecs(im2) + [pl.BlockSpec((bsz, SUBLANE, cb), lambda j, s: (0, 0, j))]
        args = [feat4, feat4, feat4] + wargs(int(rev)) + [e0]
        if rev:
            in_specs += [pl.BlockSpec((bsz, rows, SUBLANE, cb), lambda j, s: (0, 0, cgm(s), ag_off + j)),
                         pl.BlockSpec((bsz, rows, SUBLANE, cb), lambda j, s: (0, 0, cgm(s), j))]
            args += [feat4, hf.reshape(bsz, t // GRID_W, GRID_W, d_rnn)]
        in_specs += [pl.BlockSpec(memory_space=pl.ANY)]
        args += [part.reshape(bsz, t // GRID_W, GRID_W, d_rnn)]
        res = pl.pallas_call(
            functools.partial(_lru_lat_kernel, rev=rev, n_cg=n_cg),
            grid=(nb, n_cg),
            in_specs=in_specs,
            out_specs=pl.BlockSpec((bsz, rows, SUBLANE, cb), lambda j, s: (0, 0, cgm(s), j)),
            out_shape=_sds((bsz, t // GRID_W, GRID_W, d_rnn), F32),
            scratch_shapes=[pltpu.VMEM((bsz, rows, SUBLANE, cb), F32),
                            pltpu.VMEM((bsz, rows, SUBLANE, cb), F32),
                            pltpu.VMEM((bsz, SUBLANE, cb), F32)],
            input_output_aliases={len(args) - 1: 0},
            compiler_params=_cparams("parallel", "arbitrary"),
            name="lru_lat_bwd" if rev else "lru_lat_fwd",
        )(*args)
        res = res.reshape(bsz, t, d_rnn)
        if rev:
            out = res
        else:
            hf = res
    return out


def _chunk_of(s, n_ch, n_lat_ch, rev):
    if rev:
        return n_ch - 1 - s
    return jnp.where(s < n_ch - n_lat_ch, s + n_lat_ch, s - (n_ch - n_lat_ch))


MIXER_SPS = 4


def _mixer_sps(bsz):
    sps = MIXER_SPS
    while bsz % sps:
        sps //= 2
    return sps


def _head_norm(o, g):
    mu = jnp.mean(o, axis=-1, keepdims=True)
    oc = o - mu
    var = jnp.mean(oc * oc, axis=-1, keepdims=True)
    return oc * lax.rsqrt(var + LN_EPS) * g


def _gla_kernel(*refs, rev, heads):
    if rev:
        (q_ref, k_ref, v_ref, lr_ref, wlr_ref, ba_ref, tri_ref, rsel_ref, r_ref, of_ref, ng_ref, o_ref, st) = refs
    else:
        (q_ref, k_ref, v_ref, lr_ref, wlr_ref, ba_ref, tri_ref, rsel_ref, o_ref, st) = refs
    s = pl.program_id(1)

    @pl.when(s == 0)
    def _():
        st[...] = jnp.zeros(st.shape, F32)

    dk = q_ref.shape[-1] // heads
    dv = v_ref.shape[-1] // heads
    tri = tri_ref[...]
    for i in range(q_ref.shape[0]):
        g = jax.nn.log_sigmoid(_dot(lr_ref[i], wlr_ref[...], precision=HI) + ba_ref[...]) / GLA_TAU
        bcum = _dot(tri, g, precision=HI)
        bm = _dot(rsel_ref[...], g, precision=HI)
        q = q_ref[i] * dk ** -0.5
        k = k_ref[i]
        v = v_ref[i]
        for h in range(heads):
            sk = slice(h * dk, (h + 1) * dk)
            sv = slice(h * dv, (h + 1) * dv)
            qh, kh, bh = q[:, sk], k[:, sk], bcum[:, sk]
            vh = v[:, sv].astype(BF16)
            bmid, blast = bm[0:1, sk], bm[1:2, sk]
            att = _dot((qh * jnp.exp(bh - bmid)).astype(BF16), (kh * jnp.exp(bmid - bh)).astype(BF16), NT) * tri
            sth = st[i, h]
            o = _dot(att.astype(BF16), vh) + _dot((qh * jnp.exp(bh)).astype(BF16), sth.astype(BF16), NT)
            st[i, h] = jnp.exp(blast) * sth + _dot(vh, (kh * jnp.exp(blast - bh)).astype(BF16), TN)
            if rev:
                o = _head_norm(of_ref[i, :, sv] + o, ng_ref[:, sv]) * jax.nn.silu(r_ref[i, :, sv])
            o_ref[i, :, sv] = o.astype(o_ref.dtype)


def _scan_consts(rev):
    i = jnp.arange(CHUNK)
    tri = (i[None, :] >= i[:, None]) if rev else (i[None, :] <= i[:, None])
    tri = tri.astype(F32)
    mid = CHUNK // 2 if rev else CHUNK // 2 - 1
    last = 0 if rev else CHUNK - 1
    rsel = jnp.zeros((SUBLANE, CHUNK), F32).at[0].set(tri[mid]).at[1].set(tri[last])
    return tri, rsel


def _gla(feat, feat_s, p, offs, n_lat):
    bsz, t, _ = feat.shape
    kg, vg = p["ba"][0].shape[-1], p["norm_g"].shape[-1]
    oq, ok, ov, orr = offs
    assert oq % kg == 0 and ok % kg == 0 and ov % vg == 0 and orr % vg == 0
    n_ch, n_lat_ch = t // CHUNK, n_lat // CHUNK
    sps = _mixer_sps(bsz)
    of = None
    for rev in (False, True):
        d = int(rev)
        tri, rsel = _scan_consts(rev)
        cm = lambda b, s: (b, _chunk_of(s, n_ch, n_lat_ch, rev))
        const = lambda b, s: (0, 0)
        in_specs = [pl.BlockSpec((sps, CHUNK, kg), lambda b, s: cm(b, s) + (oq // kg,)),
                    pl.BlockSpec((sps, CHUNK, kg), lambda b, s: cm(b, s) + (ok // kg,)),
                    pl.BlockSpec((sps, CHUNK, vg), lambda b, s: cm(b, s) + (ov // vg,)),
                    pl.BlockSpec((sps, CHUNK, LANE), lambda b, s: cm(b, s) + (0,)),
                    pl.BlockSpec((LANE, kg), const),
                    pl.BlockSpec((1, kg), const),
                    pl.BlockSpec((CHUNK, CHUNK), const),
                    pl.BlockSpec((SUBLANE, CHUNK), const)]
        args = [feat, feat, feat, feat_s, p["wlr"][d], p["ba"][d], tri, rsel]
        if rev:
            in_specs += [pl.BlockSpec((sps, CHUNK, vg), lambda b, s: cm(b, s) + (orr // vg,)),
                         pl.BlockSpec((sps, CHUNK, vg), lambda b, s: cm(b, s) + (0,)),
                         pl.BlockSpec((1, vg), const)]
            args += [feat, of, p["norm_g"]]
        res = pl.pallas_call(
            functools.partial(_gla_kernel, rev=rev, heads=GLA_HEADS),
            grid=(bsz // sps, n_ch),
            in_specs=in_specs,
            out_specs=pl.BlockSpec((sps, CHUNK, vg), lambda b, s: cm(b, s) + (0,)),
            out_shape=_sds((bsz, t, vg), BF16 if rev else F32),
            scratch_shapes=[pltpu.VMEM((sps, GLA_HEADS, vg // GLA_HEADS, kg // GLA_HEADS), F32)],
            compiler_params=_cparams("parallel", "arbitrary"),
            name="gla_bwd" if rev else "gla_fwd",
        )(*args)
        of = res
    return of


def _conv_rows(x, prev_row, next_rows, w_ref, b_ref, lo, hi):
    n = x.shape[0]
    t = lax.broadcasted_iota(jnp.int32, (n, 1), 0)
    xm1 = jnp.where(t == 0, prev_row, pltpu.roll(x, 1, 0))
    xp1 = jnp.where(t == n - 1, next_rows[0:1], pltpu.roll(x, n - 1, 0))
    xp2 = jnp.where(t == n - 2, next_rows[0:1], jnp.where(t == n - 1, next_rows[1:2], pltpu.roll(x, n - 2, 0)))
    w = w_ref[:, lo:hi]
    return w[0:1] * xm1 + w[1:2] * x + w[2:3] * xp1 + w[3:4] * xp2 + b_ref[:, lo:hi]


def _mlstm_kernel(*refs, rev, heads, n_ch, n_lat_ch, g_off):
    if rev:
        (q_ref, qp_ref, qn_ref, k_ref, kp_ref, kn_ref, v_ref, cg_ref, cw_ref, cbias_ref, tri_ref, trit_ref,
         og_ref, hf_ref, ng_ref, o_ref, c_s, n_s, m_s) = refs
    else:
        (q_ref, qp_ref, qn_ref, k_ref, kp_ref, kn_ref, v_ref, cg_ref, cw_ref, cbias_ref, tri_ref, trit_ref,
         o_ref, c_s, n_s, m_s) = refs
    s = pl.program_id(1)

    @pl.when(s == 0)
    def _():
        c_s[...] = jnp.zeros(c_s.shape, F32)
        n_s[...] = jnp.zeros(n_s.shape, F32)
        m_s[...] = jnp.zeros(m_s.shape, F32)

    ch = _chunk_of(s, n_ch, n_lat_ch, rev)
    not_first = jnp.logical_and(ch != 0, ch != n_lat_ch).astype(F32)
    not_last = jnp.logical_and(ch != n_lat_ch - 1, ch != n_ch - 1).astype(F32)
    wm = q_ref.shape[-1]
    dh = wm // heads
    tri = tri_ref[...]
    ci0 = g_off + int(rev) * 2 * heads
    cf0 = ci0 + heads
    lane = lax.broadcasted_iota(jnp.int32, (SUBLANE, LANE), 1)
    row = lax.broadcasted_iota(jnp.int32, (SUBLANE, LANE), 0)
    sel_i = (lane == row + ci0).astype(F32)
    sel_f = (lane == row + cf0).astype(F32)
    last = 0 if rev else CHUNK - 1
    for i in range(q_ref.shape[0]):
        qc = jax.nn.silu(_conv_rows(q_ref[i], qp_ref[i, SUBLANE - 1:SUBLANE] * not_first, qn_ref[i, 0:2] * not_last,
                                    cw_ref, cbias_ref, 0, wm))
        kc = jax.nn.silu(_conv_rows(k_ref[i], kp_ref[i, SUBLANE - 1:SUBLANE] * not_first, kn_ref[i, 0:2] * not_last,
                                    cw_ref, cbias_ref, wm, 2 * wm)) * dh ** -0.5
        v = v_ref[i]
        gts = cg_ref[i]
        gls = jax.nn.log_sigmoid(gts)
        bcol_all = _dot(tri, gls, precision=HI)
        ig_rows = _dot(sel_i, gts, NT, precision=HI)
        b_rows = _dot(_dot(sel_f, gls, NT, precision=HI), trit_ref[...], precision=HI)
        for h in range(heads):
            sl = slice(h * dh, (h + 1) * dh)
            qh = qc[:, sl]
            qb, kb, vh = qh.astype(BF16), kc[:, sl].astype(BF16), v[:, sl]
            bc = bcol_all[:, cf0 + h:cf0 + h + 1]
            igc = gts[:, ci0 + h:ci0 + h + 1]
            m = m_s[i, h:h + 1, 0:1]
            dmat = jnp.where(tri > 0.0, bc - b_rows[h:h + 1, :] + ig_rows[h:h + 1, :], -jnp.inf)
            inter = bc + m
            m_row = jnp.maximum(jnp.max(dmat, axis=-1, keepdims=True), inter)
            pmat = _dot(qb, kb, NT) * jnp.exp(dmat - m_row)
            s_inter = jnp.exp(inter - m_row)
            cm = c_s[i, h]
            nv = n_s[i, h:h + 1, :]
            num = _dot(pmat.astype(BF16), vh.astype(BF16)) + s_inter * _dot(qb, cm.astype(BF16), NT)
            den = jnp.sum(pmat, axis=-1, keepdims=True) + s_inter * jnp.sum(qh * nv, axis=-1, keepdims=True)
            hout = num / jnp.maximum(jnp.abs(den), jnp.exp(-m_row))
            b_last = bc[last:last + 1]
            wl = b_last - bc + igc
            m_new = jnp.maximum(b_last + m, jnp.max(wl, axis=0, keepdims=True))
            sw = jnp.exp(wl - m_new)
            decay = jnp.exp(b_last + m - m_new)
            c_s[i, h] = decay * cm + _dot((sw * vh).astype(BF16), kb, TN)
            n_s[i, h:h + 1, :] = decay * nv + jnp.sum(sw * kc[:, sl], axis=0, keepdims=True)
            m_s[i, h:h + 1, :] = jnp.broadcast_to(m_new, (1, LANE))
            if rev:
                hout = _head_norm(hf_ref[i, :, sl] + hout, ng_ref[:, sl]) * jax.nn.sigmoid(og_ref[i, :, sl])
            o_ref[i, :, sl] = hout.astype(o_ref.dtype)


def _mlstm(feat, feat_s, p, offs, g_off, n_lat):
    bsz, t, _ = feat.shape
    wm = p["norm_g"].shape[-1]
    oq, ok, ov, oo = offs
    assert all(o % wm == 0 for o in offs)
    n_ch, n_lat_ch = t // CHUNK, n_lat // CHUNK
    n_r8 = t // SUBLANE
    per = CHUNK // SUBLANE
    sps = _mixer_sps(bsz)
    hf = None
    for rev in (False, True):
        tri, _ = _scan_consts(rev)
        chm = lambda s: _chunk_of(s, n_ch, n_lat_ch, rev)
        const = lambda b, s: (0, 0)

        def cur(off):
            return pl.BlockSpec((sps, CHUNK, wm), lambda b, s: (b, chm(s), off // wm))

        def prv(off):
            return pl.BlockSpec((sps, SUBLANE, wm), lambda b, s: (b, jnp.maximum(chm(s) * per - 1, 0), off // wm))

        def nxt(off):
            return pl.BlockSpec((sps, SUBLANE, wm), lambda b, s: (b, jnp.minimum((chm(s) + 1) * per, n_r8 - 1), off // wm))

        in_specs = [cur(oq), prv(oq), nxt(oq), cur(ok), prv(ok), nxt(ok), cur(ov),
                    pl.BlockSpec((sps, CHUNK, LANE), lambda b, s: (b, chm(s), 0)),
                    pl.BlockSpec((4, 2 * wm), const),
                    pl.BlockSpec((1, 2 * wm), const),
                    pl.BlockSpec((CHUNK, CHUNK), const),
                    pl.BlockSpec((CHUNK, CHUNK), const)]
        args = [feat] * 7 + [feat_s, p["conv_w"], p["conv_b"], tri, tri.T]
        if rev:
            in_specs += [cur(oo), pl.BlockSpec((sps, CHUNK, wm), lambda b, s: (b, chm(s), 0)), pl.BlockSpec((1, wm), const)]
            args += [feat, hf, p["norm_g"]]
        dh = wm // MLSTM_HEADS
        hf = pl.pallas_call(
            functools.partial(_mlstm_kernel, rev=rev, heads=MLSTM_HEADS, n_ch=n_ch, n_lat_ch=n_lat_ch, g_off=g_off),
            grid=(bsz // sps, n_ch),
            in_specs=in_specs,
            out_specs=pl.BlockSpec((sps, CHUNK, wm), lambda b, s: (b, chm(s), 0)),
            out_shape=_sds((bsz, t, wm), BF16 if rev else F32),
            scratch_shapes=[pltpu.VMEM((sps, MLSTM_HEADS, dh, dh), F32),
                            pltpu.VMEM((sps, SUBLANE, dh), F32),
                            pltpu.VMEM((sps, SUBLANE, LANE), F32)],
            compiler_params=_cparams("parallel", "arbitrary"),
            name="mlstm_bwd" if rev else "mlstm_fwd",
        )(*args)
    return hf


def _merge_kernel(h_ref, wg0_ref, wg1_ref, wg2_ref, bg_ref, y0_ref, y1_ref, y2_ref, wb_ref, o_ref):
    h = h_ref[...]
    acc = None
    for n, (wg_ref, y_ref) in enumerate(((wg0_ref, y0_ref), (wg1_ref, y1_ref), (wg2_ref, y2_ref))):
        g = jax.nn.sigmoid(_dot(h, wg_ref[...]) + bg_ref[n])
        term = g * _dot(y_ref[...].astype(BF16), wb_ref[n])
        acc = term if acc is None else acc + term
    o_ref[...] = acc.astype(o_ref.dtype)


def _merge(h, w_gate, layer, bg, ys, wb):
    m, d = h.shape
    r = ys[0].shape[1]
    tm, tn = _pick_tile(m, 512), _pick_tile(d, 512)
    nj = d // tn
    wspecs = [pl.BlockSpec((None, d, tn), lambda j, i, n=n: (layer, 0, n * nj + j)) for n in range(3)]
    yspec = pl.BlockSpec((tm, r), lambda j, i: (i, 0))
    return pl.pallas_call(
        _merge_kernel,
        grid=(nj, m // tm),
        in_specs=[pl.BlockSpec((tm, d), lambda j, i: (i, 0))] + wspecs +
                 [pl.BlockSpec((3, 1, tn), lambda j, i: (0, 0, j)),
                  yspec, yspec, yspec,
                  pl.BlockSpec((None, 3, r, tn), lambda j, i: (layer, 0, 0, j))],
        out_specs=pl.BlockSpec((tm, tn), lambda j, i: (i, j)),
        out_shape=_sds((m, d), BF16),
        compiler_params=_cparams("parallel", "parallel", vmem=VMEM_LIMIT_BIG),
        name="merge",
    )(h, w_gate, w_gate, w_gate, bg, *ys, wb)


def _router_kernel(x_ref, w_ref, o_ref):
    logits = _dot(w_ref[...], x_ref[0], NT)
    ex = jnp.exp(logits - jnp.max(logits, axis=0, keepdims=True))
    o_ref[0] = ex / jnp.sum(ex, axis=0, keepdims=True)


def _router(xm, w_rt):
    bsz, t, d = xm.shape
    e = w_rt.shape[0]
    tr = ROW_TILE
    return pl.pallas_call(
        _router_kernel,
        grid=(bsz, t // tr),
        in_specs=[pl.BlockSpec((1, tr, d), lambda b, i: (b, i, 0)),
                  pl.BlockSpec((e, d), lambda b, i: (0, 0))],
        out_specs=pl.BlockSpec((1, e, tr), lambda b, i: (b, 0, i)),
        out_shape=_sds((bsz, e, t), F32),
        compiler_params=_cparams("parallel", "parallel"),
        name="router",
    )(xm, w_rt)


def _prefix_excl(src_ref, dst_ref, upper):
    e, n = src_ref.shape
    off = jnp.zeros((e, 1), F32)
    for kb in range(n // LANE):
        blk = src_ref[:, kb * LANE:(kb + 1) * LANE]
        inc = _dot(blk.astype(BF16), upper)
        dst_ref[:, kb * LANE:(kb + 1) * LANE] = inc - blk + off
        off = off + inc[:, LANE - 1:LANE]


TOKEN_SPLIT_BITS = 6
TOKEN_SPLIT = 1 << TOKEN_SPLIT_BITS


def _topk_kernel(a_ref, slot_ref, slot_t_ref, gv_t_ref, idx_ref, m_s, r_s, *, cap):
    aff = a_ref[0]
    e, n = aff.shape
    bits = pltpu.bitcast(aff, jnp.int32)
    thr = jnp.zeros((e, 1), jnp.int32)
    for bit in range(30, -1, -1):
        cand = thr | (1 << bit)
        cnt = jnp.sum((bits >= cand).astype(jnp.int32), axis=1, keepdims=True)
        thr = jnp.where(cnt >= cap, cand, thr)
    gt = (bits > thr).astype(F32)
    eq = (bits == thr).astype(F32)
    need = cap - jnp.sum(gt, axis=1, keepdims=True)
    ii = lax.broadcasted_iota(jnp.int32, (LANE, LANE), 0)
    jj = lax.broadcasted_iota(jnp.int32, (LANE, LANE), 1)
    upper = (ii <= jj).astype(BF16)
    m_s[...] = eq
    _prefix_excl(m_s, r_s, upper)
    sel = gt + eq * (r_s[...] < need).astype(F32)
    m_s[...] = sel
    _prefix_excl(m_s, r_s, upper)
    slot = jnp.where(sel > 0.0, r_s[...], -1.0)
    slot_ref[0] = slot
    eye = (lax.broadcasted_iota(jnp.int32, (e, e), 0) == lax.broadcasted_iota(jnp.int32, (e, e), 1)).astype(F32)
    slot_t_ref[0] = _dot(slot, eye, TN, precision=HI)
    gv_t_ref[0] = _dot(aff, eye, TN, precision=HI)
    capp = idx_ref.shape[-1]
    tok = lax.broadcasted_iota(jnp.int32, (SUBLANE, n), 1)
    part = lax.broadcasted_iota(jnp.int32, (SUBLANE, n), 0)
    tvals = jnp.where(part == 0, tok >> TOKEN_SPLIT_BITS, tok & (TOKEN_SPLIT - 1)).astype(F32).astype(BF16)
    sidx = lax.broadcasted_iota(jnp.int32, (capp, n), 0).astype(F32)
    for ei in range(e):
        onehot = jnp.where(slot[ei:ei + 1, :] == sidx, 1.0, 0.0).astype(BF16)
        parts = _dot(tvals, onehot, NT)
        idx_ref[0, ei:ei + 1, :] = (parts[0:1] * TOKEN_SPLIT + parts[1:2]).astype(jnp.int32)


def _topk(aff_t, blk, ntok, cap, capp):
    bsz, e, _ = aff_t.shape
    assert ntok <= TOKEN_SPLIT * 256
    return pl.pallas_call(
        functools.partial(_topk_kernel, cap=cap),
        grid=(bsz,),
        in_specs=[pl.BlockSpec((1, e, ntok), lambda b: (b, 0, blk))],
        out_specs=[pl.BlockSpec((1, e, ntok), lambda b: (b, 0, 0)),
                   pl.BlockSpec((1, ntok, e), lambda b: (b, 0, 0)),
                   pl.BlockSpec((1, ntok, e), lambda b: (b, 0, 0)),
                   pl.BlockSpec((1, e, capp), lambda b: (b, 0, 0))],
        out_shape=[_sds((bsz, e, ntok), F32), _sds((bsz, ntok, e), F32), _sds((bsz, ntok, e), F32),
                   _sds((bsz, e, capp), jnp.int32)],
        scratch_shapes=[pltpu.VMEM((e, ntok), F32), pltpu.VMEM((e, ntok), F32)],
        compiler_params=_cparams("parallel"),
        name="topk",
    )(aff_t)


def _expert_up_kernel(slot_ref, x_ref, wg_ref, wu_ref, o_ref, p_s, g_s, u_s):
    k = pl.program_id(2)
    sps, capp, n = p_s.shape

    @pl.when(k == 0)
    def _():
        sidx = lax.broadcasted_iota(jnp.int32, (capp, n), 0).astype(F32)
        for i in range(sps):
            srow = slot_ref[i, pl.ds(pl.program_id(1), 1), :]
            p_s[i] = jnp.where(srow == sidx, 1.0, 0.0).astype(BF16)
        g_s[...] = jnp.zeros(g_s.shape, F32)
        u_s[...] = jnp.zeros(u_s.shape, F32)

    xg = jnp.concatenate([_dot(p_s[i], x_ref[i]).astype(BF16) for i in range(sps)], axis=0)
    g_s[...] += _dot(xg, wg_ref[0].astype(BF16))
    u_s[...] += _dot(xg, wu_ref[0].astype(BF16))

    @pl.when(k == pl.num_programs(2) - 1)
    def _():
        hid = (jax.nn.silu(g_s[...]) * u_s[...]).astype(o_ref.dtype)
        for i in range(sps):
            o_ref[i, 0] = hid[i * capp:(i + 1) * capp]


def _expert_up(slot, xm, blk, ntok, capp, sps, w_gate, w_up, layer):
    bsz, e = slot.shape[:2]
    d, ff = w_gate.shape[2:]
    tk = 512
    return pl.pallas_call(
        _expert_up_kernel,
        grid=(bsz // sps, e, d // tk),
        in_specs=[pl.BlockSpec((sps, e, ntok), lambda b, ei, k: (b, 0, 0)),
                  pl.BlockSpec((sps, ntok, tk), lambda b, ei, k: (b, blk, k)),
                  pl.BlockSpec((None, 1, tk, ff), lambda b, ei, k: (layer, ei, k, 0)),
                  pl.BlockSpec((None, 1, tk, ff), lambda b, ei, k: (layer, ei, k, 0))],
        out_specs=pl.BlockSpec((sps, 1, capp, ff), lambda b, ei, k: (b, ei, 0, 0)),
        out_shape=_sds((bsz, e, capp, ff), BF16),
        scratch_shapes=[pltpu.VMEM((sps, capp, ntok), BF16), pltpu.VMEM((sps * capp, ff), F32),
                        pltpu.VMEM((sps * capp, ff), F32)],
        compiler_params=_cparams("parallel", "parallel", "arbitrary"),
        name="expert_up",
    )(slot, xm, w_gate, w_up)


def _expert_up_gather_kernel(idx_ref, xs_hbm, m_ref, wg_ref, wu_ref, o_ref, x32_s, xb_s, g_s, u_s, sem, *, off, n_exp):
    bg, ei, k = pl.program_id(0), pl.program_id(1), pl.program_id(2)
    sps, cap, _ = x32_s.shape
    tk = wg_ref.shape[1]

    def row_copy(i, s, row):
        return pltpu.make_async_copy(xs_hbm.at[bg * sps + i, pl.ds(row, 1), :], x32_s.at[i, pl.ds(s, 1), :], sem.at[i])

    @pl.when(k == 0)
    def _():
        for i in range(sps):
            def issue(s, carry, i=i):
                row_copy(i, s, idx_ref[(bg * sps + i) * n_exp + ei, s]).start()
                return carry

            lax.fori_loop(0, cap, issue, 0, unroll=8)
        for i in range(sps):
            def wait(s, carry, i=i):
                row_copy(i, s, 0).wait()
                return carry

            lax.fori_loop(0, cap, wait, 0, unroll=8)
            sh = m_ref[i, off:off + 1, :]
            sc = m_ref[i, off + 1:off + 2, :]
            xb_s[i * cap:(i + 1) * cap, :] = (x32_s[i] * (1.0 + sc) + sh).astype(BF16)
        g_s[...] = jnp.zeros(g_s.shape, F32)
        u_s[...] = jnp.zeros(u_s.shape, F32)

    xk = xb_s[:, pl.ds(pl.multiple_of(k * tk, tk), tk)]
    g_s[...] += _dot(xk, wg_ref[0].astype(BF16))
    u_s[...] += _dot(xk, wu_ref[0].astype(BF16))

    @pl.when(k == pl.num_programs(2) - 1)
    def _():
        hid = (jax.nn.silu(g_s[...]) * u_s[...]).astype(o_ref.dtype)
        for i in range(sps):
            o_ref[i, 0] = hid[i * cap:(i + 1) * cap]


GATHER_SPS = 2


def _expert_up_gather(idx, xs, mod6, off, cap, w_gate, w_up, layer):
    bsz, e, _ = idx.shape
    d, ff = w_gate.shape[2:]
    tk = 512
    sps = GATHER_SPS if bsz % GATHER_SPS == 0 else 1
    grid_spec = pltpu.PrefetchScalarGridSpec(
        num_scalar_prefetch=1,
        grid=(bsz // sps, e, d // tk),
        in_specs=[pl.BlockSpec(memory_space=pl.ANY),
                  pl.BlockSpec((sps, 6, d), lambda b, ei, k, idx_ref: (b, 0, 0)),
                  pl.BlockSpec((None, 1, tk, ff), lambda b, ei, k, idx_ref: (layer, ei, k, 0)),
                  pl.BlockSpec((None, 1, tk, ff), lambda b, ei, k, idx_ref: (layer, ei, k, 0))],
        out_specs=pl.BlockSpec((sps, 1, cap, ff), lambda b, ei, k, idx_ref: (b, ei, 0, 0)),
        scratch_shapes=[pltpu.VMEM((sps, cap, d), F32), pltpu.VMEM((sps * cap, d), BF16),
                        pltpu.VMEM((sps * cap, ff), F32), pltpu.VMEM((sps * cap, ff), F32),
                        pltpu.SemaphoreType.DMA((sps,))])
    return pl.pallas_call(
        functools.partial(_expert_up_gather_kernel, off=off, n_exp=e),
        grid_spec=grid_spec,
        out_shape=_sds((bsz, e, cap, ff), BF16),
        compiler_params=_cparams("parallel", "parallel", "arbitrary"),
        name="expert_up_gather",
    )(idx.reshape(bsz * e, idx.shape[-1]), xs, mod6, w_gate, w_up)


def _expert_down_kernel(hid_ref, wd_ref, o_ref, w_s):
    @pl.when(pl.program_id(2) == 0)
    def _():
        w_s[...] = wd_ref[0].astype(BF16)

    o_ref[0, 0] = _dot(hid_ref[0, 0], w_s[...]).astype(o_ref.dtype)


def _expert_down(hid, w_down, layer):
    bsz, e, capp, ff = hid.shape
    d = w_down.shape[-1]
    tn = _pick_tile(d, 2048)
    return pl.pallas_call(
        _expert_down_kernel,
        grid=(e, d // tn, bsz),
        in_specs=[pl.BlockSpec((1, 1, capp, ff), lambda ei, j, b: (b, ei, 0, 0)),
                  pl.BlockSpec((None, 1, ff, tn), lambda ei, j, b: (layer, ei, 0, j))],
        out_specs=pl.BlockSpec((1, 1, capp, tn), lambda ei, j, b: (b, ei, 0, j)),
        out_shape=_sds((bsz, e, capp, d), BF16),
        scratch_shapes=[pltpu.VMEM((ff, tn), BF16)],
        compiler_params=_cparams("parallel", "parallel", "arbitrary"),
        name="expert_down",
    )(hid, w_down)


COMBINE_TN = 512


def _combine_kernel(*refs, nq):
    ye_ref, slot_t_ref, gv_t_ref = refs[:3]
    o_ref = refs[-1]
    q = pl.program_id(1)
    ei = pl.program_id(2)
    n, e = slot_t_ref.shape[1:]
    capp, d = ye_ref.shape[2:]

    @pl.when(ei == 0)
    def _():
        o_ref[...] = jnp.zeros(o_ref.shape, F32)

    @pl.when(q < nq)
    def _():
        pick = (lax.broadcasted_iota(jnp.int32, (e, LANE), 0) == ei).astype(F32)
        slot_b = _dot(slot_t_ref[0], pick, precision=HI)
        gv_b = _dot(gv_t_ref[0], pick, precision=HI)
        lane = lax.broadcasted_iota(jnp.int32, (1, LANE), 1).astype(F32)
        pt = jnp.concatenate([jnp.where(slot_b == lane + float(c * LANE), 1.0, 0.0).astype(BF16)
                              for c in range(capp // LANE)], axis=1)
        tn = COMBINE_TN
        gv = jnp.concatenate([gv_b] * (tn // LANE), axis=1)
        for j in range(d // tn):
            o_ref[0, :, j * tn:(j + 1) * tn] += _dot(pt, ye_ref[0, 0, :, j * tn:(j + 1) * tn]) * gv


def _combine(ye, slot_t, gv_t, tq, blk0, t, prev=None):
    bsz, e, capp, d = ye.shape
    ntok = slot_t.shape[1]
    nq = ntok // tq
    n_steps = nq if prev is not None else pl.cdiv(t, tq) - blk0
    qc = lambda q: jnp.minimum(q, nq - 1)
    in_specs = [pl.BlockSpec((1, 1, capp, d), lambda b, q, ei: (b, jnp.where(q < nq, ei, e - 1), 0, 0)),
                pl.BlockSpec((1, tq, e), lambda b, q, ei: (b, qc(q), 0)),
                pl.BlockSpec((1, tq, e), lambda b, q, ei: (b, qc(q), 0))]
    args = [ye, slot_t, gv_t]
    aliases = {}
    if prev is not None:
        in_specs.append(pl.BlockSpec(memory_space=pl.ANY))
        args.append(prev)
        aliases = {3: 0}
    return pl.pallas_call(
        functools.partial(_combine_kernel, nq=nq),
        grid=(bsz, n_steps, e),
        in_specs=in_specs,
        out_specs=pl.BlockSpec((1, tq, d), lambda b, q, ei: (b, blk0 + q, 0)),
        out_shape=_sds((bsz, t, d), F32),
        input_output_aliases=aliases,
        compiler_params=_cparams("parallel", "parallel", "arbitrary", vmem=VMEM_LIMIT_BIG),
        name="combine",
    )(*args)


def _round_up(x, m):
    return (x + m - 1) // m * m


def _moe_part(aff_t, xm, blk, ntok, sps, tq, t_out, we, layer, prev=None, gather_src=None):
    e = aff_t.shape[1]
    cap = EC_CAPACITY * ntok // e
    capp = _round_up(cap, LANE)
    slot, slot_t, gv_t, idx = _topk(aff_t, blk, ntok, cap, capp)
    if gather_src is not None:
        assert blk == 0 and cap == capp
        hid = _expert_up_gather(idx, *gather_src, cap, we["gate"], we["up"], layer)
    else:
        hid = _expert_up(slot, xm, blk, ntok, capp, sps, we["gate"], we["up"], layer)
    ye = _expert_down(hid, we["down"], layer)
    return _combine(ye, slot_t, gv_t, tq, blk * ntok // tq, t_out, prev)


def kernel(x, c, ctx, c_ctx, w_mod, b_mod, w_in, b_in, conv_a_w, conv_a_b, lru_wa, lru_ba, lru_wx, lru_bx, lru_lam, gla_wa2, gla_ba, gla_norm_g, conv_c_w, conv_c_b, mlstm_norm_g, w_branch, w_out, ln1_g, ln1_b, w_router, w_e_gate, w_e_up, w_e_down, ln2_g, ln2_b):
    bsz, n_lat, d = x.shape
    nc = ctx.shape[1]
    t = n_lat + nc
    depth = w_mod.shape[0]
    d_rnn = conv_a_w.shape[-1]
    kg, vg = gla_ba.shape[-1], gla_norm_g.shape[-1]
    rank = gla_wa2.shape[2]
    wm = mlstm_norm_g.shape[-1]
    n_gate = 4 * MLSTM_HEADS
    n_exp = w_router.shape[-1]
    alpha = (2 * depth) ** 0.25
    assert 2 * rank + n_gate <= LANE and t % ROW_TILE == 0 and n_lat % ROW_TILE == 0 and bsz < SUBLANE

    sizes = (d_rnn, d_rnn, kg, kg, vg, vg, 2 * rank, wm, wm, wm, wm, n_gate)
    offs = [0]
    for sz in sizes:
        offs.append(offs[-1] + sz)
    n_feat = offs[-1]
    take_main = lambda a: jnp.concatenate([a[..., offs[0]:offs[6]], a[..., offs[7]:offs[11]]], axis=-1)
    take_small = lambda a: jnp.concatenate([a[..., offs[6]:offs[7]], a[..., offs[11]:offs[12]]], axis=-1)
    n_main = (offs[6] - offs[0]) + (offs[11] - offs[7])
    pad_s = LANE - (2 * rank + n_gate)
    mo = {"a_x": 0, "a_g": d_rnn, "b_q": 2 * d_rnn, "b_k": 2 * d_rnn + kg, "b_v": 2 * d_rnn + 2 * kg,
          "b_r": 2 * d_rnn + 2 * kg + vg}
    mo["c_q"] = mo["b_r"] + vg
    mo["c_k"], mo["c_v"], mo["c_o"] = mo["c_q"] + wm, mo["c_q"] + 2 * wm, mo["c_q"] + 3 * wm

    cvec = jnp.concatenate([c, c_ctx[None], jnp.zeros((SUBLANE - bsz - 1, d), F32)], axis=0)
    mod = _mod_all(cvec, w_mod, b_mod)
    xs = (x, ctx)

    tc = LANE
    while tc < 512 and all(v % (2 * tc) == 0 for v in (offs[6], 4 * wm, d)):
        tc *= 2
    w_in_t = jnp.swapaxes(w_in, 1, 2)
    w_main = _wprep(w_in_t, tc, 0, n_main // tc, ((0, offs[6] // tc, 0), (offs[6] // tc, n_main // tc, offs[7] - offs[6])),
                    "wprep_main")
    w_gate = _wprep(w_in_t, tc, n_main // tc, 3 * d // tc, ((0, 3 * d // tc, n_feat - n_main),), "wprep_gate")
    w_small = jnp.pad(take_small(w_in), ((0, 0), (0, 0), (0, pad_s)))
    w_br = w_branch.astype(BF16)
    w_rt = jnp.swapaxes(w_router, 1, 2).astype(BF16)
    we = {"gate": w_e_gate, "up": w_e_up, "down": w_e_down}

    mod6s = [mod[l].reshape(SUBLANE, 6, d) for l in range(depth)]
    h = _modulate(x, ctx, mod6s[0], 0)
    for l in range(depth):
        last = l == depth - 1
        mod6 = mod6s[l]
        b_main = take_main(b_in[l]).reshape(1, n_main)
        b_small = jnp.pad(take_small(b_in[l]), (0, pad_s)).reshape(1, LANE)
        bg = b_in[l][n_feat:].reshape(3, 1, d)
        lru_p = {"conv_w": conv_a_w[l], "conv_b": conv_a_b[l].reshape(1, d_rnn),
                 "wa": lru_wa[l].astype(BF16), "ba": lru_ba[l].reshape(2, 1, d_rnn),
                 "wx": lru_wx[l].astype(BF16), "bx": lru_bx[l].reshape(2, 1, d_rnn),
                 "lam": lru_lam[l].reshape(2, 1, d_rnn)}
        wlr = jnp.zeros((2, LANE, kg), F32)
        for dd in range(2):
            wlr = wlr.at[dd, dd * rank:(dd + 1) * rank].set(gla_wa2[l, dd])
        gla_p = {"wlr": wlr, "ba": gla_ba[l].reshape(2, 1, kg), "norm_g": gla_norm_g[l].reshape(1, vg)}
        ml_p = {"conv_w": conv_c_w[l], "conv_b": conv_c_b[l].reshape(1, 2 * wm), "norm_g": mlstm_norm_g[l].reshape(1, wm)}

        h = h.reshape(bsz * t, d)
        feat = _matmul(h, w_main, l, b_main, F32, 1024, 1024, "feat").reshape(bsz, t, n_main)
        feat_s = _matmul(h, w_small, l, b_small, F32, 1024, LANE, "feat_small").reshape(bsz, t, LANE)
        y0 = _rglru(feat, lru_p, d_rnn, n_lat)
        y1 = _gla(feat, feat_s, gla_p, (mo["b_q"], mo["b_k"], mo["b_v"], mo["b_r"]), n_lat)
        y2 = _mlstm(feat, feat_s, ml_p, (mo["c_q"], mo["c_k"], mo["c_v"], mo["c_o"]), 2 * rank, n_lat)
        ys = [y.reshape(bsz * t, -1) for y in (y0, y1, y2)]
        merged = _merge(h, w_gate, l, bg, ys, w_br)
        y = _matmul(merged, w_out, l, jnp.zeros((1, d), F32), F32, 1024, 512, "out_proj").reshape(bsz, t, d)
        xs, xm = _res_ln(xs, y, mod6, 2, ln1_g[l], ln1_b[l], alpha, n_lat, t, mod6, 3)

        aff_t = _router(xm, w_rt[l])
        t_out = n_lat if last else t
        f = _moe_part(aff_t, xm, 0, n_lat, 1, _pick_tile(n_lat, 1024), t_out, we, l, gather_src=(xs, mod6, 3))
        if not last:
            f = _moe_part(aff_t, xm, n_lat // nc, nc, bsz, nc, t_out, we, l, prev=f)
        if last:
            xs = _res_ln(xs, f, mod6, 5, ln2_g[l], ln2_b[l], alpha, n_lat, t_out)
        else:
            xs, h = _res_ln(xs, f, mod6, 5, ln2_g[l], ln2_b[l], alpha, n_lat, t_out, mod6s[l + 1], 0)
    return xs
```
